```python
import math
import jax
import jax.numpy as jnp
from jax import lax
import numpy as np

D_MODEL = 4096
BATCH = 4
SEQ = 2048
DEPTH = 1
DEC_BATCH = 32
DEC_SEQ = 4
PAST_LEN = 8192
PAGE_SIZE = 128

HEAD_DIM = 128
N_HEADS = D_MODEL // 256
ATT_WIDTH = N_HEADS * HEAD_DIM
SSM_WIDTH = D_MODEL // 2
SSM_GROUP = 16
N_GROUPS = SSM_WIDTH // SSM_GROUP
STATE_DIM = 64
MIX_WIDTH = ATT_WIDTH + SSM_WIDTH
IN_WIDTH = 4 * ATT_WIDTH + 2 * SSM_WIDTH
D_FF = 4 * D_MODEL
Q_BLOCK = 128
EPS = 1e-6
SB_BIAS_INIT = -8.0
SPLITS = (ATT_WIDTH, 2 * ATT_WIDTH, 3 * ATT_WIDTH, 4 * ATT_WIDTH, 4 * ATT_WIDTH + SSM_WIDTH)

kernel_name = "hybrid_s5_stickbreaking_decode_step"


def rmsnorm(x, w):
    xf = x.astype(jnp.float32)
    y = xf * lax.rsqrt(jnp.mean(xf * xf, axis=-1, keepdims=True) + EPS)
    return (y * w.astype(jnp.float32)).astype(x.dtype)


def _linear_combine(e1, e2):
    a1, b1 = e1
    a2, b2 = e2
    return a1 * a2, a2 * b1 + b2


def s5_branch(u, h0_re, h0_im, a_re, a_im, b_re, b_im, c_re, c_im, d, log_dt, glu_w, glu_b):
    f32 = jnp.float32
    b_, t_, _ = u.shape
    ug = u.astype(f32).reshape(b_, t_, N_GROUPS, SSM_GROUP)
    a = lax.complex(a_re.astype(f32), a_im.astype(f32))
    dt = jnp.exp(log_dt.astype(f32))[:, None]
    a_bar = jnp.exp(dt * a)
    b_bar = ((a_bar - 1.0) / a)[..., None] * lax.complex(b_re.astype(f32), b_im.astype(f32))
    bu = jnp.einsum('gpc,btgc->btgp', b_bar, ug)
    h0 = lax.complex(h0_re.astype(f32), h0_im.astype(f32))
    bu = bu.at[:, 0].add(a_bar * h0)
    a_seq = jnp.broadcast_to(a_bar, bu.shape)
    _, h = lax.associative_scan(_linear_combine, (a_seq, bu), axis=1)
    c = lax.complex(c_re.astype(f32), c_im.astype(f32))
    y = jnp.einsum('gcp,btgp->btgc', c, h).real + d.astype(f32) * ug
    z = jax.nn.gelu(y)
    gate = jax.nn.sigmoid(jnp.einsum('btgc,gce->btge', z, glu_w.astype(f32)) + glu_b.astype(f32))
    out = (z * gate).reshape(b_, t_, SSM_WIDTH).astype(u.dtype)
    h_last = h[:, -1]
    return out, h_last.real, h_last.imag


def stick_breaking(q, k, v, sb_bias, q_pos, k_pos):
    z = jnp.einsum('bqhd,bkhd->bhqk', q, k, preferred_element_type=jnp.float32) * (HEAD_DIM ** -0.5)
    z = z + sb_bias.astype(jnp.float32)[None, :, None, None]
    valid = k_pos[None, :] < q_pos[:, None]
    sp = jnp.where(valid, jax.nn.softplus(z), 0.0)
    suffix = lax.cumsum(sp, axis=3, reverse=True) - sp
    w = jnp.where(valid, jnp.exp(jax.nn.log_sigmoid(z) - suffix), 0.0)
    o = jnp.einsum('bhqk,bkhd->bqhd', w, v.astype(jnp.float32))
    return o.astype(q.dtype)


def stick_breaking_prompt(q, k, v, sb_bias):
    b_, t_, h_, dh = q.shape
    nb = t_ // Q_BLOCK
    qb = jnp.moveaxis(q.reshape(b_, nb, Q_BLOCK, h_, dh), 1, 0)
    k_pos = jnp.arange(t_)

    def one_block(args):
        q_blk, i = args
        q_pos = i * Q_BLOCK + jnp.arange(Q_BLOCK)
        return stick_breaking(q_blk, k, v, sb_bias, q_pos, k_pos)

    out = lax.map(one_block, (qb, jnp.arange(nb)))
    return jnp.moveaxis(out, 0, 1).reshape(b_, t_, h_, dh)


def hybrid_layer(x, h0_re, h0_im, past_k, past_v, norm_mix_w, w_in, sb_bias, a_re, a_im, b_re, b_im,
                 c_re, c_im, d, log_dt, glu_w, glu_b, w_out, norm_mlp_w, w_up, w_down):
    b_, t_, _ = x.shape
    h = rmsnorm(x, norm_mix_w)
    proj = h @ w_in
    q, k, v, g_att, u, g_ssm = jnp.split(proj, SPLITS, axis=-1)
    q = q.reshape(b_, t_, N_HEADS, HEAD_DIM)
    k = k.reshape(b_, t_, N_HEADS, HEAD_DIM)
    v = v.reshape(b_, t_, N_HEADS, HEAD_DIM)
    if past_k is None:
        att = stick_breaking_prompt(q, k, v, sb_bias)
    else:
        past = past_k.shape[1]
        k_all = jnp.concatenate([past_k.astype(k.dtype), k], axis=1)
        v_all = jnp.concatenate([past_v.astype(v.dtype), v], axis=1)
        att = stick_breaking(q, k_all, v_all, sb_bias, past + jnp.arange(t_), jnp.arange(past + t_))
    ssm_out, hT_re, hT_im = s5_branch(u, h0_re, h0_im, a_re, a_im, b_re, b_im, c_re, c_im,
                                      d, log_dt, glu_w, glu_b)
    mixed = jnp.concatenate([jax.nn.sigmoid(g_ssm) * ssm_out,
                             jax.nn.sigmoid(g_att) * att.reshape(b_, t_, ATT_WIDTH)], axis=-1)
    x = x + mixed @ w_out
    h2 = rmsnorm(x, norm_mlp_w)
    x = x + jnp.square(jax.nn.relu(h2 @ w_up)) @ w_down
    return x, k, v, hT_re, hT_im


def setup_inputs(seed: int = 0) -> dict:
    key = jax.random.key(seed)
    ks = jax.random.split(key, 32)
    f32 = jnp.float32
    n_pages = PAST_LEN // PAGE_SIZE
    n_used = DEC_BATCH * n_pages
    n_phys = n_used + n_used // 4

    def nrm(k, shape, scale):
        return jax.random.normal(k, shape, f32) * scale

    x_prompt = nrm(ks[0], (BATCH, SEQ, D_MODEL), 1.0)
    x_sample = nrm(ks[1], (DEC_BATCH, DEC_SEQ, D_MODEL), 1.0)
    cache_k = nrm(ks[2], (DEPTH, n_phys, PAGE_SIZE, N_HEADS, HEAD_DIM), 1.0)
    cache_v = nrm(ks[3], (DEPTH, n_phys, PAGE_SIZE, N_HEADS, HEAD_DIM), 1.0)
    state_ssm_re = nrm(ks[4], (DEPTH, DEC_BATCH, N_GROUPS, STATE_DIM), 0.1)
    state_ssm_im = nrm(ks[5], (DEPTH, DEC_BATCH, N_GROUPS, STATE_DIM), 0.1)
    page_table = jax.random.permutation(ks[6], n_phys)[:n_used].reshape(DEC_BATCH, n_pages).astype(jnp.int32)
    norm_mix_w = 1.0 + nrm(ks[7], (DEPTH, D_MODEL), 0.02)
    w_in = nrm(ks[8], (DEPTH, D_MODEL, IN_WIDTH), D_MODEL ** -0.5)
    sb_bias = SB_BIAS_INIT + nrm(ks[24], (DEPTH, N_HEADS), 0.1)
    ssm_a_re = -0.5 + nrm(ks[9], (DEPTH, N_GROUPS, STATE_DIM), 0.01)
    ssm_a_im = math.pi * jnp.arange(STATE_DIM, dtype=f32) + nrm(ks[10], (DEPTH, N_GROUPS, STATE_DIM), 0.01)
    ssm_b_re = nrm(ks[11], (DEPTH, N_GROUPS, STATE_DIM, SSM_GROUP), (2 * SSM_GROUP) ** -0.5)
    ssm_b_im = nrm(ks[12], (DEPTH, N_GROUPS, STATE_DIM, SSM_GROUP), (2 * SSM_GROUP) ** -0.5)
    ssm_c_re = nrm(ks[13], (DEPTH, N_GROUPS, SSM_GROUP, STATE_DIM), STATE_DIM ** -0.5)
    ssm_c_im = nrm(ks[14], (DEPTH, N_GROUPS, SSM_GROUP, STATE_DIM), STATE_DIM ** -0.5)
    ssm_d = nrm(ks[15], (DEPTH, N_GROUPS, SSM_GROUP), 1.0)
    ssm_log_dt = jax.random.uniform(ks[16], (DEPTH, N_GROUPS), f32, math.log(1e-3), math.log(1e-1))
    glu_w = nrm(ks[17], (DEPTH, N_GROUPS, SSM_GROUP, SSM_GROUP), SSM_GROUP ** -0.5)
    glu_b = nrm(ks[18], (DEPTH, N_GROUPS, SSM_GROUP), 0.01)
    w_out = nrm(ks[19], (DEPTH, MIX_WIDTH, D_MODEL), MIX_WIDTH ** -0.5)
    norm_mlp_w = 1.0 + nrm(ks[20], (DEPTH, D_MODEL), 0.02)
    w_up = nrm(ks[21], (DEPTH, D_MODEL, D_FF), D_MODEL ** -0.5)
    w_down = nrm(ks[22], (DEPTH, D_FF, D_MODEL), D_FF ** -0.5)
    norm_final_w = 1.0 + nrm(ks[23], (D_MODEL,), 0.02)
    return {"x_prompt": x_prompt, "x_sample": x_sample, "cache_k": cache_k, "cache_v": cache_v,
            "state_ssm_re": state_ssm_re, "state_ssm_im": state_ssm_im, "page_table": page_table,
            "norm_mix_w": norm_mix_w, "w_in": w_in, "sb_bias": sb_bias, "ssm_a_re": ssm_a_re,
            "ssm_a_im": ssm_a_im, "ssm_b_re": ssm_b_re, "ssm_b_im": ssm_b_im, "ssm_c_re": ssm_c_re,
            "ssm_c_im": ssm_c_im, "ssm_d": ssm_d, "ssm_log_dt": ssm_log_dt, "glu_w": glu_w, "glu_b": glu_b,
            "w_out": w_out, "norm_mlp_w": norm_mlp_w, "w_up": w_up, "w_down": w_down,
            "norm_final_w": norm_final_w}


def reference(x_prompt, x_sample, cache_k, cache_v, state_ssm_re, state_ssm_im, page_table,
              norm_mix_w, w_in, sb_bias, ssm_a_re, ssm_a_im, ssm_b_re, ssm_b_im, ssm_c_re, ssm_c_im,
              ssm_d, ssm_log_dt, glu_w, glu_b, w_out, norm_mlp_w, w_up, w_down, norm_final_w):
    dec_b = x_sample.shape[0]
    xp, xs = x_prompt, x_sample
    kp_l, vp_l, srp_l, sip_l, ks_l, vs_l, srs_l, sis_l = [], [], [], [], [], [], [], []
    for l in range(DEPTH):
        params = (norm_mix_w[l], w_in[l], sb_bias[l], ssm_a_re[l], ssm_a_im[l], ssm_b_re[l], ssm_b_im[l],
                  ssm_c_re[l], ssm_c_im[l], ssm_d[l], ssm_log_dt[l], glu_w[l], glu_b[l], w_out[l],
                  norm_mlp_w[l], w_up[l], w_down[l])
        zeros = jnp.zeros((xp.shape[0], N_GROUPS, STATE_DIM), jnp.float32)
        xp, kp, vp, srp, sip = hybrid_layer(xp, zeros, zeros, None, None, *params)
        past_k = cache_k[l][page_table].reshape(dec_b, -1, N_HEADS, HEAD_DIM)
        past_v = cache_v[l][page_table].reshape(dec_b, -1, N_HEADS, HEAD_DIM)
        xs, ksn, vsn, srs, sis = hybrid_layer(xs, state_ssm_re[l], state_ssm_im[l], past_k, past_v, *params)
        kp_l.append(kp); vp_l.append(vp); srp_l.append(srp); sip_l.append(sip)
        ks_l.append(ksn); vs_l.append(vsn); srs_l.append(srs); sis_l.append(sis)
    y_prompt = rmsnorm(xp, norm_final_w)
    y_sample = rmsnorm(xs, norm_final_w)
    return (y_prompt, y_sample, jnp.stack(kp_l), jnp.stack(vp_l), jnp.stack(srp_l), jnp.stack(sip_l),
            jnp.stack(ks_l), jnp.stack(vs_l), jnp.stack(srs_l), jnp.stack(sis_l))
```

```python
import functools
import math

import jax
import jax.numpy as jnp
from jax import lax
from jax.experimental import pallas as pl
from jax.experimental.pallas import tpu as pltpu

F32 = jnp.float32
BF16 = jnp.bfloat16

D_MODEL = 4096
HEAD_DIM = 128
N_HEADS = 16
ATT_WIDTH = N_HEADS * HEAD_DIM
SSM_WIDTH = 2048
SSM_GROUP = 16
N_GROUPS = 128
STATE_DIM = 64
IN_WIDTH = 4 * ATT_WIDTH + 2 * SSM_WIDTH
D_FF = 4 * D_MODEL
PAGE_SIZE = 128
EPS = 1e-6

LANES = 128
SUBLANES = 8
GROUPS_PER_BLOCK = LANES // SSM_GROUP
N_GBLOCKS = N_GROUPS // GROUPS_PER_BLOCK
STATE_LANES = GROUPS_PER_BLOCK * STATE_DIM
VMEM_LIMIT = 56 * 1024 * 1024


def _cparams(sem):
    return pltpu.CompilerParams(dimension_semantics=sem, vmem_limit_bytes=VMEM_LIMIT)


def _rms_rows(x, w):
    return x * lax.rsqrt(jnp.mean(x * x, axis=-1, keepdims=True) + EPS) * w


def _mm_body(*refs, norm, epilogue, nk):
    it = iter(refs)
    x_ref = next(it)
    nw_ref = next(it) if norm else None
    w_ref = next(it)
    res_ref = next(it) if epilogue == "res" else None
    o_ref = next(it)
    xs_ref = next(it) if norm else None
    acc_ref = next(it) if nk > 1 else None
    j = pl.program_id(1)
    k = pl.program_id(2)

    if norm:
        @pl.when(j == 0)
        def _():
            xs_ref[...] = _rms_rows(x_ref[...], nw_ref[...]).astype(BF16)
        lhs = xs_ref[...]
    else:
        lhs = x_ref[...]
    part = jnp.dot(lhs, w_ref[...], preferred_element_type=F32)

    def finish(acc):
        if epilogue == "relu2":
            r = jnp.maximum(acc, 0.0)
            acc = r * r
        elif epilogue == "res":
            acc = acc + res_ref[...]
        o_ref[...] = acc.astype(o_ref.dtype)

    if nk == 1:
        finish(part)
    else:
        @pl.when(k == 0)
        def _():
            acc_ref[...] = part

        @pl.when(k > 0)
        def _():
            acc_ref[...] += part

        @pl.when(k == nk - 1)
        def _():
            finish(acc_ref[...])


def _matmul(x, w, *, bm, bn, bk, norm_w=None, res=None, epilogue="none", out_dtype=F32):
    m, kdim = x.shape
    n = w.shape[1]
    bm = min(bm, m)
    nk = kdim // bk
    norm = norm_w is not None
    assert not (norm and nk != 1)
    in_specs = [pl.BlockSpec((bm, bk), lambda i, j, k: (i, k))]
    args = [x]
    if norm:
        in_specs.append(pl.BlockSpec((1, kdim), lambda i, j, k: (0, 0)))
        args.append(norm_w.reshape(1, kdim))
    in_specs.append(pl.BlockSpec((bk, bn), lambda i, j, k: (k, j)))
    args.append(w)
    if epilogue == "res":
        in_specs.append(pl.BlockSpec((bm, bn), lambda i, j, k: (i, j)))
        args.append(res)
    scratch = []
    if norm:
        scratch.append(pltpu.VMEM((bm, kdim), BF16))
    if nk > 1:
        scratch.append(pltpu.VMEM((bm, bn), F32))
    return pl.pallas_call(
        functools.partial(_mm_body, norm=norm, epilogue=epilogue, nk=nk),
        grid=(m // bm, n // bn, nk),
        in_specs=in_specs,
        out_specs=pl.BlockSpec((bm, bn), lambda i, j, k: (i, j)),
        out_shape=jax.ShapeDtypeStruct((m, n), out_dtype),
        scratch_shapes=scratch,
        compiler_params=_cparams(("parallel", "arbitrary", "arbitrary")),
    )(*args)


def _rmsnorm_body(x_ref, w_ref, o_ref):
    o_ref[...] = _rms_rows(x_ref[...], w_ref[...])


def _rmsnorm(x, w, *, bm=256):
    m, d = x.shape
    bm = min(bm, m)
    return pl.pallas_call(
        _rmsnorm_body,
        grid=(m // bm,),
        in_specs=[pl.BlockSpec((bm, d), lambda i: (i, 0)), pl.BlockSpec((1, d), lambda i: (0, 0))],
        out_specs=pl.BlockSpec((bm, d), lambda i: (i, 0)),
        out_shape=jax.ShapeDtypeStruct((m, d), F32),
        compiler_params=_cparams(("parallel",)),
    )(x, w.reshape(1, d))


def _softplus(z):
    return jnp.maximum(z, 0.0) + jnp.log1p(jnp.exp(-jnp.abs(z)))


def _attn_prompt_body(bias_ref, q_ref, k_ref, v_ref, g_ref, o_ref, ko_ref, vo_ref, kb_ref, vb_ref, *, tq):
    h = pl.program_id(1)
    qi = pl.program_id(2)

    @pl.when(qi == 0)
    def _():
        k = k_ref[...]
        v = v_ref[...]
        ko_ref[...] = k
        vo_ref[...] = v
        kb_ref[...] = k.astype(BF16)
        vb_ref[...] = v.astype(BF16)

    bias = bias_ref[h]
    q = (q_ref[...] * (HEAD_DIM ** -0.5)).astype(BF16)
    row = lax.broadcasted_iota(jnp.int32, (tq, tq), 0)
    col = lax.broadcasted_iota(jnp.int32, (tq, tq), 1)
    later = (row > col).astype(BF16)
    valid = col < row

    def block(kb, carry, acc, masked):
        start = pl.multiple_of(kb * tq, tq)
        kblk = kb_ref[pl.ds(start, tq), :]
        vblk = vb_ref[pl.ds(start, tq), :]
        z = lax.dot_general(q, kblk, (((1,), (1,)), ((), ())), preferred_element_type=F32) + bias
        sp = _softplus(z)
        if masked:
            sp = jnp.where(valid, sp, 0.0)
        inner = jnp.dot(sp.astype(BF16), later, preferred_element_type=F32)
        w = jnp.exp(z - sp - inner - carry)
        if masked:
            w = jnp.where(valid, w, 0.0)
        acc = acc + jnp.dot(w.astype(BF16), vblk, preferred_element_type=F32)
        carry = carry + jnp.sum(sp, axis=1, keepdims=True)
        return carry, acc

    carry, acc = block(qi, jnp.zeros((tq, 1), F32), jnp.zeros((tq, HEAD_DIM), F32), True)

    def body(it, c):
        return block(qi - 1 - it, c[0], c[1], False)

    carry, acc = lax.fori_loop(0, qi, body, (carry, acc))
    o_ref[...] = (acc * jax.nn.sigmoid(g_ref[...])).astype(o_ref.dtype)


def _attn_prompt(proj, sb_bias, *, tq=256):
    b, t, _ = proj.shape
    nq = t // tq
    qspec = lambda off: pl.BlockSpec((None, tq, HEAD_DIM), lambda bi, h, qi: (bi, qi, off + h))
    kvspec = lambda off: pl.BlockSpec((None, t, HEAD_DIM), lambda bi, h, qi: (bi, 0, off + h))
    return pl.pallas_call(
        functools.partial(_attn_prompt_body, tq=tq),
        grid=(b, N_HEADS, nq),
        in_specs=[pl.BlockSpec(memory_space=pltpu.SMEM),
                  qspec(0), kvspec(N_HEADS), kvspec(2 * N_HEADS), qspec(3 * N_HEADS)],
        out_specs=[qspec(0), kvspec(0), kvspec(0)],
        out_shape=[jax.ShapeDtypeStruct((b, t, ATT_WIDTH), BF16),
                   jax.ShapeDtypeStruct((b, t, ATT_WIDTH), F32),
                   jax.ShapeDtypeStruct((b, t, ATT_WIDTH), F32)],
        scratch_shapes=[pltpu.VMEM((t, HEAD_DIM), BF16), pltpu.VMEM((t, HEAD_DIM), BF16)],
        compiler_params=_cparams(("parallel", "parallel", "arbitrary")),
    )(sb_bias, proj, proj, proj, proj)


def _attn_sample_body(pt_ref, q_ref, kn_ref, vn_ref, g_ref, bias_ref, *refs, pages_per_step, n_tok):
    kp_refs = refs[:pages_per_step]
    vp_refs = refs[pages_per_step:2 * pages_per_step]
    o_ref = refs[2 * pages_per_step]
    qbd_ref, acc_ref, carry_ref = refs[2 * pages_per_step + 1:]
    c = pl.program_id(1)
    n_rows = n_tok * N_HEADS
    bias = bias_ref[...]
    row = lax.broadcasted_iota(jnp.int32, (PAGE_SIZE, PAGE_SIZE), 0)
    col = lax.broadcasted_iota(jnp.int32, (PAGE_SIZE, PAGE_SIZE), 1)
    later = (col > row).astype(BF16)

    def page(kp, vp, qbd, carry, acc, valid):
        z = lax.dot_general(kp, qbd, (((1,), (1,)), ((), ())), preferred_element_type=F32) + bias
        sp = _softplus(z)
        if valid is not None:
            sp = jnp.where(valid, sp, 0.0)
        inner = jnp.dot(later, sp.astype(BF16), preferred_element_type=F32)
        w = jnp.exp(z - sp - inner - carry)
        if valid is not None:
            w = jnp.where(valid, w, 0.0)
        acc = acc + lax.dot_general(w.astype(BF16), vp, (((0,), (0,)), ((), ())),
                                    preferred_element_type=F32)
        return carry + jnp.sum(sp, axis=0, keepdims=True), acc

    @pl.when(c == 0)
    def _():
        q = q_ref[...] * (HEAD_DIM ** -0.5)
        qt = jnp.concatenate([q] * N_HEADS, axis=1)
        rh = lax.broadcasted_iota(jnp.int32, (n_rows, ATT_WIDTH), 0) % N_HEADS
        lh = lax.broadcasted_iota(jnp.int32, (n_rows, ATT_WIDTH), 1) // HEAD_DIM
        qbd = jnp.where(rh == lh, qt, 0.0).astype(BF16)
        qbd_ref[...] = qbd
        pad = jnp.zeros((PAGE_SIZE - kn_ref.shape[0], ATT_WIDTH), BF16)
        kn = jnp.concatenate([kn_ref[...].astype(BF16), pad], axis=0)
        vn = jnp.concatenate([vn_ref[...].astype(BF16), pad], axis=0)
        key = lax.broadcasted_iota(jnp.int32, (PAGE_SIZE, n_rows), 0)
        qtok = lax.broadcasted_iota(jnp.int32, (PAGE_SIZE, n_rows), 1) // N_HEADS
        carry, acc = page(kn, vn, qbd, jnp.zeros((1, n_rows), F32), jnp.zeros((n_rows, ATT_WIDTH), F32),
                          key < qtok)
        carry_ref[...] = carry
        acc_ref[...] = acc

    qbd = qbd_ref[...]
    carry = carry_ref[...]
    acc = acc_ref[...]
    for i in range(pages_per_step):
        carry, acc = page(kp_refs[i][...].astype(BF16), vp_refs[i][...].astype(BF16), qbd, carry, acc, None)
    carry_ref[...] = carry
    acc_ref[...] = acc

    @pl.when(c == pl.num_programs(1) - 1)
    def _():
        rh = lax.broadcasted_iota(jnp.int32, (n_rows, HEAD_DIM), 0) % N_HEADS
        out = jnp.zeros((n_rows, HEAD_DIM), F32)
        for h in range(N_HEADS):
            out = out + jnp.where(rh == h, acc[:, h * HEAD_DIM:(h + 1) * HEAD_DIM], 0.0)
        o_ref[...] = out * jax.nn.sigmoid(g_ref[...])


def _attn_sample(proj_s, cache_k, cache_v, page_table, sb_bias, *, pages_per_step=4):
    b, n_tok, _ = proj_s.shape
    n_pages = page_table.shape[1]
    n_rows = n_tok * N_HEADS
    steps = n_pages // pages_per_step
    q_rows = proj_s[:, :, :ATT_WIDTH].reshape(b, n_rows, HEAD_DIM)
    g_rows = proj_s[:, :, 3 * ATT_WIDTH:4 * ATT_WIDTH].reshape(b, n_rows, HEAD_DIM)
    bias_row = jnp.tile(sb_bias, n_tok).reshape(1, n_rows)
    n_new = 2 * SUBLANES
    pad_new = lambda a: jnp.pad(a, ((0, 0), (0, n_new - n_tok), (0, 0)))
    k_new = pad_new(proj_s[:, :, ATT_WIDTH:2 * ATT_WIDTH])
    v_new = pad_new(proj_s[:, :, 2 * ATT_WIDTH:3 * ATT_WIDTH])

    def page_spec(i):
        def imap(bi, c, pt):
            return (pt[bi, n_pages - 1 - (c * pages_per_step + i)], 0, 0)
        return pl.BlockSpec((None, PAGE_SIZE, ATT_WIDTH), imap)

    rows_spec = pl.BlockSpec((None, n_rows, HEAD_DIM), lambda bi, c, pt: (bi, 0, 0))
    new_spec = pl.BlockSpec((None, n_new, ATT_WIDTH), lambda bi, c, pt: (bi, 0, 0))
    grid_spec = pltpu.PrefetchScalarGridSpec(
        num_scalar_prefetch=1,
        grid=(b, steps),
        in_specs=[rows_spec, new_spec, new_spec, rows_spec,
                  pl.BlockSpec((1, n_rows), lambda bi, c, pt: (0, 0))]
                 + [page_spec(i) for i in range(pages_per_step)] * 2,
        out_specs=rows_spec,
        scratch_shapes=[pltpu.VMEM((n_rows, ATT_WIDTH), BF16),
                        pltpu.VMEM((n_rows, ATT_WIDTH), F32),
                        pltpu.VMEM((1, n_rows), F32)],
    )
    out = pl.pallas_call(
        functools.partial(_attn_sample_body, pages_per_step=pages_per_step, n_tok=n_tok),
        grid_spec=grid_spec,
        out_shape=jax.ShapeDtypeStruct((b, n_rows, HEAD_DIM), F32),
        compiler_params=_cparams(("parallel", "arbitrary")),
    )(page_table, q_rows, k_new, v_new, g_rows, bias_row,
      *([cache_k] * pages_per_step), *([cache_v] * pages_per_step))
    return out.reshape(b, n_tok, ATT_WIDTH)


def _ssm_params_body(are_ref, aim_ref, ldt_ref, bre_ref, bim_ref, abr_ref, abi_ref, bbr_ref, bbi_ref):
    a_re = are_ref[...]
    a_im = aim_ref[...]
    dt = jnp.exp(ldt_ref[...])
    mag = jnp.exp(dt * a_re)
    ab_re = mag * jnp.cos(dt * a_im)
    ab_im = mag * jnp.sin(dt * a_im)
    abr_ref[...] = ab_re
    abi_ref[...] = ab_im
    den = a_re * a_re + a_im * a_im
    n_re = ab_re - 1.0
    co_re = (n_re * a_re + ab_im * a_im) / den
    co_im = (ab_im * a_re - n_re * a_im) / den
    b_re = bre_ref[...]
    b_im = bim_ref[...]
    bbr_ref[...] = co_re[:, None, :] * b_re - co_im[:, None, :] * b_im
    bbi_ref[...] = co_re[:, None, :] * b_im + co_im[:, None, :] * b_re


def _ssm_params(a_re, a_im, log_dt, b_re, b_im):
    g, p = a_re.shape
    c = b_re.shape[1]
    return pl.pallas_call(
        _ssm_params_body,
        out_shape=[jax.ShapeDtypeStruct((g, p), F32), jax.ShapeDtypeStruct((g, p), F32),
                   jax.ShapeDtypeStruct((g, c, p), F32), jax.ShapeDtypeStruct((g, c, p), F32)],
    )(a_re, a_im, log_dt.reshape(g, 1), b_re, b_im)


def _gelu_tanh(y):
    return 0.5 * y * (1.0 + jnp.tanh(math.sqrt(2.0 / math.pi) * (y + 0.044715 * (y * y * y))))


def _ssm_prompt_body(u_ref, g_ref, b_ref, c_ref, gw_ref, ab_ref, dg_ref, o_ref, hT_ref, bu_ref, hh_ref, st_ref,
                     *, lc):
    ci = pl.program_id(1)
    rows = lc * SUBLANES
    n_slab = STATE_LANES // LANES

    @pl.when(ci == 0)
    def _():
        st_ref[...] = jnp.zeros_like(st_ref)

    u = u_ref[...].reshape(rows, LANES)
    first = (lax.broadcasted_iota(jnp.int32, (rows, LANES), 0) % SUBLANES) < (SUBLANES // 2)
    lhs = jnp.concatenate([jnp.where(first, u, 0.0), jnp.where(first, 0.0, u)], axis=1).astype(BF16)
    bu_ref[...] = jnp.dot(lhs, b_ref[...], preferred_element_type=F32)

    ar = [ab_ref[j] for j in range(n_slab)]
    ai = [ab_ref[n_slab + j] for j in range(n_slab)]

    def step(t, h):
        r0 = pl.multiple_of(t * SUBLANES, SUBLANES)
        new = []
        for j in range(n_slab):
            hr, hi = h[j], h[n_slab + j]
            bre = bu_ref[pl.ds(r0, SUBLANES), j * LANES:(j + 1) * LANES]
            bim = bu_ref[pl.ds(r0, SUBLANES), STATE_LANES + j * LANES:STATE_LANES + (j + 1) * LANES]
            nr = ar[j] * hr - ai[j] * hi + bre
            ni = ar[j] * hi + ai[j] * hr + bim
            hh_ref[pl.ds(r0, SUBLANES), j * LANES:(j + 1) * LANES] = nr
            hh_ref[pl.ds(r0, SUBLANES), STATE_LANES + j * LANES:STATE_LANES + (j + 1) * LANES] = ni
            new.append((nr, ni))
        return tuple(x[0] for x in new) + tuple(x[1] for x in new)

    h0 = tuple(st_ref[j] for j in range(2 * n_slab))
    hN = lax.fori_loop(0, lc, step, h0, unroll=8)
    for j in range(2 * n_slab):
        st_ref[j] = hN[j]
        hT_ref[j] = hN[j]

    y2 = jnp.dot(hh_ref[...].astype(BF16), c_ref[...], preferred_element_type=F32)
    d = jnp.broadcast_to(dg_ref[0][None], (lc, SUBLANES, LANES)).reshape(rows, LANES)
    gb = jnp.broadcast_to(dg_ref[1][None], (lc, SUBLANES, LANES)).reshape(rows, LANES)
    y = jnp.where(first, y2[:, :LANES], y2[:, LANES:]) + d * u
    z = _gelu_tanh(y)
    g2 = jnp.dot(z.astype(BF16), gw_ref[...], preferred_element_type=F32)
    gate = jax.nn.sigmoid(jnp.where(first, g2[:, :LANES], g2[:, LANES:]) + gb)
    out = z * gate * jax.nn.sigmoid(g_ref[...].reshape(rows, LANES))
    o_ref[...] = out.reshape(lc, SUBLANES, LANES)


def _ssm_prompt(u_tm, g_tm, b_st, c_st, gw_st, ab_tm, dg_tm, *, lc=256):
    npair, t = u_tm.shape[:2]
    n_slab2 = 2 * STATE_LANES // LANES
    seq_spec = pl.BlockSpec((None, lc, SUBLANES, LANES), lambda p, c: (p, c, 0, 0))
    par_spec = lambda shp: pl.BlockSpec((None,) + shp, lambda p, c: (p,) + (0,) * len(shp))
    return pl.pallas_call(
        functools.partial(_ssm_prompt_body, lc=lc),
        grid=(npair, t // lc),
        in_specs=[seq_spec, seq_spec, par_spec((2 * LANES, 2 * STATE_LANES)), par_spec((2 * STATE_LANES, 2 * LANES)),
                  par_spec((LANES, 2 * LANES)), par_spec((n_slab2, SUBLANES, LANES)),
                  par_spec((2, SUBLANES, LANES))],
        out_specs=[seq_spec, par_spec((n_slab2, SUBLANES, LANES))],
        out_shape=[jax.ShapeDtypeStruct(u_tm.shape, F32),
                   jax.ShapeDtypeStruct((npair, n_slab2, SUBLANES, LANES), F32)],
        scratch_shapes=[pltpu.VMEM((lc * SUBLANES, 2 * STATE_LANES), F32),
                        pltpu.VMEM((lc * SUBLANES, 2 * STATE_LANES), F32),
                        pltpu.VMEM((n_slab2, SUBLANES, LANES), F32)],
        compiler_params=_cparams(("parallel", "arbitrary")),
    )(u_tm, g_tm, b_st, c_st, gw_st, ab_tm, dg_tm)


def _ssm_sample_body(u_ref, g_ref, hre_ref, him_ref, b_ref, c_ref, gw_ref, ab_ref, dg_ref,
                     o_ref, ore_ref, oim_ref, *, n_tok, nb):
    hp = lax.Precision.HIGHEST
    u = u_ref[...].reshape(n_tok * nb, LANES)
    bu = jnp.dot(u, b_ref[...], preferred_element_type=F32, precision=hp)
    ar = ab_ref[0:1, :]
    ai = ab_ref[1:2, :]
    hr = hre_ref[...]
    hi = him_ref[...]
    hs = []
    for t in range(n_tok):
        bre = bu[t * nb:(t + 1) * nb, :STATE_LANES]
        bim = bu[t * nb:(t + 1) * nb, STATE_LANES:]
        hr, hi = ar * hr - ai * hi + bre, ar * hi + ai * hr + bim
        hs.append(jnp.concatenate([hr, hi], axis=1))
    ore_ref[...] = hr
    oim_ref[...] = hi
    hh = jnp.concatenate(hs, axis=0)
    y = jnp.dot(hh, c_ref[...], preferred_element_type=F32, precision=hp) + dg_ref[0:1, :] * u
    z = _gelu_tanh(y)
    gate = jax.nn.sigmoid(jnp.dot(z, gw_ref[...], preferred_element_type=F32, precision=hp) + dg_ref[1:2, :])
    out = z * gate * jax.nn.sigmoid(g_ref[...].reshape(n_tok * nb, LANES))
    o_ref[...] = out.reshape(n_tok, nb, LANES)


def _ssm_sample(u_t, g_t, h_re, h_im, b_blk, c_blk, gw_blk, ab_row, dg_row):
    n_tok, nb, _ = u_t.shape
    seq_spec = pl.BlockSpec((n_tok, nb, LANES), lambda gb: (0, 0, gb))
    st_spec = pl.BlockSpec((nb, STATE_LANES), lambda gb: (0, gb))
    par_spec = lambda shp: pl.BlockSpec((None,) + shp, lambda gb: (gb,) + (0,) * len(shp))
    return pl.pallas_call(
        functools.partial(_ssm_sample_body, n_tok=n_tok, nb=nb),
        grid=(N_GBLOCKS,),
        in_specs=[seq_spec, seq_spec, st_spec, st_spec,
                  par_spec((LANES, 2 * STATE_LANES)), par_spec((2 * STATE_LANES, LANES)),
                  par_spec((LANES, LANES)), par_spec((2, STATE_LANES)), par_spec((2, LANES))],
        out_specs=[seq_spec, st_spec, st_spec],
        out_shape=[jax.ShapeDtypeStruct(u_t.shape, F32),
                   jax.ShapeDtypeStruct(h_re.shape, F32), jax.ShapeDtypeStruct(h_im.shape, F32)],
        compiler_params=_cparams(("parallel",)),
    )(u_t, g_t, h_re, h_im, b_blk, c_blk, gw_blk, ab_row, dg_row)


def _ssm_block_weights(ab_re, ab_im, bb_re, bb_im, c_re, c_im, d, glu_w, glu_b):
    nb, gpb = N_GBLOCKS, GROUPS_PER_BLOCK
    eye = jnp.eye(gpb, dtype=F32)
    bb = jnp.stack([bb_re, bb_im], axis=2).reshape(nb, gpb, SSM_GROUP, 2, STATE_DIM)
    b_blk = jnp.einsum('bgcrp,gh->bgcrhp', bb, eye).reshape(nb, LANES, 2 * STATE_LANES)
    cc = jnp.stack([c_re, -c_im], axis=2).reshape(nb, gpb, SSM_GROUP, 2, STATE_DIM)
    c_blk = jnp.einsum('bgcrp,gh->brgphc', cc, eye).reshape(nb, 2 * STATE_LANES, LANES)
    gw = glu_w.reshape(nb, gpb, SSM_GROUP, SSM_GROUP)
    gw_blk = jnp.einsum('bgce,gh->bgche', gw, eye).reshape(nb, LANES, LANES)
    ab_row = jnp.stack([ab_re.reshape(nb, STATE_LANES), ab_im.reshape(nb, STATE_LANES)], axis=1)
    dg_row = jnp.stack([d.reshape(nb, LANES), glu_b.reshape(nb, LANES)], axis=1)
    return b_blk, c_blk, gw_blk, ab_row, dg_row


def _pair_weights(b_blk, c_blk, gw_blk, ab_row, dg_row, nbatch):
    npair = N_GBLOCKS // 2
    n_slab = STATE_LANES // LANES
    b_st = b_blk.reshape(npair, 2 * LANES, 2 * STATE_LANES).astype(BF16)
    c_st = c_blk.reshape(npair, 2, 2 * STATE_LANES, LANES).transpose(0, 2, 1, 3).reshape(
        npair, 2 * STATE_LANES, 2 * LANES).astype(BF16)
    gw_st = gw_blk.reshape(npair, 2, LANES, LANES).transpose(0, 2, 1, 3).reshape(
        npair, LANES, 2 * LANES).astype(BF16)
    ab = ab_row.reshape(npair, 2, 2, n_slab, LANES).transpose(0, 2, 3, 1, 4)
    ab_tm = jnp.repeat(ab, nbatch, axis=3).reshape(npair, 2 * n_slab, 2 * nbatch, LANES)
    dg = dg_row.reshape(npair, 2, 2, LANES).transpose(0, 2, 1, 3)
    dg_tm = jnp.repeat(dg, nbatch, axis=2)
    return b_st, c_st, gw_st, ab_tm, dg_tm


def _dense_tail(x2d, mixed, w_out, norm_mlp_w, w_up, w_down, norm_final_w):
    x1 = _matmul(mixed, w_out, bm=1024, bn=1024, bk=D_MODEL, res=x2d, epilogue="res")
    hid = _matmul(x1, w_up, bm=512, bn=1024, bk=D_MODEL, norm_w=norm_mlp_w, epilogue="relu2", out_dtype=BF16)
    x2 = _matmul(hid, w_down, bm=1024, bn=1024, bk=2048, res=x1, epilogue="res")
    return _rmsnorm(x2, norm_final_w)


def kernel(x_prompt, x_sample, cache_k, cache_v, state_ssm_re, state_ssm_im, page_table, norm_mix_w, w_in, sb_bias,
           ssm_a_re, ssm_a_im, ssm_b_re, ssm_b_im, ssm_c_re, ssm_c_im, ssm_d, ssm_log_dt, glu_w, glu_b, w_out,
           norm_mlp_w, w_up, w_down, norm_final_w):
    depth = w_in.shape[0]
    assert depth == 1
    nb_p, t_p, _ = x_prompt.shape
    nb_s, t_s, _ = x_sample.shape
    assert nb_p * 2 == SUBLANES
    l = 0
    w_in_b = w_in[l].astype(BF16)
    w_out_b = w_out[l].astype(BF16)
    w_up_b = w_up[l].astype(BF16)
    w_down_b = w_down[l].astype(BF16)

    ab_re, ab_im, bb_re, bb_im = _ssm_params(ssm_a_re[l], ssm_a_im[l], ssm_log_dt[l],
                                             ssm_b_re[l].transpose(0, 2, 1), ssm_b_im[l].transpose(0, 2, 1))
    blk = _ssm_block_weights(ab_re, ab_im, bb_re, bb_im, ssm_c_re[l], ssm_c_im[l], ssm_d[l], glu_w[l], glu_b[l])
    pair = _pair_weights(*blk, nbatch=nb_p)

    xp = x_prompt.reshape(nb_p * t_p, D_MODEL)
    proj_p = _matmul(xp, w_in_b, bm=512, bn=1024, bk=D_MODEL, norm_w=norm_mix_w[l])
    proj_p3 = proj_p.reshape(nb_p, t_p, IN_WIDTH)
    att_p, k_p, v_p = _attn_prompt(proj_p3, sb_bias[l])

    npair = N_GBLOCKS // 2

    def to_tm(cols):
        return cols.reshape(nb_p, t_p, npair, 2, LANES).transpose(2, 1, 3, 0, 4).reshape(npair, t_p, SUBLANES, LANES)

    u_tm = to_tm(proj_p3[:, :, 4 * ATT_WIDTH:4 * ATT_WIDTH + SSM_WIDTH])
    g_tm = to_tm(proj_p3[:, :, 4 * ATT_WIDTH + SSM_WIDTH:])
    ssm_tm, hT = _ssm_prompt(u_tm, g_tm, *pair)
    ssm_p = ssm_tm.reshape(npair, t_p, 2, nb_p, LANES).transpose(3, 1, 0, 2, 4).reshape(nb_p * t_p, SSM_WIDTH)
    n_slab = STATE_LANES // LANES
    hT = hT.reshape(npair, 2, n_slab, 2, nb_p, 2, STATE_DIM).transpose(1, 4, 0, 3, 2, 5, 6).reshape(
        2, nb_p, N_GROUPS, STATE_DIM)
    mixed_p = jnp.concatenate([ssm_p.astype(BF16), att_p.reshape(nb_p * t_p, ATT_WIDTH)], axis=1)
    y_p = _dense_tail(xp, mixed_p, w_out_b, norm_mlp_w[l], w_up_b, w_down_b, norm_final_w)

    xs = x_sample.reshape(nb_s * t_s, D_MODEL)
    proj_s = _matmul(xs, w_in_b, bm=128, bn=1024, bk=D_MODEL, norm_w=norm_mix_w[l])
    proj_s3 = proj_s.reshape(nb_s, t_s, IN_WIDTH)
    n_phys = cache_k.shape[1]
    att_s = _attn_sample(proj_s3, cache_k[l].reshape(n_phys, PAGE_SIZE, ATT_WIDTH),
                         cache_v[l].reshape(n_phys, PAGE_SIZE, ATT_WIDTH), page_table, sb_bias[l])
    u_t = proj_s3[:, :, 4 * ATT_WIDTH:4 * ATT_WIDTH + SSM_WIDTH].transpose(1, 0, 2)
    g_t = proj_s3[:, :, 4 * ATT_WIDTH + SSM_WIDTH:].transpose(1, 0, 2)
    ssm_t, hs_re, hs_im = _ssm_sample(u_t, g_t, state_ssm_re[l].reshape(nb_s, N_GROUPS * STATE_DIM),
                                      state_ssm_im[l].reshape(nb_s, N_GROUPS * STATE_DIM), *blk)
    ssm_s = ssm_t.transpose(1, 0, 2).reshape(nb_s * t_s, SSM_WIDTH)
    mixed_s = jnp.concatenate([ssm_s, att_s.reshape(nb_s * t_s, ATT_WIDTH)], axis=1).astype(BF16)
    y_s = _dense_tail(xs, mixed_s, w_out_b, norm_mlp_w[l], w_up_b, w_down_b, norm_final_w)

    kv_shape_p = (1, nb_p, t_p, N_HEADS, HEAD_DIM)
    kv_shape_s = (1, nb_s, t_s, N_HEADS, HEAD_DIM)
    return (y_p.reshape(nb_p, t_p, D_MODEL), y_s.reshape(nb_s, t_s, D_MODEL),
            k_p.reshape(kv_shape_p), v_p.reshape(kv_shape_p),
            hT[0][None], hT[1][None],
            proj_s3[:, :, ATT_WIDTH:2 * ATT_WIDTH].reshape(kv_shape_s),
            proj_s3[:, :, 2 * ATT_WIDTH:3 * ATT_WIDTH].reshape(kv_shape_s),
            hs_re.reshape(1, nb_s, N_GROUPS, STATE_DIM), hs_im.reshape(1, nb_s, N_GROUPS, STATE_DIM))
```

```python
import functools
import math

import jax
import jax.numpy as jnp
from jax import lax
from jax.experimental import pallas as pl
from jax.experimental.pallas import tpu as pltpu

F32 = jnp.float32
BF16 = jnp.bfloat16

D_MODEL = 4096
HEAD_DIM = 128
N_HEADS = 16
ATT_WIDTH = N_HEADS * HEAD_DIM
SSM_WIDTH = 2048
SSM_GROUP = 16
N_GROUPS = 128
STATE_DIM = 64
IN_WIDTH = 4 * ATT_WIDTH + 2 * SSM_WIDTH
D_FF = 4 * D_MODEL
PAGE_SIZE = 128
EPS = 1e-6

LANES = 128
SUBLANES = 8
GROUPS_PER_BLOCK = LANES // SSM_GROUP
N_GBLOCKS = N_GROUPS // GROUPS_PER_BLOCK
STATE_LANES = GROUPS_PER_BLOCK * STATE_DIM
VMEM_LIMIT = 56 * 1024 * 1024


def _cparams(sem):
    return pltpu.CompilerParams(dimension_semantics=sem, vmem_limit_bytes=VMEM_LIMIT)


def _rms_rows(x, w):
    return x * lax.rsqrt(jnp.mean(x * x, axis=-1, keepdims=True) + EPS) * w


def _mm_body(*refs, norm, epilogue, nk):
    it = iter(refs)
    x_ref = next(it)
    nw_ref = next(it) if norm else None
    w_ref = next(it)
    res_ref = next(it) if epilogue == "res" else None
    o_ref = next(it)
    xs_ref = next(it) if norm else None
    acc_ref = next(it) if nk > 1 else None
    j = pl.program_id(1)
    k = pl.program_id(2)

    if norm:
        @pl.when(j == 0)
        def _():
            xs_ref[...] = _rms_rows(x_ref[...], nw_ref[...]).astype(BF16)
        lhs = xs_ref[...]
    else:
        lhs = x_ref[...]
    part = jnp.dot(lhs, w_ref[...], preferred_element_type=F32)

    def finish(acc):
        if epilogue == "relu2":
            r = jnp.maximum(acc, 0.0)
            acc = r * r
        elif epilogue == "res":
            acc = acc + res_ref[...]
        o_ref[...] = acc.astype(o_ref.dtype)

    if nk == 1:
        finish(part)
    else:
        @pl.when(k == 0)
        def _():
            acc_ref[...] = part

        @pl.when(k > 0)
        def _():
            acc_ref[...] += part

        @pl.when(k == nk - 1)
        def _():
            finish(acc_ref[...])


def _matmul(x, w, *, name, bm, bn, bk, norm_w=None, res=None, epilogue="none", out_dtype=F32):
    m, kdim = x.shape
    n = w.shape[1]
    bm = min(bm, m)
    nk = kdim // bk
    norm = norm_w is not None
    assert not (norm and nk != 1)
    in_specs = [pl.BlockSpec((bm, bk), lambda i, j, k: (i, k))]
    args = [x]
    if norm:
        in_specs.append(pl.BlockSpec((1, kdim), lambda i, j, k: (0, 0)))
        args.append(norm_w.reshape(1, kdim))
    in_specs.append(pl.BlockSpec((bk, bn), lambda i, j, k: (k, j)))
    args.append(w)
    if epilogue == "res":
        in_specs.append(pl.BlockSpec((bm, bn), lambda i, j, k: (i, j)))
        args.append(res)
    scratch = []
    if norm:
        scratch.append(pltpu.VMEM((bm, kdim), BF16))
    if nk > 1:
        scratch.append(pltpu.VMEM((bm, bn), F32))
    return pl.pallas_call(
        functools.partial(_mm_body, norm=norm, epilogue=epilogue, nk=nk),
        grid=(m // bm, n // bn, nk),
        in_specs=in_specs,
        out_specs=pl.BlockSpec((bm, bn), lambda i, j, k: (i, j)),
        out_shape=jax.ShapeDtypeStruct((m, n), out_dtype),
        scratch_shapes=scratch,
        compiler_params=_cparams(("parallel", "arbitrary", "arbitrary")),
        name=name,
    )(*args)


def _rmsnorm_body(x_ref, w_ref, o_ref):
    o_ref[...] = _rms_rows(x_ref[...], w_ref[...])


def _rmsnorm(x, w, *, bm=256):
    m, d = x.shape
    bm = min(bm, m)
    return pl.pallas_call(
        _rmsnorm_body,
        grid=(m // bm,),
        in_specs=[pl.BlockSpec((bm, d), lambda i: (i, 0)), pl.BlockSpec((1, d), lambda i: (0, 0))],
        out_specs=pl.BlockSpec((bm, d), lambda i: (i, 0)),
        out_shape=jax.ShapeDtypeStruct((m, d), F32),
        compiler_params=_cparams(("parallel",)),
        name="rmsnorm",
    )(x, w.reshape(1, d))


LOG2E = 1.0 / math.log(2.0)
ATT_SCALE2 = HEAD_DIM ** -0.5 * LOG2E


def _softplus2(z2):
    return jnp.maximum(z2, 0.0) + jnp.log(1.0 + jnp.exp2(-jnp.abs(z2))) * LOG2E


def _attn_prompt_body(bias_ref, q_ref, k_ref, v_ref, g_ref, o_ref, ko_ref, vo_ref, kb_ref, vb_ref, acc_ref, carry_ref,
                      *, tk, nsub, nhead):
    hg = pl.program_id(1)
    qi = pl.program_id(2)

    @pl.when(qi == 0)
    def _():
        k = k_ref[...]
        v = v_ref[...]
        ko_ref[...] = k
        vo_ref[...] = v
        kb_ref[...] = k.astype(BF16)
        vb_ref[...] = v.astype(BF16)

    tq = tk * nsub
    row = lax.broadcasted_iota(jnp.int32, (tk, tk), 0)
    col = lax.broadcasted_iota(jnp.int32, (tk, tk), 1)
    later = (row > col).astype(BF16)
    lanes = lambda h: slice(h * HEAD_DIM, (h + 1) * HEAD_DIM)
    q = [(q_ref[:, lanes(h)] * ATT_SCALE2).astype(BF16) for h in range(nhead)]
    bias = [bias_ref[hg * nhead + h] * LOG2E for h in range(nhead)]

    acc_ref[...] = jnp.zeros_like(acc_ref)
    carry_ref[...] = jnp.zeros_like(carry_ref)

    def block(h, lo, kb, valid):
        start = pl.multiple_of(kb * tk, tk)
        kblk = kb_ref[pl.ds(start, tk), lanes(h)]
        vblk = vb_ref[pl.ds(start, tk), lanes(h)]
        z = lax.dot_general(q[h][lo:], kblk, (((1,), (1,)), ((), ())), preferred_element_type=F32) + bias[h]
        sp = _softplus2(z)
        if valid is not None:
            sp = jnp.where(valid, sp, 0.0)
        inner = jnp.dot(sp.astype(BF16), later, preferred_element_type=F32)
        carry = carry_ref[h, lo:, :]
        w = jnp.exp2(z - sp - inner - carry)
        if valid is not None:
            w = jnp.where(valid, w, 0.0)
        acc_ref[lo:, lanes(h)] += jnp.dot(w.astype(BF16), vblk, preferred_element_type=F32)
        carry_ref[h, lo:, :] = carry + inner[:, :1] + sp[:, :1]

    for j in reversed(range(nsub)):
        m = tq - j * tk
        ri = lax.broadcasted_iota(jnp.int32, (m, tk), 0)
        ci = lax.broadcasted_iota(jnp.int32, (m, tk), 1)
        valid = (ri >= tk) | (ci < ri)
        for h in range(nhead):
            block(h, j * tk, nsub * qi + j, valid)

    @pl.loop(0, nsub * qi)
    def _(it):
        for h in range(nhead):
            block(h, 0, nsub * qi - 1 - it, None)

    o_ref[...] = (acc_ref[...] * jax.nn.sigmoid(g_ref[...])).astype(o_ref.dtype)


def _attn_prompt(proj, sb_bias, *, tk=256, nsub=4, nhead=2):
    b, t, _ = proj.shape
    tq = tk * nsub
    wid = nhead * HEAD_DIM
    ngrp = N_HEADS // nhead
    qspec = lambda off: pl.BlockSpec((None, tq, wid), lambda bi, h, qi: (bi, qi, off + h))
    kvspec = lambda off: pl.BlockSpec((None, t, wid), lambda bi, h, qi: (bi, 0, off + h))
    return pl.pallas_call(
        functools.partial(_attn_prompt_body, tk=tk, nsub=nsub, nhead=nhead),
        grid=(b, ngrp, t // tq),
        in_specs=[pl.BlockSpec(memory_space=pltpu.SMEM),
                  qspec(0), kvspec(ngrp), kvspec(2 * ngrp), qspec(3 * ngrp)],
        out_specs=[qspec(0), kvspec(0), kvspec(0)],
        out_shape=[jax.ShapeDtypeStruct((b, t, ATT_WIDTH), BF16),
                   jax.ShapeDtypeStruct((b, t, ATT_WIDTH), F32),
                   jax.ShapeDtypeStruct((b, t, ATT_WIDTH), F32)],
        scratch_shapes=[pltpu.VMEM((t, wid), BF16), pltpu.VMEM((t, wid), BF16),
                        pltpu.VMEM((tq, wid), F32), pltpu.VMEM((nhead, tq, 1), F32)],
        compiler_params=_cparams(("parallel", "parallel", "arbitrary")),
        name="attn_prompt",
    )(sb_bias, proj, proj, proj, proj)


def _attn_sample_body(pt_ref, q_ref, kn_ref, vn_ref, g_ref, bias_ref, *refs, pages_per_step, n_tok):
    kp_refs = refs[:pages_per_step]
    vp_refs = refs[pages_per_step:2 * pages_per_step]
    o_ref = refs[2 * pages_per_step]
    qbd_ref, acc_ref, carry_ref = refs[2 * pages_per_step + 1:]
    c = pl.program_id(1)
    n_rows = n_tok * N_HEADS
    bias = bias_ref[...] * LOG2E
    row = lax.broadcasted_iota(jnp.int32, (PAGE_SIZE, PAGE_SIZE), 0)
    col = lax.broadcasted_iota(jnp.int32, (PAGE_SIZE, PAGE_SIZE), 1)
    later = (row > col).astype(BF16)

    def page(kp, vp, valid):
        z = lax.dot_general(qbd_ref[...], kp, (((1,), (1,)), ((), ())), preferred_element_type=F32) + bias
        sp = _softplus2(z)
        if valid is not None:
            sp = jnp.where(valid, sp, 0.0)
        inner = jnp.dot(sp.astype(BF16), later, preferred_element_type=F32)
        carry = carry_ref[...]
        w = jnp.exp2(z - sp - inner - carry)
        if valid is not None:
            w = jnp.where(valid, w, 0.0)
        acc_ref[...] += jnp.dot(w.astype(BF16), vp, preferred_element_type=F32)
        carry_ref[...] = carry + inner[:, :1] + sp[:, :1]

    @pl.when(c == 0)
    def _():
        q = q_ref[...] * ATT_SCALE2
        qt = jnp.concatenate([q] * N_HEADS, axis=1)
        rh = lax.broadcasted_iota(jnp.int32, (n_rows, ATT_WIDTH), 0) % N_HEADS
        lh = lax.broadcasted_iota(jnp.int32, (n_rows, ATT_WIDTH), 1) // HEAD_DIM
        qbd_ref[...] = jnp.where(rh == lh, qt, 0.0).astype(BF16)
        acc_ref[...] = jnp.zeros_like(acc_ref)
        carry_ref[...] = jnp.zeros_like(carry_ref)
        pad = jnp.zeros((PAGE_SIZE - kn_ref.shape[0], ATT_WIDTH), BF16)
        kn = jnp.concatenate([kn_ref[...].astype(BF16), pad], axis=0)
        vn = jnp.concatenate([vn_ref[...].astype(BF16), pad], axis=0)
        qtok = lax.broadcasted_iota(jnp.int32, (n_rows, PAGE_SIZE), 0) // N_HEADS
        key = lax.broadcasted_iota(jnp.int32, (n_rows, PAGE_SIZE), 1)
        page(kn, vn, key < qtok)

    for i in range(pages_per_step):
        page(kp_refs[i][...].astype(BF16), vp_refs[i][...].astype(BF16), None)

    @pl.when(c == pl.num_programs(1) - 1)
    def _():
        rh = lax.broadcasted_iota(jnp.int32, (n_rows, HEAD_DIM), 0) % N_HEADS
        out = jnp.zeros((n_rows, HEAD_DIM), F32)
        for h in range(N_HEADS):
            out = out + jnp.where(rh == h, acc_ref[:, h * HEAD_DIM:(h + 1) * HEAD_DIM], 0.0)
        o_ref[...] = out * jax.nn.sigmoid(g_ref[...])


def _attn_sample(proj_s, cache_k, cache_v, page_table, sb_bias, *, pages_per_step=8):
    b, n_tok, _ = proj_s.shape
    n_pages = page_table.shape[1]
    n_rows = n_tok * N_HEADS
    steps = n_pages // pages_per_step
    q_rows = proj_s[:, :, :ATT_WIDTH].reshape(b, n_rows, HEAD_DIM)
    g_rows = proj_s[:, :, 3 * ATT_WIDTH:4 * ATT_WIDTH].reshape(b, n_rows, HEAD_DIM)
    bias_col = jnp.tile(sb_bias, n_tok).reshape(n_rows, 1)
    n_new = 2 * SUBLANES
    pad_new = lambda a: jnp.pad(a, ((0, 0), (0, n_new - n_tok), (0, 0)))
    k_new = pad_new(proj_s[:, :, ATT_WIDTH:2 * ATT_WIDTH])
    v_new = pad_new(proj_s[:, :, 2 * ATT_WIDTH:3 * ATT_WIDTH])

    def page_spec(i):
        def imap(bi, c, pt):
            return (pt[bi, n_pages - 1 - (c * pages_per_step + i)], 0, 0)
        return pl.BlockSpec((None, PAGE_SIZE, ATT_WIDTH), imap)

    rows_spec = pl.BlockSpec((None, n_rows, HEAD_DIM), lambda bi, c, pt: (bi, 0, 0))
    new_spec = pl.BlockSpec((None, n_new, ATT_WIDTH), lambda bi, c, pt: (bi, 0, 0))
    grid_spec = pltpu.PrefetchScalarGridSpec(
        num_scalar_prefetch=1,
        grid=(b, steps),
        in_specs=[rows_spec, new_spec, new_spec, rows_spec,
                  pl.BlockSpec((n_rows, 1), lambda bi, c, pt: (0, 0))]
                 + [page_spec(i) for i in range(pages_per_step)] * 2,
        out_specs=rows_spec,
        scratch_shapes=[pltpu.VMEM((n_rows, ATT_WIDTH), BF16),
                        pltpu.VMEM((n_rows, ATT_WIDTH), F32),
                        pltpu.VMEM((n_rows, 1), F32)],
    )
    out = pl.pallas_call(
        functools.partial(_attn_sample_body, pages_per_step=pages_per_step, n_tok=n_tok),
        grid_spec=grid_spec,
        out_shape=jax.ShapeDtypeStruct((b, n_rows, HEAD_DIM), F32),
        compiler_params=_cparams(("parallel", "arbitrary")),
        name="attn_sample",
    )(page_table, q_rows, k_new, v_new, g_rows, bias_col,
      *([cache_k] * pages_per_step), *([cache_v] * pages_per_step))
    return out.reshape(b, n_tok, ATT_WIDTH)


def _ssm_params_body(are_ref, aim_ref, ldt_ref, bre_ref, bim_ref, abr_ref, abi_ref, bbr_ref, bbi_ref):
    a_re = are_ref[...]
    a_im = aim_ref[...]
    dt = jnp.exp(ldt_ref[...])
    mag = jnp.exp(dt * a_re)
    ab_re = mag * jnp.cos(dt * a_im)
    ab_im = mag * jnp.sin(dt * a_im)
    abr_ref[...] = ab_re
    abi_ref[...] = ab_im
    den = a_re * a_re + a_im * a_im
    n_re = ab_re - 1.0
    co_re = (n_re * a_re + ab_im * a_im) / den
    co_im = (ab_im * a_re - n_re * a_im) / den
    b_re = bre_ref[...]
    b_im = bim_ref[...]
    bbr_ref[...] = co_re[:, None, :] * b_re - co_im[:, None, :] * b_im
    bbi_ref[...] = co_re[:, None, :] * b_im + co_im[:, None, :] * b_re


def _ssm_params(a_re, a_im, log_dt, b_re, b_im):
    g, p = a_re.shape
    c = b_re.shape[1]
    return pl.pallas_call(
        _ssm_params_body,
        name="ssm_params",
        out_shape=[jax.ShapeDtypeStruct((g, p), F32), jax.ShapeDtypeStruct((g, p), F32),
                   jax.ShapeDtypeStruct((g, c, p), F32), jax.ShapeDtypeStruct((g, c, p), F32)],
    )(a_re, a_im, log_dt.reshape(g, 1), b_re, b_im)


def _gelu_tanh(y):
    return 0.5 * y * (1.0 + jnp.tanh(math.sqrt(2.0 / math.pi) * (y + 0.044715 * (y * y * y))))


def _ssm_prompt_body(u_ref, g_ref, b_ref, c_ref, gw_ref, ab_ref, dg_ref, o_ref, hT_ref, bu_ref, hh_ref, st_ref,
                     *, lc):
    ci = pl.program_id(1)
    rows = lc * SUBLANES
    n_slab = STATE_LANES // LANES

    @pl.when(ci == 0)
    def _():
        st_ref[...] = jnp.zeros_like(st_ref)

    u = u_ref[...].reshape(rows, LANES)
    first = (lax.broadcasted_iota(jnp.int32, (rows, LANES), 0) % SUBLANES) < (SUBLANES // 2)
    lhs = jnp.concatenate([jnp.where(first, u, 0.0), jnp.where(first, 0.0, u)], axis=1).astype(BF16)
    bu_ref[...] = jnp.dot(lhs, b_ref[...], preferred_element_type=F32)

    ar = [ab_ref[j] for j in range(n_slab)]
    ai = [ab_ref[n_slab + j] for j in range(n_slab)]

    def step(t, h):
        r0 = pl.multiple_of(t * SUBLANES, SUBLANES)
        new = []
        for j in range(n_slab):
            hr, hi = h[j], h[n_slab + j]
            bre = bu_ref[pl.ds(r0, SUBLANES), j * LANES:(j + 1) * LANES]
            bim = bu_ref[pl.ds(r0, SUBLANES), STATE_LANES + j * LANES:STATE_LANES + (j + 1) * LANES]
            nr = ar[j] * hr - ai[j] * hi + bre
            ni = ar[j] * hi + ai[j] * hr + bim
            hh_ref[pl.ds(r0, SUBLANES), j * LANES:(j + 1) * LANES] = nr
            hh_ref[pl.ds(r0, SUBLANES), STATE_LANES + j * LANES:STATE_LANES + (j + 1) * LANES] = ni
            new.append((nr, ni))
        return tuple(x[0] for x in new) + tuple(x[1] for x in new)

    h0 = tuple(st_ref[j] for j in range(2 * n_slab))
    hN = lax.fori_loop(0, lc, step, h0, unroll=8)
    for j in range(2 * n_slab):
        st_ref[j] = hN[j]
        hT_ref[j] = hN[j]

    y2 = jnp.dot(hh_ref[...].astype(BF16), c_ref[...], preferred_element_type=F32)
    d = jnp.broadcast_to(dg_ref[0][None], (lc, SUBLANES, LANES)).reshape(rows, LANES)
    gb = jnp.broadcast_to(dg_ref[1][None], (lc, SUBLANES, LANES)).reshape(rows, LANES)
    y = jnp.where(first, y2[:, :LANES], y2[:, LANES:]) + d * u
    z = _gelu_tanh(y)
    g2 = jnp.dot(z.astype(BF16), gw_ref[...], preferred_element_type=F32)
    gate = jax.nn.sigmoid(jnp.where(first, g2[:, :LANES], g2[:, LANES:]) + gb)
    out = z * gate * jax.nn.sigmoid(g_ref[...].reshape(rows, LANES))
    o_ref[...] = out.reshape(lc, SUBLANES, LANES)


def _ssm_prompt(u_tm, g_tm, b_st, c_st, gw_st, ab_tm, dg_tm, *, lc=256):
    npair, t = u_tm.shape[:2]
    n_slab2 = 2 * STATE_LANES // LANES
    seq_spec = pl.BlockSpec((None, lc, SUBLANES, LANES), lambda p, c: (p, c, 0, 0))
    par_spec = lambda shp: pl.BlockSpec((None,) + shp, lambda p, c: (p,) + (0,) * len(shp))
    return pl.pallas_call(
        functools.partial(_ssm_prompt_body, lc=lc),
        grid=(npair, t // lc),
        in_specs=[seq_spec, seq_spec, par_spec((2 * LANES, 2 * STATE_LANES)), par_spec((2 * STATE_LANES, 2 * LANES)),
                  par_spec((LANES, 2 * LANES)), par_spec((n_slab2, SUBLANES, LANES)),
                  par_spec((2, SUBLANES, LANES))],
        out_specs=[seq_spec, par_spec((n_slab2, SUBLANES, LANES))],
        out_shape=[jax.ShapeDtypeStruct(u_tm.shape, F32),
                   jax.ShapeDtypeStruct((npair, n_slab2, SUBLANES, LANES), F32)],
        scratch_shapes=[pltpu.VMEM((lc * SUBLANES, 2 * STATE_LANES), F32),
                        pltpu.VMEM((lc * SUBLANES, 2 * STATE_LANES), F32),
                        pltpu.VMEM((n_slab2, SUBLANES, LANES), F32)],
        compiler_params=_cparams(("parallel", "arbitrary")),
        name="ssm_prompt",
    )(u_tm, g_tm, b_st, c_st, gw_st, ab_tm, dg_tm)


def _ssm_sample_body(u_ref, g_ref, hre_ref, him_ref, b_ref, c_ref, gw_ref, ab_ref, dg_ref,
                     o_ref, ore_ref, oim_ref, *, n_tok, nb):
    hp = lax.Precision.HIGHEST
    u = u_ref[...].reshape(n_tok * nb, LANES)
    bu = jnp.dot(u, b_ref[...], preferred_element_type=F32, precision=hp)
    ar = ab_ref[0:1, :]
    ai = ab_ref[1:2, :]
    hr = hre_ref[...]
    hi = him_ref[...]
    hs = []
    for t in range(n_tok):
        bre = bu[t * nb:(t + 1) * nb, :STATE_LANES]
        bim = bu[t * nb:(t + 1) * nb, STATE_LANES:]
        hr, hi = ar * hr - ai * hi + bre, ar * hi + ai * hr + bim
        hs.append(jnp.concatenate([hr, hi], axis=1))
    ore_ref[...] = hr
    oim_ref[...] = hi
    hh = jnp.concatenate(hs, axis=0)
    y = jnp.dot(hh, c_ref[...], preferred_element_type=F32, precision=hp) + dg_ref[0:1, :] * u
    z = _gelu_tanh(y)
    gate = jax.nn.sigmoid(jnp.dot(z, gw_ref[...], preferred_element_type=F32, precision=hp) + dg_ref[1:2, :])
    out = z * gate * jax.nn.sigmoid(g_ref[...].reshape(n_tok * nb, LANES))
    o_ref[...] = out.reshape(n_tok, nb, LANES)


def _ssm_sample(u_t, g_t, h_re, h_im, b_blk, c_blk, gw_blk, ab_row, dg_row):
    n_tok, nb, _ = u_t.shape
    seq_spec = pl.BlockSpec((n_tok, nb, LANES), lambda gb: (0, 0, gb))
    st_spec = pl.BlockSpec((nb, STATE_LANES), lambda gb: (0, gb))
    par_spec = lambda shp: pl.BlockSpec((None,) + shp, lambda gb: (gb,) + (0,) * len(shp))
    return pl.pallas_call(
        functools.partial(_ssm_sample_body, n_tok=n_tok, nb=nb),
        grid=(N_GBLOCKS,),
        in_specs=[seq_spec, seq_spec, st_spec, st_spec,
                  par_spec((LANES, 2 * STATE_LANES)), par_spec((2 * STATE_LANES, LANES)),
                  par_spec((LANES, LANES)), par_spec((2, STATE_LANES)), par_spec((2, LANES))],
        out_specs=[seq_spec, st_spec, st_spec],
        out_shape=[jax.ShapeDtypeStruct(u_t.shape, F32),
                   jax.ShapeDtypeStruct(h_re.shape, F32), jax.ShapeDtypeStruct(h_im.shape, F32)],
        compiler_params=_cparams(("parallel",)),
        name="ssm_sample",
    )(u_t, g_t, h_re, h_im, b_blk, c_blk, gw_blk, ab_row, dg_row)


def _ssm_block_weights(ab_re, ab_im, bb_re, bb_im, c_re, c_im, d, glu_w, glu_b):
    nb, gpb = N_GBLOCKS, GROUPS_PER_BLOCK
    eye = jnp.eye(gpb, dtype=F32)
    bb = jnp.stack([bb_re, bb_im], axis=2).reshape(nb, gpb, SSM_GROUP, 2, STATE_DIM)
    b_blk = jnp.einsum('bgcrp,gh->bgcrhp', bb, eye).reshape(nb, LANES, 2 * STATE_LANES)
    cc = jnp.stack([c_re, -c_im], axis=2).reshape(nb, gpb, SSM_GROUP, 2, STATE_DIM)
    c_blk = jnp.einsum('bgcrp,gh->brgphc', cc, eye).reshape(nb, 2 * STATE_LANES, LANES)
    gw = glu_w.reshape(nb, gpb, SSM_GROUP, SSM_GROUP)
    gw_blk = jnp.einsum('bgce,gh->bgche', gw, eye).reshape(nb, LANES, LANES)
    ab_row = jnp.stack([ab_re.reshape(nb, STATE_LANES), ab_im.reshape(nb, STATE_LANES)], axis=1)
    dg_row = jnp.stack([d.reshape(nb, LANES), glu_b.reshape(nb, LANES)], axis=1)
    return b_blk, c_blk, gw_blk, ab_row, dg_row


def _pair_weights(b_blk, c_blk, gw_blk, ab_row, dg_row, nbatch):
    npair = N_GBLOCKS // 2
    n_slab = STATE_LANES // LANES
    b_st = b_blk.reshape(npair, 2 * LANES, 2 * STATE_LANES).astype(BF16)
    c_st = c_blk.reshape(npair, 2, 2 * STATE_LANES, LANES).transpose(0, 2, 1, 3).reshape(
        npair, 2 * STATE_LANES, 2 * LANES).astype(BF16)
    gw_st = gw_blk.reshape(npair, 2, LANES, LANES).transpose(0, 2, 1, 3).reshape(
        npair, LANES, 2 * LANES).astype(BF16)
    ab = ab_row.reshape(npair, 2, 2, n_slab, LANES).transpose(0, 2, 3, 1, 4)
    ab_tm = jnp.repeat(ab, nbatch, axis=3).reshape(npair, 2 * n_slab, 2 * nbatch, LANES)
    dg = dg_row.reshape(npair, 2, 2, LANES).transpose(0, 2, 1, 3)
    dg_tm = jnp.repeat(dg, nbatch, axis=2)
    return b_st, c_st, gw_st, ab_tm, dg_tm


def _dense_tail(x2d, mixed, w_out, norm_mlp_w, w_up, w_down, norm_final_w):
    x1 = _matmul(mixed, w_out, name="out_proj", bm=1024, bn=1024, bk=D_MODEL, res=x2d, epilogue="res")
    hid = _matmul(x1, w_up, name="mlp_up", bm=512, bn=1024, bk=D_MODEL, norm_w=norm_mlp_w, epilogue="relu2",
                  out_dtype=BF16)
    x2 = _matmul(hid, w_down, name="mlp_down", bm=1024, bn=1024, bk=2048, res=x1, epilogue="res")
    return _rmsnorm(x2, norm_final_w)


def kernel(x_prompt, x_sample, cache_k, cache_v, state_ssm_re, state_ssm_im, page_table, norm_mix_w, w_in, sb_bias,
           ssm_a_re, ssm_a_im, ssm_b_re, ssm_b_im, ssm_c_re, ssm_c_im, ssm_d, ssm_log_dt, glu_w, glu_b, w_out,
           norm_mlp_w, w_up, w_down, norm_final_w):
    depth = w_in.shape[0]
    assert depth == 1
    nb_p, t_p, _ = x_prompt.shape
    nb_s, t_s, _ = x_sample.shape
    assert nb_p * 2 == SUBLANES
    l = 0
    w_in_b = w_in[l].astype(BF16)
    w_out_b = w_out[l].astype(BF16)
    w_up_b = w_up[l].astype(BF16)
    w_down_b = w_down[l].astype(BF16)

    ab_re, ab_im, bb_re, bb_im = _ssm_params(ssm_a_re[l], ssm_a_im[l], ssm_log_dt[l],
                                             ssm_b_re[l].transpose(0, 2, 1), ssm_b_im[l].transpose(0, 2, 1))
    blk = _ssm_block_weights(ab_re, ab_im, bb_re, bb_im, ssm_c_re[l], ssm_c_im[l], ssm_d[l], glu_w[l], glu_b[l])
    pair = _pair_weights(*blk, nbatch=nb_p)

    xp = x_prompt.reshape(nb_p * t_p, D_MODEL)
    proj_p = _matmul(xp, w_in_b, name="in_proj", bm=512, bn=1024, bk=D_MODEL, norm_w=norm_mix_w[l])
    proj_p3 = proj_p.reshape(nb_p, t_p, IN_WIDTH)
    att_p, k_p, v_p = _attn_prompt(proj_p3, sb_bias[l])

    npair = N_GBLOCKS // 2

    def to_tm(cols):
        return cols.reshape(nb_p, t_p, npair, 2, LANES).transpose(2, 1, 3, 0, 4).reshape(npair, t_p, SUBLANES, LANES)

    u_tm = to_tm(proj_p3[:, :, 4 * ATT_WIDTH:4 * ATT_WIDTH + SSM_WIDTH])
    g_tm = to_tm(proj_p3[:, :, 4 * ATT_WIDTH + SSM_WIDTH:])
    ssm_tm, hT = _ssm_prompt(u_tm, g_tm, *pair)
    ssm_p = ssm_tm.reshape(npair, t_p, 2, nb_p, LANES).transpose(3, 1, 0, 2, 4).reshape(nb_p * t_p, SSM_WIDTH)
    n_slab = STATE_LANES // LANES
    hT = hT.reshape(npair, 2, n_slab, 2, nb_p, 2, STATE_DIM).transpose(1, 4, 0, 3, 2, 5, 6).reshape(
        2, nb_p, N_GROUPS, STATE_DIM)
    mixed_p = jnp.concatenate([ssm_p.astype(BF16), att_p.reshape(nb_p * t_p, ATT_WIDTH)], axis=1)
    y_p = _dense_tail(xp, mixed_p, w_out_b, norm_mlp_w[l], w_up_b, w_down_b, norm_final_w)

    xs = x_sample.reshape(nb_s * t_s, D_MODEL)
    proj_s = _matmul(xs, w_in_b, name="in_proj", bm=128, bn=1024, bk=D_MODEL, norm_w=norm_mix_w[l])
    proj_s3 = proj_s.reshape(nb_s, t_s, IN_WIDTH)
    n_phys = cache_k.shape[1]
    att_s = _attn_sample(proj_s3, cache_k.reshape(depth * n_phys, PAGE_SIZE, ATT_WIDTH),
                         cache_v.reshape(depth * n_phys, PAGE_SIZE, ATT_WIDTH), page_table + l * n_phys,
                         sb_bias[l])
    u_t = proj_s3[:, :, 4 * ATT_WIDTH:4 * ATT_WIDTH + SSM_WIDTH].transpose(1, 0, 2)
    g_t = proj_s3[:, :, 4 * ATT_WIDTH + SSM_WIDTH:].transpose(1, 0, 2)
    ssm_t, hs_re, hs_im = _ssm_sample(u_t, g_t, state_ssm_re[l].reshape(nb_s, N_GROUPS * STATE_DIM),
                                      state_ssm_im[l].reshape(nb_s, N_GROUPS * STATE_DIM), *blk)
    ssm_s = ssm_t.transpose(1, 0, 2).reshape(nb_s * t_s, SSM_WIDTH)
    mixed_s = jnp.concatenate([ssm_s, att_s.reshape(nb_s * t_s, ATT_WIDTH)], axis=1).astype(BF16)
    y_s = _dense_tail(xs, mixed_s, w_out_b, norm_mlp_w[l], w_up_b, w_down_b, norm_final_w)

    kv_shape_p = (1, nb_p, t_p, N_HEADS, HEAD_DIM)
    kv_shape_s = (1, nb_s, t_s, N_HEADS, HEAD_DIM)
    return (y_p.reshape(nb_p, t_p, D_MODEL), y_s.reshape(nb_s, t_s, D_MODEL),
            k_p.reshape(kv_shape_p), v_p.reshape(kv_shape_p),
            hT[0][None], hT[1][None],
            proj_s3[:, :, ATT_WIDTH:2 * ATT_WIDTH].reshape(kv_shape_s),
            proj_s3[:, :, 2 * ATT_WIDTH:3 * ATT_WIDTH].reshape(kv_shape_s),
            hs_re.reshape(1, nb_s, N_GROUPS, STATE_DIM), hs_im.reshape(1, nb_s, N_GROUPS, STATE_DIM))
```

```python
import functools
import math

import jax
import jax.numpy as jnp
from jax import lax
from jax.experimental import pallas as pl
from jax.experimental.pallas import tpu as pltpu

F32 = jnp.float32
BF16 = jnp.bfloat16

D_MODEL = 4096
HEAD_DIM = 128
N_HEADS = 16
ATT_WIDTH = N_HEADS * HEAD_DIM
SSM_WIDTH = 2048
SSM_GROUP = 16
N_GROUPS = 128
STATE_DIM = 64
IN_WIDTH = 4 * ATT_WIDTH + 2 * SSM_WIDTH
D_FF = 4 * D_MODEL
PAGE_SIZE = 128
EPS = 1e-6

LANES = 128
SUBLANES = 8
HEAD_HALVES = N_HEADS // SUBLANES
GROUPS_PER_BLOCK = LANES // SSM_GROUP
N_GBLOCKS = N_GROUPS // GROUPS_PER_BLOCK
STATE_LANES = GROUPS_PER_BLOCK * STATE_DIM
VMEM_LIMIT = 56 * 1024 * 1024


def _cparams(sem):
    return pltpu.CompilerParams(dimension_semantics=sem, vmem_limit_bytes=VMEM_LIMIT)


def _rms_rows(x, w):
    return x * lax.rsqrt(jnp.mean(x * x, axis=-1, keepdims=True) + EPS) * w


def _mm_body(*refs, norm, epilogue, nk):
    it = iter(refs)
    x_ref = next(it)
    nw_ref = next(it) if norm else None
    w_ref = next(it)
    res_ref = next(it) if epilogue == "res" else None
    o_ref = next(it)
    xs_ref = next(it) if norm else None
    acc_ref = next(it) if nk > 1 else None
    j = pl.program_id(1)
    k = pl.program_id(2)

    if norm:
        @pl.when(j == 0)
        def _():
            xs_ref[...] = _rms_rows(x_ref[...], nw_ref[...]).astype(BF16)
        lhs = xs_ref[...]
    else:
        lhs = x_ref[...]
    part = jnp.dot(lhs, w_ref[...], preferred_element_type=F32)

    def finish(acc):
        if epilogue == "relu2":
            r = jnp.maximum(acc, 0.0)
            acc = r * r
        elif epilogue == "res":
            acc = acc + res_ref[...]
        o_ref[...] = acc.astype(o_ref.dtype)

    if nk == 1:
        finish(part)
    else:
        @pl.when(k == 0)
        def _():
            acc_ref[...] = part

        @pl.when(k > 0)
        def _():
            acc_ref[...] += part

        @pl.when(k == nk - 1)
        def _():
            finish(acc_ref[...])


def _matmul(x, w, *, name, bm, bn, bk, norm_w=None, res=None, epilogue="none", out_dtype=F32):
    m, kdim = x.shape
    n = w.shape[1]
    bm = min(bm, m)
    nk = kdim // bk
    norm = norm_w is not None
    assert not (norm and nk != 1)
    in_specs = [pl.BlockSpec((bm, bk), lambda i, j, k: (i, k))]
    args = [x]
    if norm:
        in_specs.append(pl.BlockSpec((1, kdim), lambda i, j, k: (0, 0)))
        args.append(norm_w.reshape(1, kdim))
    in_specs.append(pl.BlockSpec((bk, bn), lambda i, j, k: (k, j)))
    args.append(w)
    if epilogue == "res":
        in_specs.append(pl.BlockSpec((bm, bn), lambda i, j, k: (i, j)))
        args.append(res)
    scratch = []
    if norm:
        scratch.append(pltpu.VMEM((bm, kdim), BF16))
    if nk > 1:
        scratch.append(pltpu.VMEM((bm, bn), F32))
    return pl.pallas_call(
        functools.partial(_mm_body, norm=norm, epilogue=epilogue, nk=nk),
        grid=(m // bm, n // bn, nk),
        in_specs=in_specs,
        out_specs=pl.BlockSpec((bm, bn), lambda i, j, k: (i, j)),
        out_shape=jax.ShapeDtypeStruct((m, n), out_dtype),
        scratch_shapes=scratch,
        compiler_params=_cparams(("parallel", "arbitrary", "arbitrary")),
        name=name,
    )(*args)


def _out_proj_body(a_ref, b_ref, wa_ref, wb_ref, res_ref, o_ref):
    acc = jnp.dot(a_ref[...], wa_ref[...], preferred_element_type=F32)
    acc = acc + jnp.dot(b_ref[...], wb_ref[...], preferred_element_type=F32)
    o_ref[...] = acc + res_ref[...]


def _out_proj(a, b, w, res, *, bm=1024, bn=1024):
    m, ka = a.shape
    kb = b.shape[1]
    n = w.shape[1]
    bm = min(bm, m)
    assert ka == kb
    row_spec = lambda wid: pl.BlockSpec((bm, wid), lambda i, j: (i, 0))
    return pl.pallas_call(
        _out_proj_body,
        grid=(m // bm, n // bn),
        in_specs=[row_spec(ka), row_spec(kb),
                  pl.BlockSpec((ka, bn), lambda i, j: (0, j)), pl.BlockSpec((kb, bn), lambda i, j: (1, j)),
                  pl.BlockSpec((bm, bn), lambda i, j: (i, j))],
        out_specs=pl.BlockSpec((bm, bn), lambda i, j: (i, j)),
        out_shape=jax.ShapeDtypeStruct((m, n), F32),
        compiler_params=_cparams(("parallel", "arbitrary")),
        name="out_proj",
    )(a, b, w, w, res)


def _rmsnorm_body(x_ref, w_ref, o_ref):
    o_ref[...] = _rms_rows(x_ref[...], w_ref[...])


def _rmsnorm(x, w, *, bm=256):
    m, d = x.shape
    bm = min(bm, m)
    return pl.pallas_call(
        _rmsnorm_body,
        grid=(m // bm,),
        in_specs=[pl.BlockSpec((bm, d), lambda i: (i, 0)), pl.BlockSpec((1, d), lambda i: (0, 0))],
        out_specs=pl.BlockSpec((bm, d), lambda i: (i, 0)),
        out_shape=jax.ShapeDtypeStruct((m, d), F32),
        compiler_params=_cparams(("parallel",)),
        name="rmsnorm",
    )(x, w.reshape(1, d))


LOG2E = 1.0 / math.log(2.0)
ATT_SCALE2 = HEAD_DIM ** -0.5 * LOG2E


def _softplus2(z2):
    return jnp.maximum(z2, 0.0) + jnp.log(1.0 + jnp.exp2(-jnp.abs(z2))) * LOG2E


def _attn_prompt_body(bias_ref, q_ref, k_ref, v_ref, g_ref, o_ref, ko_ref, vo_ref, kb_ref, vb_ref, acc_ref, carry_ref,
                      *, tk, nsub, nhead):
    hg = pl.program_id(1)
    qi = pl.program_id(2)

    @pl.when(qi == 0)
    def _():
        k = k_ref[...]
        v = v_ref[...]
        ko_ref[...] = k
        vo_ref[...] = v
        kb_ref[...] = k.astype(BF16)
        vb_ref[...] = v.astype(BF16)

    tq = tk * nsub
    row = lax.broadcasted_iota(jnp.int32, (tk, tk), 0)
    col = lax.broadcasted_iota(jnp.int32, (tk, tk), 1)
    later = (row > col).astype(BF16)
    lanes = lambda h: slice(h * HEAD_DIM, (h + 1) * HEAD_DIM)
    q = [(q_ref[:, lanes(h)] * ATT_SCALE2).astype(BF16) for h in range(nhead)]
    bias = [bias_ref[hg * nhead + h] * LOG2E for h in range(nhead)]

    acc_ref[...] = jnp.zeros_like(acc_ref)
    carry_ref[...] = jnp.zeros_like(carry_ref)

    def block(h, lo, kb, valid):
        start = pl.multiple_of(kb * tk, tk)
        kblk = kb_ref[pl.ds(start, tk), lanes(h)]
        vblk = vb_ref[pl.ds(start, tk), lanes(h)]
        z = lax.dot_general(q[h][lo:], kblk, (((1,), (1,)), ((), ())), preferred_element_type=F32) + bias[h]
        sp = _softplus2(z)
        if valid is not None:
            sp = jnp.where(valid, sp, 0.0)
        inner = jnp.dot(sp.astype(BF16), later, preferred_element_type=F32)
        carry = carry_ref[h, lo:, :]
        w = jnp.exp2(z - sp - inner - carry)
        if valid is not None:
            w = jnp.where(valid, w, 0.0)
        acc_ref[lo:, lanes(h)] += jnp.dot(w.astype(BF16), vblk, preferred_element_type=F32)
        carry_ref[h, lo:, :] = carry + inner[:, :1] + sp[:, :1]

    for j in reversed(range(nsub)):
        m = tq - j * tk
        ri = lax.broadcasted_iota(jnp.int32, (m, tk), 0)
        ci = lax.broadcasted_iota(jnp.int32, (m, tk), 1)
        valid = (ri >= tk) | (ci < ri)
        for h in range(nhead):
            block(h, j * tk, nsub * qi + j, valid)

    @pl.loop(0, nsub * qi)
    def _(it):
        for h in range(nhead):
            block(h, 0, nsub * qi - 1 - it, None)

    o_ref[...] = (acc_ref[...] * jax.nn.sigmoid(g_ref[...])).astype(o_ref.dtype)


def _attn_prompt(proj, sb_bias, *, tk=256, nsub=4, nhead=2):
    b, t, _ = proj.shape
    tq = tk * nsub
    wid = nhead * HEAD_DIM
    ngrp = N_HEADS // nhead
    qspec = lambda off: pl.BlockSpec((None, tq, wid), lambda bi, h, qi: (bi, qi, off + h))
    kvspec = lambda off: pl.BlockSpec((None, t, wid), lambda bi, h, qi: (bi, 0, off + h))
    return pl.pallas_call(
        functools.partial(_attn_prompt_body, tk=tk, nsub=nsub, nhead=nhead),
        grid=(b, ngrp, t // tq),
        in_specs=[pl.BlockSpec(memory_space=pltpu.SMEM),
                  qspec(0), kvspec(ngrp), kvspec(2 * ngrp), qspec(3 * ngrp)],
        out_specs=[qspec(0), kvspec(0), kvspec(0)],
        out_shape=[jax.ShapeDtypeStruct((b, t, ATT_WIDTH), BF16),
                   jax.ShapeDtypeStruct((b, t, ATT_WIDTH), F32),
                   jax.ShapeDtypeStruct((b, t, ATT_WIDTH), F32)],
        scratch_shapes=[pltpu.VMEM((t, wid), BF16), pltpu.VMEM((t, wid), BF16),
                        pltpu.VMEM((tq, wid), F32), pltpu.VMEM((nhead, tq, 1), F32)],
        compiler_params=_cparams(("parallel", "parallel", "arbitrary")),
        name="attn_prompt",
    )(sb_bias, proj, proj, proj, proj)


SLOTS = PAGE_SIZE * HEAD_HALVES
HALF_WIDTH = SUBLANES * HEAD_DIM


def _attn_sample_body(pt_ref, q_ref, kn_ref, vn_ref, g_ref, bias_ref, *refs, pages_per_step, n_tok):
    kp_refs = refs[:pages_per_step]
    vp_refs = refs[pages_per_step:2 * pages_per_step]
    o_ref = refs[2 * pages_per_step]
    qbd_ref, kcat_ref, vcat_ref, acc_ref, carry_ref = refs[2 * pages_per_step + 1:]
    c = pl.program_id(1)
    n_rows = n_tok * N_HEADS
    bias = bias_ref[...] * LOG2E
    row = lax.broadcasted_iota(jnp.int32, (SLOTS, SLOTS), 0)
    col = lax.broadcasted_iota(jnp.int32, (SLOTS, SLOTS), 1)
    later = ((row // HEAD_HALVES) > (col // HEAD_HALVES)).astype(BF16)

    def attend(npages, causal):
        n = npages * SLOTS
        z = lax.dot_general(qbd_ref[...], kcat_ref[:n, :], (((1,), (1,)), ((), ())),
                            preferred_element_type=F32) + bias
        rhalf = (lax.broadcasted_iota(jnp.int32, (n_rows, n), 0) % N_HEADS) // SUBLANES
        slot = lax.broadcasted_iota(jnp.int32, (n_rows, n), 1)
        valid = rhalf == slot % HEAD_HALVES
        if causal:
            valid = valid & (slot // HEAD_HALVES < lax.broadcasted_iota(jnp.int32, (n_rows, n), 0) // N_HEADS)
        sp = jnp.where(valid, _softplus2(z), 0.0)
        pages = lambda a: [a[:, i * SLOTS:(i + 1) * SLOTS] for i in range(npages)]
        stacked = jnp.concatenate(pages(sp), axis=0).astype(BF16)
        inner = jnp.dot(stacked, later, preferred_element_type=F32)
        inner = jnp.concatenate([inner[i * n_rows:(i + 1) * n_rows] for i in range(npages)], axis=1)
        carry = carry_ref[...]
        carries = []
        for i in range(npages):
            carries.append(jnp.broadcast_to(carry, (n_rows, SLOTS)))
            f = i * SLOTS
            carry = carry + inner[:, f:f + 1] + sp[:, f:f + 1] + sp[:, f + 1:f + 2]
        carry_ref[...] = carry
        w = jnp.where(valid, jnp.exp2(z - sp - inner - jnp.concatenate(carries, axis=1)), 0.0)
        acc_ref[...] += jnp.dot(w.astype(BF16), vcat_ref[:n, :], preferred_element_type=F32)

    @pl.when(c == 0)
    def _():
        q = q_ref[...] * ATT_SCALE2
        qt = jnp.concatenate([q] * SUBLANES, axis=1)
        rh = lax.broadcasted_iota(jnp.int32, (n_rows, HALF_WIDTH), 0) % SUBLANES
        lh = lax.broadcasted_iota(jnp.int32, (n_rows, HALF_WIDTH), 1) // HEAD_DIM
        qbd_ref[...] = jnp.where(rh == lh, qt, 0.0).astype(BF16)
        acc_ref[...] = jnp.zeros_like(acc_ref)
        carry_ref[...] = jnp.zeros_like(carry_ref)
        n_new = kn_ref.shape[0]
        kcat_ref[:n_new, :] = kn_ref[...].astype(BF16)
        vcat_ref[:n_new, :] = vn_ref[...].astype(BF16)
        kcat_ref[n_new:SLOTS, :] = jnp.zeros((SLOTS - n_new, HALF_WIDTH), BF16)
        vcat_ref[n_new:SLOTS, :] = jnp.zeros((SLOTS - n_new, HALF_WIDTH), BF16)
        attend(1, True)

    for i in range(pages_per_step):
        for h8 in range(SUBLANES):
            dst = (slice(i * SLOTS, (i + 1) * SLOTS), slice(h8 * HEAD_DIM, (h8 + 1) * HEAD_DIM))
            kcat_ref[dst] = kp_refs[i][pl.ds(h8, SLOTS, stride=SUBLANES), :].astype(BF16)
            vcat_ref[dst] = vp_refs[i][pl.ds(h8, SLOTS, stride=SUBLANES), :].astype(BF16)
    attend(pages_per_step, False)

    @pl.when(c == pl.num_programs(1) - 1)
    def _():
        rh = lax.broadcasted_iota(jnp.int32, (n_rows, HEAD_DIM), 0) % SUBLANES
        out = jnp.zeros((n_rows, HEAD_DIM), F32)
        for h8 in range(SUBLANES):
            out = out + jnp.where(rh == h8, acc_ref[:, h8 * HEAD_DIM:(h8 + 1) * HEAD_DIM], 0.0)
        o_ref[...] = out * jax.nn.sigmoid(g_ref[...])


def _attn_sample(proj_s, cache_k, cache_v, page_table, sb_bias, *, pages_per_step=8):
    b, n_tok, _ = proj_s.shape
    n_pages = page_table.shape[1]
    n_rows = n_tok * N_HEADS
    steps = n_pages // pages_per_step
    q_rows = proj_s[:, :, :ATT_WIDTH].reshape(b, n_rows, HEAD_DIM)
    g_rows = proj_s[:, :, 3 * ATT_WIDTH:4 * ATT_WIDTH].reshape(b, n_rows, HEAD_DIM)
    bias_col = jnp.tile(sb_bias, n_tok).reshape(n_rows, 1)
    n_new = 2 * SUBLANES
    assert n_tok * HEAD_HALVES <= n_new

    def new_slots(a):
        a = a.reshape(b, n_tok * HEAD_HALVES, HALF_WIDTH)
        return jnp.pad(a, ((0, 0), (0, n_new - n_tok * HEAD_HALVES), (0, 0)))

    k_new = new_slots(proj_s[:, :, ATT_WIDTH:2 * ATT_WIDTH])
    v_new = new_slots(proj_s[:, :, 2 * ATT_WIDTH:3 * ATT_WIDTH])

    def page_spec(i):
        def imap(bi, c, pt):
            return (pt[bi, n_pages - 1 - (c * pages_per_step + i)], 0, 0)
        return pl.BlockSpec((None, PAGE_SIZE * N_HEADS, HEAD_DIM), imap)

    page_specs = [page_spec(i) for i in range(pages_per_step)]
    rows_spec = pl.BlockSpec((None, n_rows, HEAD_DIM), lambda bi, c, pt: (bi, 0, 0))
    new_spec = pl.BlockSpec((None, n_new, HALF_WIDTH), lambda bi, c, pt: (bi, 0, 0))
    grid_spec = pltpu.PrefetchScalarGridSpec(
        num_scalar_prefetch=1,
        grid=(b, steps),
        in_specs=[rows_spec, new_spec, new_spec, rows_spec,
                  pl.BlockSpec((n_rows, 1), lambda bi, c, pt: (0, 0))]
                 + page_specs * 2,
        out_specs=rows_spec,
        scratch_shapes=[pltpu.VMEM((n_rows, HALF_WIDTH), BF16),
                        pltpu.VMEM((pages_per_step * SLOTS, HALF_WIDTH), BF16),
                        pltpu.VMEM((pages_per_step * SLOTS, HALF_WIDTH), BF16),
                        pltpu.VMEM((n_rows, HALF_WIDTH), F32),
                        pltpu.VMEM((n_rows, 1), F32)],
    )
    out = pl.pallas_call(
        functools.partial(_attn_sample_body, pages_per_step=pages_per_step, n_tok=n_tok),
        grid_spec=grid_spec,
        out_shape=jax.ShapeDtypeStruct((b, n_rows, HEAD_DIM), F32),
        compiler_params=_cparams(("parallel", "arbitrary")),
        name="attn_sample",
    )(page_table, q_rows, k_new, v_new, g_rows, bias_col,
      *([cache_k] * pages_per_step), *([cache_v] * pages_per_step))
    return out.reshape(b, n_tok, ATT_WIDTH)


def _ssm_params_body(are_ref, aim_ref, ldt_ref, bre_ref, bim_ref, abr_ref, abi_ref, bbr_ref, bbi_ref):
    a_re = are_ref[...]
    a_im = aim_ref[...]
    dt = jnp.exp(ldt_ref[...])
    mag = jnp.exp(dt * a_re)
    ab_re = mag * jnp.cos(dt * a_im)
    ab_im = mag * jnp.sin(dt * a_im)
    abr_ref[...] = ab_re
    abi_ref[...] = ab_im
    den = a_re * a_re + a_im * a_im
    n_re = ab_re - 1.0
    co_re = (n_re * a_re + ab_im * a_im) / den
    co_im = (ab_im * a_re - n_re * a_im) / den
    b_re = bre_ref[...]
    b_im = bim_ref[...]
    bbr_ref[...] = co_re[:, None, :] * b_re - co_im[:, None, :] * b_im
    bbi_ref[...] = co_re[:, None, :] * b_im + co_im[:, None, :] * b_re


def _ssm_params(a_re, a_im, log_dt, b_re, b_im):
    g, p = a_re.shape
    c = b_re.shape[1]
    return pl.pallas_call(
        _ssm_params_body,
        name="ssm_params",
        out_shape=[jax.ShapeDtypeStruct((g, p), F32), jax.ShapeDtypeStruct((g, p), F32),
                   jax.ShapeDtypeStruct((g, c, p), F32), jax.ShapeDtypeStruct((g, c, p), F32)],
    )(a_re, a_im, log_dt.reshape(g, 1), b_re, b_im)


def _gelu_tanh(y):
    return 0.5 * y * (1.0 + jnp.tanh(math.sqrt(2.0 / math.pi) * (y + 0.044715 * (y * y * y))))


def _ssm_prompt_body(u_ref, g_ref, b_ref, c_ref, gw_ref, ab_ref, dg_ref, o_ref, hT_ref, bu_ref, hh_ref, st_ref, tm_ref,
                     *, lc, nb):
    ci = pl.program_id(1)
    rows = lc * SUBLANES
    n_slab = STATE_LANES // LANES
    seqs = [(sel, b) for sel in range(2) for b in range(nb)]
    seq_rows = lambda i: pl.ds(i, lc, stride=SUBLANES)
    for i, (sel, b) in enumerate(seqs):
        tm_ref.at[0][seq_rows(i), :] = u_ref[b, :, sel * LANES:(sel + 1) * LANES]
        tm_ref.at[1][seq_rows(i), :] = g_ref[b, :, sel * LANES:(sel + 1) * LANES]

    @pl.when(ci == 0)
    def _():
        st_ref[...] = jnp.zeros_like(st_ref)

    u = tm_ref[0]
    first = (lax.broadcasted_iota(jnp.int32, (rows, LANES), 0) % SUBLANES) < (SUBLANES // 2)
    lhs = jnp.concatenate([jnp.where(first, u, 0.0), jnp.where(first, 0.0, u)], axis=1).astype(BF16)
    bu_ref[...] = jnp.dot(lhs, b_ref[...], preferred_element_type=F32)

    ar = [ab_ref[j] for j in range(n_slab)]
    ai = [ab_ref[n_slab + j] for j in range(n_slab)]

    def step(t, h):
        r0 = pl.multiple_of(t * SUBLANES, SUBLANES)
        new = []
        for j in range(n_slab):
            hr, hi = h[j], h[n_slab + j]
            bre = bu_ref[pl.ds(r0, SUBLANES), j * LANES:(j + 1) * LANES]
            bim = bu_ref[pl.ds(r0, SUBLANES), STATE_LANES + j * LANES:STATE_LANES + (j + 1) * LANES]
            nr = ar[j] * hr - ai[j] * hi + bre
            ni = ar[j] * hi + ai[j] * hr + bim
            hh_ref[pl.ds(r0, SUBLANES), j * LANES:(j + 1) * LANES] = nr
            hh_ref[pl.ds(r0, SUBLANES), STATE_LANES + j * LANES:STATE_LANES + (j + 1) * LANES] = ni
            new.append((nr, ni))
        return tuple(x[0] for x in new) + tuple(x[1] for x in new)

    h0 = tuple(st_ref[j] for j in range(2 * n_slab))
    hN = lax.fori_loop(0, lc, step, h0, unroll=8)
    for j in range(2 * n_slab):
        st_ref[j] = hN[j]
        hT_ref[j] = hN[j]

    y2 = jnp.dot(hh_ref[...].astype(BF16), c_ref[...], preferred_element_type=F32)
    d = jnp.broadcast_to(dg_ref[0][None], (lc, SUBLANES, LANES)).reshape(rows, LANES)
    gb = jnp.broadcast_to(dg_ref[1][None], (lc, SUBLANES, LANES)).reshape(rows, LANES)
    y = jnp.where(first, y2[:, :LANES], y2[:, LANES:]) + d * u
    z = _gelu_tanh(y)
    g2 = jnp.dot(z.astype(BF16), gw_ref[...], preferred_element_type=F32)
    gate = jax.nn.sigmoid(jnp.where(first, g2[:, :LANES], g2[:, LANES:]) + gb)
    tm_ref[2] = z * gate * jax.nn.sigmoid(tm_ref[1])
    for i, (sel, b) in enumerate(seqs):
        o_ref[b, :, sel * LANES:(sel + 1) * LANES] = tm_ref.at[2][seq_rows(i), :].astype(o_ref.dtype)


def _ssm_prompt(proj, b_st, c_st, gw_st, ab_tm, dg_tm, *, lc=256):
    nb, t, _ = proj.shape
    npair = N_GBLOCKS // 2
    n_slab2 = 2 * STATE_LANES // LANES
    wid = 2 * LANES
    seq_spec = lambda off: pl.BlockSpec((nb, lc, wid), lambda p, c: (0, c, off + p))
    u_off = 4 * ATT_WIDTH // wid
    par_spec = lambda shp: pl.BlockSpec((None,) + shp, lambda p, c: (p,) + (0,) * len(shp))
    return pl.pallas_call(
        functools.partial(_ssm_prompt_body, lc=lc, nb=nb),
        grid=(npair, t // lc),
        in_specs=[seq_spec(u_off), seq_spec(u_off + npair),
                  par_spec((2 * LANES, 2 * STATE_LANES)), par_spec((2 * STATE_LANES, 2 * LANES)),
                  par_spec((LANES, 2 * LANES)), par_spec((n_slab2, SUBLANES, LANES)),
                  par_spec((2, SUBLANES, LANES))],
        out_specs=[seq_spec(0), par_spec((n_slab2, SUBLANES, LANES))],
        out_shape=[jax.ShapeDtypeStruct((nb, t, SSM_WIDTH), BF16),
                   jax.ShapeDtypeStruct((npair, n_slab2, SUBLANES, LANES), F32)],
        scratch_shapes=[pltpu.VMEM((lc * SUBLANES, 2 * STATE_LANES), F32),
                        pltpu.VMEM((lc * SUBLANES, 2 * STATE_LANES), F32),
                        pltpu.VMEM((n_slab2, SUBLANES, LANES), F32),
                        pltpu.VMEM((3, lc * SUBLANES, LANES), F32)],
        compiler_params=_cparams(("parallel", "arbitrary")),
        name="ssm_prompt",
    )(proj, proj, b_st, c_st, gw_st, ab_tm, dg_tm)


def _ssm_sample_body(u_ref, g_ref, hre_ref, him_ref, b_ref, c_ref, gw_ref, ab_ref, dg_ref,
                     o_ref, ore_ref, oim_ref, *, n_tok, nb):
    hp = lax.Precision.HIGHEST
    u = u_ref[...].reshape(n_tok * nb, LANES)
    bu = jnp.dot(u, b_ref[...], preferred_element_type=F32, precision=hp)
    ar = ab_ref[0:1, :]
    ai = ab_ref[1:2, :]
    hr = hre_ref[...]
    hi = him_ref[...]
    hs = []
    for t in range(n_tok):
        bre = bu[t * nb:(t + 1) * nb, :STATE_LANES]
        bim = bu[t * nb:(t + 1) * nb, STATE_LANES:]
        hr, hi = ar * hr - ai * hi + bre, ar * hi + ai * hr + bim
        hs.append(jnp.concatenate([hr, hi], axis=1))
    ore_ref[...] = hr
    oim_ref[...] = hi
    hh = jnp.concatenate(hs, axis=0)
    y = jnp.dot(hh, c_ref[...], preferred_element_type=F32, precision=hp) + dg_ref[0:1, :] * u
    z = _gelu_tanh(y)
    gate = jax.nn.sigmoid(jnp.dot(z, gw_ref[...], preferred_element_type=F32, precision=hp) + dg_ref[1:2, :])
    out = z * gate * jax.nn.sigmoid(g_ref[...].reshape(n_tok * nb, LANES))
    o_ref[...] = out.reshape(n_tok, nb, LANES)


def _ssm_sample(u_t, g_t, h_re, h_im, b_blk, c_blk, gw_blk, ab_row, dg_row):
    n_tok, nb, _ = u_t.shape
    seq_spec = pl.BlockSpec((n_tok, nb, LANES), lambda gb: (0, 0, gb))
    st_spec = pl.BlockSpec((nb, STATE_LANES), lambda gb: (0, gb))
    par_spec = lambda shp: pl.BlockSpec((None,) + shp, lambda gb: (gb,) + (0,) * len(shp))
    return pl.pallas_call(
        functools.partial(_ssm_sample_body, n_tok=n_tok, nb=nb),
        grid=(N_GBLOCKS,),
        in_specs=[seq_spec, seq_spec, st_spec, st_spec,
                  par_spec((LANES, 2 * STATE_LANES)), par_spec((2 * STATE_LANES, LANES)),
                  par_spec((LANES, LANES)), par_spec((2, STATE_LANES)), par_spec((2, LANES))],
        out_specs=[seq_spec, st_spec, st_spec],
        out_shape=[jax.ShapeDtypeStruct(u_t.shape, F32),
                   jax.ShapeDtypeStruct(h_re.shape, F32), jax.ShapeDtypeStruct(h_im.shape, F32)],
        compiler_params=_cparams(("parallel",)),
        name="ssm_sample",
    )(u_t, g_t, h_re, h_im, b_blk, c_blk, gw_blk, ab_row, dg_row)


def _ssm_block_weights(ab_re, ab_im, bb_re, bb_im, c_re, c_im, d, glu_w, glu_b):
    nb, gpb = N_GBLOCKS, GROUPS_PER_BLOCK
    eye = jnp.eye(gpb, dtype=F32)
    bb = jnp.stack([bb_re, bb_im], axis=2).reshape(nb, gpb, SSM_GROUP, 2, STATE_DIM)
    b_blk = jnp.einsum('bgcrp,gh->bgcrhp', bb, eye).reshape(nb, LANES, 2 * STATE_LANES)
    cc = jnp.stack([c_re, -c_im], axis=2).reshape(nb, gpb, SSM_GROUP, 2, STATE_DIM)
    c_blk = jnp.einsum('bgcrp,gh->brgphc', cc, eye).reshape(nb, 2 * STATE_LANES, LANES)
    gw = glu_w.reshape(nb, gpb, SSM_GROUP, SSM_GROUP)
    gw_blk = jnp.einsum('bgce,gh->bgche', gw, eye).reshape(nb, LANES, LANES)
    ab_row = jnp.stack([ab_re.reshape(nb, STATE_LANES), ab_im.reshape(nb, STATE_LANES)], axis=1)
    dg_row = jnp.stack([d.reshape(nb, LANES), glu_b.reshape(nb, LANES)], axis=1)
    return b_blk, c_blk, gw_blk, ab_row, dg_row


def _pair_weights(b_blk, c_blk, gw_blk, ab_row, dg_row, nbatch):
    npair = N_GBLOCKS // 2
    n_slab = STATE_LANES // LANES
    b_st = b_blk.reshape(npair, 2 * LANES, 2 * STATE_LANES).astype(BF16)
    c_st = c_blk.reshape(npair, 2, 2 * STATE_LANES, LANES).transpose(0, 2, 1, 3).reshape(
        npair, 2 * STATE_LANES, 2 * LANES).astype(BF16)
    gw_st = gw_blk.reshape(npair, 2, LANES, LANES).transpose(0, 2, 1, 3).reshape(
        npair, LANES, 2 * LANES).astype(BF16)
    ab = ab_row.reshape(npair, 2, 2, n_slab, LANES).transpose(0, 2, 3, 1, 4)
    ab_tm = jnp.repeat(ab, nbatch, axis=3).reshape(npair, 2 * n_slab, 2 * nbatch, LANES)
    dg = dg_row.reshape(npair, 2, 2, LANES).transpose(0, 2, 1, 3)
    dg_tm = jnp.repeat(dg, nbatch, axis=2)
    return b_st, c_st, gw_st, ab_tm, dg_tm


def _dense_tail(x2d, ssm, att, w_out, norm_mlp_w, w_up, w_down, norm_final_w):
    x1 = _out_proj(ssm, att, w_out, x2d)
    hid = _matmul(x1, w_up, name="mlp_up", bm=512, bn=1024, bk=D_MODEL, norm_w=norm_mlp_w, epilogue="relu2",
                  out_dtype=BF16)
    x2 = _matmul(hid, w_down, name="mlp_down", bm=1024, bn=1024, bk=2048, res=x1, epilogue="res")
    return _rmsnorm(x2, norm_final_w)


def kernel(x_prompt, x_sample, cache_k, cache_v, state_ssm_re, state_ssm_im, page_table, norm_mix_w, w_in, sb_bias,
           ssm_a_re, ssm_a_im, ssm_b_re, ssm_b_im, ssm_c_re, ssm_c_im, ssm_d, ssm_log_dt, glu_w, glu_b, w_out,
           norm_mlp_w, w_up, w_down, norm_final_w):
    depth = w_in.shape[0]
    assert depth == 1
    nb_p, t_p, _ = x_prompt.shape
    nb_s, t_s, _ = x_sample.shape
    assert nb_p * 2 == SUBLANES
    l = 0
    w_in_b = w_in[l].astype(BF16)
    w_out_b = w_out[l].astype(BF16)
    w_up_b = w_up[l].astype(BF16)
    w_down_b = w_down[l].astype(BF16)

    ab_re, ab_im, bb_re, bb_im = _ssm_params(ssm_a_re[l], ssm_a_im[l], ssm_log_dt[l],
                                             ssm_b_re[l].transpose(0, 2, 1), ssm_b_im[l].transpose(0, 2, 1))
    blk = _ssm_block_weights(ab_re, ab_im, bb_re, bb_im, ssm_c_re[l], ssm_c_im[l], ssm_d[l], glu_w[l], glu_b[l])
    pair = _pair_weights(*blk, nbatch=nb_p)

    xp = x_prompt.reshape(nb_p * t_p, D_MODEL)
    proj_p = _matmul(xp, w_in_b, name="in_proj", bm=512, bn=1024, bk=D_MODEL, norm_w=norm_mix_w[l])
    proj_p3 = proj_p.reshape(nb_p, t_p, IN_WIDTH)
    att_p, k_p, v_p = _attn_prompt(proj_p3, sb_bias[l])

    npair = N_GBLOCKS // 2
    ssm_p, hT = _ssm_prompt(proj_p3, *pair)
    n_slab = STATE_LANES // LANES
    hT = hT.reshape(npair, 2, n_slab, 2, nb_p, 2, STATE_DIM).transpose(1, 4, 0, 3, 2, 5, 6).reshape(
        2, nb_p, N_GROUPS, STATE_DIM)
    y_p = _dense_tail(xp, ssm_p.reshape(nb_p * t_p, SSM_WIDTH), att_p.reshape(nb_p * t_p, ATT_WIDTH),
                      w_out_b, norm_mlp_w[l], w_up_b, w_down_b, norm_final_w)

    xs = x_sample.reshape(nb_s * t_s, D_MODEL)
    proj_s = _matmul(xs, w_in_b, name="in_proj", bm=128, bn=1024, bk=D_MODEL, norm_w=norm_mix_w[l])
    proj_s3 = proj_s.reshape(nb_s, t_s, IN_WIDTH)
    n_phys = cache_k.shape[1]
    pool = (depth * n_phys, PAGE_SIZE * N_HEADS, HEAD_DIM)
    att_s = _attn_sample(proj_s3, cache_k.reshape(pool), cache_v.reshape(pool), page_table + l * n_phys,
                         sb_bias[l])
    u_t = proj_s3[:, :, 4 * ATT_WIDTH:4 * ATT_WIDTH + SSM_WIDTH].transpose(1, 0, 2)
    g_t = proj_s3[:, :, 4 * ATT_WIDTH + SSM_WIDTH:].transpose(1, 0, 2)
    ssm_t, hs_re, hs_im = _ssm_sample(u_t, g_t, state_ssm_re[l].reshape(nb_s, N_GROUPS * STATE_DIM),
                                      state_ssm_im[l].reshape(nb_s, N_GROUPS * STATE_DIM), *blk)
    ssm_s = ssm_t.transpose(1, 0, 2).reshape(nb_s * t_s, SSM_WIDTH).astype(BF16)
    y_s = _dense_tail(xs, ssm_s, att_s.reshape(nb_s * t_s, ATT_WIDTH).astype(BF16),
                      w_out_b, norm_mlp_w[l], w_up_b, w_down_b, norm_final_w)

    kv_shape_p = (1, nb_p, t_p, N_HEADS, HEAD_DIM)
    kv_shape_s = (1, nb_s, t_s, N_HEADS, HEAD_DIM)
    return (y_p.reshape(nb_p, t_p, D_MODEL), y_s.reshape(nb_s, t_s, D_MODEL),
            k_p.reshape(kv_shape_p), v_p.reshape(kv_shape_p),
            hT[0][None], hT[1][None],
            proj_s3[:, :, ATT_WIDTH:2 * ATT_WIDTH].reshape(kv_shape_s),
            proj_s3[:, :, 2 * ATT_WIDTH:3 * ATT_WIDTH].reshape(kv_shape_s),
            hs_re.reshape(1, nb_s, N_GROUPS, STATE_DIM), hs_im.reshape(1, nb_s, N_GROUPS, STATE_DIM))
```

```python
import functools
import math

import jax
import jax.numpy as jnp
from jax import lax
from jax.experimental import pallas as pl
from jax.experimental.pallas import tpu as pltpu

F32 = jnp.float32
BF16 = jnp.bfloat16

D_MODEL = 4096
HEAD_DIM = 128
N_HEADS = 16
ATT_WIDTH = N_HEADS * HEAD_DIM
SSM_WIDTH = 2048
SSM_GROUP = 16
N_GROUPS = 128
STATE_DIM = 64
IN_WIDTH = 4 * ATT_WIDTH + 2 * SSM_WIDTH
D_FF = 4 * D_MODEL
PAGE_SIZE = 128
EPS = 1e-6

LANES = 128
SUBLANES = 8
HEAD_HALVES = N_HEADS // SUBLANES
GROUPS_PER_BLOCK = LANES // SSM_GROUP
N_GBLOCKS = N_GROUPS // GROUPS_PER_BLOCK
STATE_LANES = GROUPS_PER_BLOCK * STATE_DIM
VMEM_LIMIT = 56 * 1024 * 1024


def _cparams(sem):
    return pltpu.CompilerParams(dimension_semantics=sem, vmem_limit_bytes=VMEM_LIMIT)


def _rms_rows(x, w):
    return x * lax.rsqrt(jnp.mean(x * x, axis=-1, keepdims=True) + EPS) * w


def _mm_body(*refs, norm, epilogue, nk, cast_w):
    it = iter(refs)
    x_ref = next(it)
    nw_ref = next(it) if norm else None
    w_ref = next(it)
    res_ref = next(it) if epilogue == "res" else None
    o_ref = next(it)
    wb_ref = next(it) if cast_w else None
    xs_ref = next(it) if norm else None
    acc_ref = next(it) if nk > 1 else None
    j = pl.program_id(1)
    k = pl.program_id(2)

    if norm:
        @pl.when(j == 0)
        def _():
            xs_ref[...] = _rms_rows(x_ref[...], nw_ref[...]).astype(BF16)
        lhs = xs_ref[...]
    else:
        lhs = x_ref[...]
    w = w_ref[...]
    if cast_w:
        w = w.astype(BF16)
        wb_ref[...] = w
    part = jnp.dot(lhs, w, preferred_element_type=F32)

    def finish(acc):
        if epilogue == "relu2":
            r = jnp.maximum(acc, 0.0)
            acc = r * r
        elif epilogue == "res":
            acc = acc + res_ref[...]
        o_ref[...] = acc.astype(o_ref.dtype)

    if nk == 1:
        finish(part)
    else:
        @pl.when(k == 0)
        def _():
            acc_ref[...] = part

        @pl.when(k > 0)
        def _():
            acc_ref[...] += part

        @pl.when(k == nk - 1)
        def _():
            finish(acc_ref[...])


def _matmul(x, w, *, name, bm, bn, bk, norm_w=None, res=None, epilogue="none", out_dtype=F32, cast_w=False):
    m, kdim = x.shape
    n = w.shape[1]
    bm = min(bm, m)
    nk = kdim // bk
    norm = norm_w is not None
    assert not (norm and nk != 1)
    assert not (cast_w and m != bm)
    in_specs = [pl.BlockSpec((bm, bk), lambda i, j, k: (i, k))]
    args = [x]
    if norm:
        in_specs.append(pl.BlockSpec((1, kdim), lambda i, j, k: (0, 0)))
        args.append(norm_w.reshape(1, kdim))
    in_specs.append(pl.BlockSpec((bk, bn), lambda i, j, k: (k, j)))
    args.append(w)
    if epilogue == "res":
        in_specs.append(pl.BlockSpec((bm, bn), lambda i, j, k: (i, j)))
        args.append(res)
    scratch = []
    if norm:
        scratch.append(pltpu.VMEM((bm, kdim), BF16))
    if nk > 1:
        scratch.append(pltpu.VMEM((bm, bn), F32))
    out_specs = pl.BlockSpec((bm, bn), lambda i, j, k: (i, j))
    out_shape = jax.ShapeDtypeStruct((m, n), out_dtype)
    if cast_w:
        out_specs = [out_specs, pl.BlockSpec((bk, bn), lambda i, j, k: (k, j))]
        out_shape = [out_shape, jax.ShapeDtypeStruct((kdim, n), BF16)]
    return pl.pallas_call(
        functools.partial(_mm_body, norm=norm, epilogue=epilogue, nk=nk, cast_w=cast_w),
        grid=(m // bm, n // bn, nk),
        in_specs=in_specs,
        out_specs=out_specs,
        out_shape=out_shape,
        scratch_shapes=scratch,
        compiler_params=_cparams(("parallel", "arbitrary", "arbitrary")),
        name=name,
    )(*args)


def _out_proj_body(a_ref, b_ref, wa_ref, wb_ref, res_ref, o_ref):
    acc = jnp.dot(a_ref[...], wa_ref[...], preferred_element_type=F32)
    acc = acc + jnp.dot(b_ref[...], wb_ref[...], preferred_element_type=F32)
    o_ref[...] = acc + res_ref[...]


def _out_proj(a, b, w, res, *, bm=1024, bn=1024):
    m, ka = a.shape
    kb = b.shape[1]
    n = w.shape[1]
    bm = min(bm, m)
    assert ka == kb
    row_spec = lambda wid: pl.BlockSpec((bm, wid), lambda i, j: (i, 0))
    return pl.pallas_call(
        _out_proj_body,
        grid=(m // bm, n // bn),
        in_specs=[row_spec(ka), row_spec(kb),
                  pl.BlockSpec((ka, bn), lambda i, j: (0, j)), pl.BlockSpec((kb, bn), lambda i, j: (1, j)),
                  pl.BlockSpec((bm, bn), lambda i, j: (i, j))],
        out_specs=pl.BlockSpec((bm, bn), lambda i, j: (i, j)),
        out_shape=jax.ShapeDtypeStruct((m, n), F32),
        compiler_params=_cparams(("parallel", "arbitrary")),
        name="out_proj",
    )(a, b, w, w, res)


def _rmsnorm_body(x_ref, w_ref, o_ref):
    o_ref[...] = _rms_rows(x_ref[...], w_ref[...])


def _rmsnorm(x, w, *, bm=256):
    m, d = x.shape
    bm = min(bm, m)
    return pl.pallas_call(
        _rmsnorm_body,
        grid=(m // bm,),
        in_specs=[pl.BlockSpec((bm, d), lambda i: (i, 0)), pl.BlockSpec((1, d), lambda i: (0, 0))],
        out_specs=pl.BlockSpec((bm, d), lambda i: (i, 0)),
        out_shape=jax.ShapeDtypeStruct((m, d), F32),
        compiler_params=_cparams(("parallel",)),
        name="rmsnorm",
    )(x, w.reshape(1, d))


LOG2E = 1.0 / math.log(2.0)
ATT_SCALE2 = HEAD_DIM ** -0.5 * LOG2E


def _softplus2(z2):
    return jnp.maximum(z2, 0.0) + jnp.log(1.0 + jnp.exp2(-jnp.abs(z2))) * LOG2E


def _attn_prompt_body(bias_ref, q_ref, k_ref, v_ref, g_ref, o_ref, ko_ref, vo_ref, kb_ref, vb_ref, acc_ref, carry_ref,
                      *, tk, nsub, nhead):
    hg = pl.program_id(1)
    qi = pl.program_id(2)

    @pl.when(qi == 0)
    def _():
        k = k_ref[...]
        v = v_ref[...]
        ko_ref[...] = k
        vo_ref[...] = v
        kb_ref[...] = k.astype(BF16)
        vb_ref[...] = v.astype(BF16)

    tq = tk * nsub
    row = lax.broadcasted_iota(jnp.int32, (tk, tk), 0)
    col = lax.broadcasted_iota(jnp.int32, (tk, tk), 1)
    later = (row > col).astype(BF16)
    lanes = lambda h: slice(h * HEAD_DIM, (h + 1) * HEAD_DIM)
    q = [(q_ref[:, lanes(h)] * ATT_SCALE2).astype(BF16) for h in range(nhead)]
    bias = [bias_ref[hg * nhead + h] * LOG2E for h in range(nhead)]

    acc_ref[...] = jnp.zeros_like(acc_ref)
    carry_ref[...] = jnp.zeros_like(carry_ref)

    def block(h, lo, kb, valid):
        start = pl.multiple_of(kb * tk, tk)
        kblk = kb_ref[pl.ds(start, tk), lanes(h)]
        vblk = vb_ref[pl.ds(start, tk), lanes(h)]
        z = lax.dot_general(q[h][lo:], kblk, (((1,), (1,)), ((), ())), preferred_element_type=F32) + bias[h]
        sp = _softplus2(z)
        if valid is not None:
            sp = jnp.where(valid, sp, 0.0)
        inner = jnp.dot(sp.astype(BF16), later, preferred_element_type=F32)
        carry = carry_ref[h, lo:, :]
        w = jnp.exp2(z - sp - inner - carry)
        if valid is not None:
            w = jnp.where(valid, w, 0.0)
        acc_ref[lo:, lanes(h)] += jnp.dot(w.astype(BF16), vblk, preferred_element_type=F32)
        carry_ref[h, lo:, :] = carry + inner[:, :1] + sp[:, :1]

    for j in reversed(range(nsub)):
        m = tq - j * tk
        ri = lax.broadcasted_iota(jnp.int32, (m, tk), 0)
        ci = lax.broadcasted_iota(jnp.int32, (m, tk), 1)
        valid = (ri >= tk) | (ci < ri)
        for h in range(nhead):
            block(h, j * tk, nsub * qi + j, valid)

    @pl.loop(0, nsub * qi)
    def _(it):
        for h in range(nhead):
            block(h, 0, nsub * qi - 1 - it, None)

    o_ref[...] = (acc_ref[...] * jax.nn.sigmoid(g_ref[...])).astype(o_ref.dtype)


def _attn_prompt(proj, sb_bias, *, tk=256, nsub=4, nhead=2):
    b, t, _ = proj.shape
    tq = tk * nsub
    wid = nhead * HEAD_DIM
    ngrp = N_HEADS // nhead
    qspec = lambda off: pl.BlockSpec((None, tq, wid), lambda bi, h, qi: (bi, qi, off + h))
    kvspec = lambda off: pl.BlockSpec((None, t, wid), lambda bi, h, qi: (bi, 0, off + h))
    return pl.pallas_call(
        functools.partial(_attn_prompt_body, tk=tk, nsub=nsub, nhead=nhead),
        grid=(b, ngrp, t // tq),
        in_specs=[pl.BlockSpec(memory_space=pltpu.SMEM),
                  qspec(0), kvspec(ngrp), kvspec(2 * ngrp), qspec(3 * ngrp)],
        out_specs=[qspec(0), kvspec(0), kvspec(0)],
        out_shape=[jax.ShapeDtypeStruct((b, t, ATT_WIDTH), BF16),
                   jax.ShapeDtypeStruct((b, t, ATT_WIDTH), F32),
                   jax.ShapeDtypeStruct((b, t, ATT_WIDTH), F32)],
        scratch_shapes=[pltpu.VMEM((t, wid), BF16), pltpu.VMEM((t, wid), BF16),
                        pltpu.VMEM((tq, wid), F32), pltpu.VMEM((nhead, tq, 1), F32)],
        compiler_params=_cparams(("parallel", "parallel", "arbitrary")),
        name="attn_prompt",
    )(sb_bias, proj, proj, proj, proj)


SLOTS = PAGE_SIZE * HEAD_HALVES
HALF_WIDTH = SUBLANES * HEAD_DIM


def _attn_sample_body(pt_ref, q_ref, kn_ref, vn_ref, g_ref, bias_ref, *refs, pages_per_step, n_tok):
    kp_refs = refs[:pages_per_step]
    vp_refs = refs[pages_per_step:2 * pages_per_step]
    o_ref = refs[2 * pages_per_step]
    qbd_ref, kcat_ref, vcat_ref, acc_ref, carry_ref = refs[2 * pages_per_step + 1:]
    c = pl.program_id(1)
    n_rows = n_tok * N_HEADS
    bias = bias_ref[...] * LOG2E
    row = lax.broadcasted_iota(jnp.int32, (SLOTS, SLOTS), 0)
    col = lax.broadcasted_iota(jnp.int32, (SLOTS, SLOTS), 1)
    later = ((row // HEAD_HALVES) > (col // HEAD_HALVES)).astype(BF16)

    def attend(npages, causal):
        n = npages * SLOTS
        z = lax.dot_general(qbd_ref[...], kcat_ref[:n, :], (((1,), (1,)), ((), ())),
                            preferred_element_type=F32) + bias
        rhalf = (lax.broadcasted_iota(jnp.int32, (n_rows, n), 0) % N_HEADS) // SUBLANES
        slot = lax.broadcasted_iota(jnp.int32, (n_rows, n), 1)
        valid = rhalf == slot % HEAD_HALVES
        if causal:
            valid = valid & (slot // HEAD_HALVES < lax.broadcasted_iota(jnp.int32, (n_rows, n), 0) // N_HEADS)
        sp = jnp.where(valid, _softplus2(z), 0.0)
        pages = lambda a: [a[:, i * SLOTS:(i + 1) * SLOTS] for i in range(npages)]
        stacked = jnp.concatenate(pages(sp), axis=0).astype(BF16)
        inner = jnp.dot(stacked, later, preferred_element_type=F32)
        inner = jnp.concatenate([inner[i * n_rows:(i + 1) * n_rows] for i in range(npages)], axis=1)
        carry = carry_ref[...]
        carries = []
        for i in range(npages):
            carries.append(jnp.broadcast_to(carry, (n_rows, SLOTS)))
            f = i * SLOTS
            carry = carry + inner[:, f:f + 1] + sp[:, f:f + 1] + sp[:, f + 1:f + 2]
        carry_ref[...] = carry
        w = jnp.where(valid, jnp.exp2(z - sp - inner - jnp.concatenate(carries, axis=1)), 0.0)
        acc_ref[...] += jnp.dot(w.astype(BF16), vcat_ref[:n, :], preferred_element_type=F32)

    @pl.when(c == 0)
    def _():
        q = q_ref[...] * ATT_SCALE2
        qt = jnp.concatenate([q] * SUBLANES, axis=1)
        rh = lax.broadcasted_iota(jnp.int32, (n_rows, HALF_WIDTH), 0) % SUBLANES
        lh = lax.broadcasted_iota(jnp.int32, (n_rows, HALF_WIDTH), 1) // HEAD_DIM
        qbd_ref[...] = jnp.where(rh == lh, qt, 0.0).astype(BF16)
        acc_ref[...] = jnp.zeros_like(acc_ref)
        carry_ref[...] = jnp.zeros_like(carry_ref)
        n_new = kn_ref.shape[0]
        kcat_ref[:n_new, :] = kn_ref[...].astype(BF16)
        vcat_ref[:n_new, :] = vn_ref[...].astype(BF16)
        kcat_ref[n_new:SLOTS, :] = jnp.zeros((SLOTS - n_new, HALF_WIDTH), BF16)
        vcat_ref[n_new:SLOTS, :] = jnp.zeros((SLOTS - n_new, HALF_WIDTH), BF16)
        attend(1, True)

    for i in range(pages_per_step):
        for h8 in range(SUBLANES):
            dst = (slice(i * SLOTS, (i + 1) * SLOTS), slice(h8 * HEAD_DIM, (h8 + 1) * HEAD_DIM))
            kcat_ref[dst] = kp_refs[i][pl.ds(h8, SLOTS, stride=SUBLANES), :].astype(BF16)
            vcat_ref[dst] = vp_refs[i][pl.ds(h8, SLOTS, stride=SUBLANES), :].astype(BF16)
    attend(pages_per_step, False)

    @pl.when(c == pl.num_programs(1) - 1)
    def _():
        rh = lax.broadcasted_iota(jnp.int32, (n_rows, HEAD_DIM), 0) % SUBLANES
        out = jnp.zeros((n_rows, HEAD_DIM), F32)
        for h8 in range(SUBLANES):
            out = out + jnp.where(rh == h8, acc_ref[:, h8 * HEAD_DIM:(h8 + 1) * HEAD_DIM], 0.0)
        o_ref[...] = out * jax.nn.sigmoid(g_ref[...])


def _attn_sample(proj_s, cache_k, cache_v, page_table, sb_bias, *, pages_per_step=8):
    b, n_tok, _ = proj_s.shape
    n_pages = page_table.shape[1]
    n_rows = n_tok * N_HEADS
    steps = n_pages // pages_per_step
    q_rows = proj_s[:, :, :ATT_WIDTH].reshape(b, n_rows, HEAD_DIM)
    g_rows = proj_s[:, :, 3 * ATT_WIDTH:4 * ATT_WIDTH].reshape(b, n_rows, HEAD_DIM)
    bias_col = jnp.tile(sb_bias, n_tok).reshape(n_rows, 1)
    n_new = 2 * SUBLANES
    assert n_tok * HEAD_HALVES <= n_new

    def new_slots(a):
        a = a.reshape(b, n_tok * HEAD_HALVES, HALF_WIDTH)
        return jnp.pad(a, ((0, 0), (0, n_new - n_tok * HEAD_HALVES), (0, 0)))

    k_new = new_slots(proj_s[:, :, ATT_WIDTH:2 * ATT_WIDTH])
    v_new = new_slots(proj_s[:, :, 2 * ATT_WIDTH:3 * ATT_WIDTH])

    def page_spec(i):
        def imap(bi, c, pt):
            return (pt[bi, n_pages - 1 - (c * pages_per_step + i)], 0, 0)
        return pl.BlockSpec((None, PAGE_SIZE * N_HEADS, HEAD_DIM), imap)

    page_specs = [page_spec(i) for i in range(pages_per_step)]
    rows_spec = pl.BlockSpec((None, n_rows, HEAD_DIM), lambda bi, c, pt: (bi, 0, 0))
    new_spec = pl.BlockSpec((None, n_new, HALF_WIDTH), lambda bi, c, pt: (bi, 0, 0))
    grid_spec = pltpu.PrefetchScalarGridSpec(
        num_scalar_prefetch=1,
        grid=(b, steps),
        in_specs=[rows_spec, new_spec, new_spec, rows_spec,
                  pl.BlockSpec((n_rows, 1), lambda bi, c, pt: (0, 0))]
                 + page_specs * 2,
        out_specs=rows_spec,
        scratch_shapes=[pltpu.VMEM((n_rows, HALF_WIDTH), BF16),
                        pltpu.VMEM((pages_per_step * SLOTS, HALF_WIDTH), BF16),
                        pltpu.VMEM((pages_per_step * SLOTS, HALF_WIDTH), BF16),
                        pltpu.VMEM((n_rows, HALF_WIDTH), F32),
                        pltpu.VMEM((n_rows, 1), F32)],
    )
    out = pl.pallas_call(
        functools.partial(_attn_sample_body, pages_per_step=pages_per_step, n_tok=n_tok),
        grid_spec=grid_spec,
        out_shape=jax.ShapeDtypeStruct((b, n_rows, HEAD_DIM), F32),
        compiler_params=_cparams(("parallel", "arbitrary")),
        name="attn_sample",
    )(page_table, q_rows, k_new, v_new, g_rows, bias_col,
      *([cache_k] * pages_per_step), *([cache_v] * pages_per_step))
    return out.reshape(b, n_tok, ATT_WIDTH)


def _ssm_params_body(are_ref, aim_ref, ldt_ref, bre_ref, bim_ref, abr_ref, abi_ref, bbr_ref, bbi_ref):
    a_re = are_ref[...]
    a_im = aim_ref[...]
    dt = jnp.exp(ldt_ref[...])
    mag = jnp.exp(dt * a_re)
    ab_re = mag * jnp.cos(dt * a_im)
    ab_im = mag * jnp.sin(dt * a_im)
    abr_ref[...] = ab_re
    abi_ref[...] = ab_im
    den = a_re * a_re + a_im * a_im
    n_re = ab_re - 1.0
    co_re = (n_re * a_re + ab_im * a_im) / den
    co_im = (ab_im * a_re - n_re * a_im) / den
    b_re = bre_ref[...]
    b_im = bim_ref[...]
    bbr_ref[...] = co_re[:, None, :] * b_re - co_im[:, None, :] * b_im
    bbi_ref[...] = co_re[:, None, :] * b_im + co_im[:, None, :] * b_re


def _ssm_params(a_re, a_im, log_dt, b_re, b_im):
    g, p = a_re.shape
    c = b_re.shape[1]
    return pl.pallas_call(
        _ssm_params_body,
        name="ssm_params",
        out_shape=[jax.ShapeDtypeStruct((g, p), F32), jax.ShapeDtypeStruct((g, p), F32),
                   jax.ShapeDtypeStruct((g, c, p), F32), jax.ShapeDtypeStruct((g, c, p), F32)],
    )(a_re, a_im, log_dt.reshape(g, 1), b_re, b_im)


def _gelu_tanh(y):
    return 0.5 * y * (1.0 + jnp.tanh(math.sqrt(2.0 / math.pi) * (y + 0.044715 * (y * y * y))))


def _ssm_prompt_body(u_ref, g_ref, b_ref, c_ref, gw_ref, ab_ref, dg_ref, o_ref, hT_ref, bu_ref, hh_ref, st_ref, tm_ref,
                     *, lc, nb, nsc):
    ci = pl.program_id(1)
    rows = lc * SUBLANES
    n_slab = STATE_LANES // LANES
    seqs = [(sel, b) for sel in range(2) for b in range(nb)]
    seq_rows = lambda i: pl.ds(i, lc, stride=SUBLANES)
    for i, (sel, b) in enumerate(seqs):
        tm_ref.at[0][seq_rows(i), :] = u_ref[b, :, sel * LANES:(sel + 1) * LANES]
        tm_ref.at[1][seq_rows(i), :] = g_ref[b, :, sel * LANES:(sel + 1) * LANES]

    @pl.when(ci == 0)
    def _():
        st_ref[...] = jnp.zeros_like(st_ref)

    ar = [ab_ref[j] for j in range(n_slab)]
    ai = [ab_ref[n_slab + j] for j in range(n_slab)]
    h = [st_ref[j] for j in range(2 * n_slab)]
    sub = rows // nsc
    first = (lax.broadcasted_iota(jnp.int32, (sub, LANES), 0) % SUBLANES) < (SUBLANES // 2)
    d = jnp.broadcast_to(dg_ref[0][None], (sub // SUBLANES, SUBLANES, LANES)).reshape(sub, LANES)
    gb = jnp.broadcast_to(dg_ref[1][None], (sub // SUBLANES, SUBLANES, LANES)).reshape(sub, LANES)
    for sc in range(nsc):
        lo = sc * sub
        u = tm_ref[0, lo:lo + sub, :]
        lhs = jnp.concatenate([jnp.where(first, u, 0.0), jnp.where(first, 0.0, u)], axis=1).astype(BF16)
        bu_ref[lo:lo + sub, :] = jnp.dot(lhs, b_ref[...], preferred_element_type=F32)
        for r0 in range(lo, lo + sub, SUBLANES):
            for j in range(n_slab):
                re_l = slice(j * LANES, (j + 1) * LANES)
                im_l = slice(STATE_LANES + j * LANES, STATE_LANES + (j + 1) * LANES)
                hr, hi = h[j], h[n_slab + j]
                h[j] = ar[j] * hr - ai[j] * hi + bu_ref[r0:r0 + SUBLANES, re_l]
                h[n_slab + j] = ar[j] * hi + ai[j] * hr + bu_ref[r0:r0 + SUBLANES, im_l]
                hh_ref[r0:r0 + SUBLANES, re_l] = h[j]
                hh_ref[r0:r0 + SUBLANES, im_l] = h[n_slab + j]
        y2 = jnp.dot(hh_ref[lo:lo + sub, :].astype(BF16), c_ref[...], preferred_element_type=F32)
        y = jnp.where(first, y2[:, :LANES], y2[:, LANES:]) + d * u
        z = _gelu_tanh(y)
        g2 = jnp.dot(z.astype(BF16), gw_ref[...], preferred_element_type=F32)
        gate = jax.nn.sigmoid(jnp.where(first, g2[:, :LANES], g2[:, LANES:]) + gb)
        tm_ref[2, lo:lo + sub, :] = z * gate * jax.nn.sigmoid(tm_ref[1, lo:lo + sub, :])
    for j in range(2 * n_slab):
        st_ref[j] = h[j]
        hT_ref[j] = h[j]
    for i, (sel, b) in enumerate(seqs):
        o_ref[b, :, sel * LANES:(sel + 1) * LANES] = tm_ref.at[2][seq_rows(i), :].astype(o_ref.dtype)


def _ssm_prompt(proj, b_st, c_st, gw_st, ab_tm, dg_tm, *, lc=256, nsc=4):
    nb, t, _ = proj.shape
    npair = N_GBLOCKS // 2
    n_slab2 = 2 * STATE_LANES // LANES
    wid = 2 * LANES
    seq_spec = lambda off: pl.BlockSpec((nb, lc, wid), lambda p, c: (0, c, off + p))
    u_off = 4 * ATT_WIDTH // wid
    par_spec = lambda shp: pl.BlockSpec((None,) + shp, lambda p, c: (p,) + (0,) * len(shp))
    return pl.pallas_call(
        functools.partial(_ssm_prompt_body, lc=lc, nb=nb, nsc=nsc),
        grid=(npair, t // lc),
        in_specs=[seq_spec(u_off), seq_spec(u_off + npair),
                  par_spec((2 * LANES, 2 * STATE_LANES)), par_spec((2 * STATE_LANES, 2 * LANES)),
                  par_spec((LANES, 2 * LANES)), par_spec((n_slab2, SUBLANES, LANES)),
                  par_spec((2, SUBLANES, LANES))],
        out_specs=[seq_spec(0), par_spec((n_slab2, SUBLANES, LANES))],
        out_shape=[jax.ShapeDtypeStruct((nb, t, SSM_WIDTH), BF16),
                   jax.ShapeDtypeStruct((npair, n_slab2, SUBLANES, LANES), F32)],
        scratch_shapes=[pltpu.VMEM((lc * SUBLANES, 2 * STATE_LANES), F32),
                        pltpu.VMEM((lc * SUBLANES, 2 * STATE_LANES), F32),
                        pltpu.VMEM((n_slab2, SUBLANES, LANES), F32),
                        pltpu.VMEM((3, lc * SUBLANES, LANES), F32)],
        compiler_params=_cparams(("parallel", "arbitrary")),
        name="ssm_prompt",
    )(proj, proj, b_st, c_st, gw_st, ab_tm, dg_tm)


def _ssm_sample_body(u_ref, g_ref, hre_ref, him_ref, b_ref, c_ref, gw_ref, ab_ref, dg_ref,
                     o_ref, ore_ref, oim_ref, *, n_tok, nb):
    hp = lax.Precision.HIGHEST
    u = u_ref[...].reshape(n_tok * nb, LANES)
    bu = jnp.dot(u, b_ref[...], preferred_element_type=F32, precision=hp)
    ar = ab_ref[0:1, :]
    ai = ab_ref[1:2, :]
    hr = hre_ref[...]
    hi = him_ref[...]
    hs = []
    for t in range(n_tok):
        bre = bu[t * nb:(t + 1) * nb, :STATE_LANES]
        bim = bu[t * nb:(t + 1) * nb, STATE_LANES:]
        hr, hi = ar * hr - ai * hi + bre, ar * hi + ai * hr + bim
        hs.append(jnp.concatenate([hr, hi], axis=1))
    ore_ref[...] = hr
    oim_ref[...] = hi
    hh = jnp.concatenate(hs, axis=0)
    y = jnp.dot(hh, c_ref[...], preferred_element_type=F32, precision=hp) + dg_ref[0:1, :] * u
    z = _gelu_tanh(y)
    gate = jax.nn.sigmoid(jnp.dot(z, gw_ref[...], preferred_element_type=F32, precision=hp) + dg_ref[1:2, :])
    out = z * gate * jax.nn.sigmoid(g_ref[...].reshape(n_tok * nb, LANES))
    o_ref[...] = out.reshape(n_tok, nb, LANES)


def _ssm_sample(u_t, g_t, h_re, h_im, b_blk, c_blk, gw_blk, ab_row, dg_row):
    n_tok, nb, _ = u_t.shape
    seq_spec = pl.BlockSpec((n_tok, nb, LANES), lambda gb: (0, 0, gb))
    st_spec = pl.BlockSpec((nb, STATE_LANES), lambda gb: (0, gb))
    par_spec = lambda shp: pl.BlockSpec((None,) + shp, lambda gb: (gb,) + (0,) * len(shp))
    return pl.pallas_call(
        functools.partial(_ssm_sample_body, n_tok=n_tok, nb=nb),
        grid=(N_GBLOCKS,),
        in_specs=[seq_spec, seq_spec, st_spec, st_spec,
                  par_spec((LANES, 2 * STATE_LANES)), par_spec((2 * STATE_LANES, LANES)),
                  par_spec((LANES, LANES)), par_spec((2, STATE_LANES)), par_spec((2, LANES))],
        out_specs=[seq_spec, st_spec, st_spec],
        out_shape=[jax.ShapeDtypeStruct(u_t.shape, F32),
                   jax.ShapeDtypeStruct(h_re.shape, F32), jax.ShapeDtypeStruct(h_im.shape, F32)],
        compiler_params=_cparams(("parallel",)),
        name="ssm_sample",
    )(u_t, g_t, h_re, h_im, b_blk, c_blk, gw_blk, ab_row, dg_row)


def _ssm_block_weights(ab_re, ab_im, bb_re, bb_im, c_re, c_im, d, glu_w, glu_b):
    nb, gpb = N_GBLOCKS, GROUPS_PER_BLOCK
    eye = jnp.eye(gpb, dtype=F32)
    bb = jnp.stack([bb_re, bb_im], axis=2).reshape(nb, gpb, SSM_GROUP, 2, STATE_DIM)
    b_blk = jnp.einsum('bgcrp,gh->bgcrhp', bb, eye).reshape(nb, LANES, 2 * STATE_LANES)
    cc = jnp.stack([c_re, -c_im], axis=2).reshape(nb, gpb, SSM_GROUP, 2, STATE_DIM)
    c_blk = jnp.einsum('bgcrp,gh->brgphc', cc, eye).reshape(nb, 2 * STATE_LANES, LANES)
    gw = glu_w.reshape(nb, gpb, SSM_GROUP, SSM_GROUP)
    gw_blk = jnp.einsum('bgce,gh->bgche', gw, eye).reshape(nb, LANES, LANES)
    ab_row = jnp.stack([ab_re.reshape(nb, STATE_LANES), ab_im.reshape(nb, STATE_LANES)], axis=1)
    dg_row = jnp.stack([d.reshape(nb, LANES), glu_b.reshape(nb, LANES)], axis=1)
    return b_blk, c_blk, gw_blk, ab_row, dg_row


def _pair_weights(b_blk, c_blk, gw_blk, ab_row, dg_row, nbatch):
    npair = N_GBLOCKS // 2
    n_slab = STATE_LANES // LANES
    b_st = b_blk.reshape(npair, 2 * LANES, 2 * STATE_LANES).astype(BF16)
    c_st = c_blk.reshape(npair, 2, 2 * STATE_LANES, LANES).transpose(0, 2, 1, 3).reshape(
        npair, 2 * STATE_LANES, 2 * LANES).astype(BF16)
    gw_st = gw_blk.reshape(npair, 2, LANES, LANES).transpose(0, 2, 1, 3).reshape(
        npair, LANES, 2 * LANES).astype(BF16)
    ab = ab_row.reshape(npair, 2, 2, n_slab, LANES).transpose(0, 2, 3, 1, 4)
    ab_tm = jnp.repeat(ab, nbatch, axis=3).reshape(npair, 2 * n_slab, 2 * nbatch, LANES)
    dg = dg_row.reshape(npair, 2, 2, LANES).transpose(0, 2, 1, 3)
    dg_tm = jnp.repeat(dg, nbatch, axis=2)
    return b_st, c_st, gw_st, ab_tm, dg_tm


def _dense_tail_prompt(x2d, ssm, att, w_out, norm_mlp_w, w_up, w_down, norm_final_w):
    x1 = _out_proj(ssm, att, w_out, x2d)
    hid = _matmul(x1, w_up, name="mlp_up", bm=512, bn=1024, bk=D_MODEL, norm_w=norm_mlp_w, epilogue="relu2",
                  out_dtype=BF16)
    x2 = _matmul(hid, w_down, name="mlp_down", bm=1024, bn=1024, bk=2048, res=x1, epilogue="res")
    return _rmsnorm(x2, norm_final_w)


def _dense_tail_sample(x2d, mixed, w_out, norm_mlp_w, w_up, w_down, norm_final_w):
    x1, w_out_b = _matmul(mixed, w_out, name="out_proj_s", bm=LANES, bn=512, bk=D_MODEL, res=x2d, epilogue="res",
                          cast_w=True)
    hid, w_up_b = _matmul(x1, w_up, name="mlp_up_s", bm=LANES, bn=512, bk=D_MODEL, norm_w=norm_mlp_w,
                          epilogue="relu2", out_dtype=BF16, cast_w=True)
    x2, w_down_b = _matmul(hid, w_down, name="mlp_down_s", bm=LANES, bn=1024, bk=2048, res=x1, epilogue="res",
                           cast_w=True)
    return _rmsnorm(x2, norm_final_w), w_out_b, w_up_b, w_down_b


def kernel(x_prompt, x_sample, cache_k, cache_v, state_ssm_re, state_ssm_im, page_table, norm_mix_w, w_in, sb_bias,
           ssm_a_re, ssm_a_im, ssm_b_re, ssm_b_im, ssm_c_re, ssm_c_im, ssm_d, ssm_log_dt, glu_w, glu_b, w_out,
           norm_mlp_w, w_up, w_down, norm_final_w):
    depth = w_in.shape[0]
    assert depth == 1
    nb_p, t_p, _ = x_prompt.shape
    nb_s, t_s, _ = x_sample.shape
    assert nb_p * 2 == SUBLANES
    assert nb_s * t_s == LANES
    l = 0

    ab_re, ab_im, bb_re, bb_im = _ssm_params(ssm_a_re[l], ssm_a_im[l], ssm_log_dt[l],
                                             ssm_b_re[l].transpose(0, 2, 1), ssm_b_im[l].transpose(0, 2, 1))
    blk = _ssm_block_weights(ab_re, ab_im, bb_re, bb_im, ssm_c_re[l], ssm_c_im[l], ssm_d[l], glu_w[l], glu_b[l])
    pair = _pair_weights(*blk, nbatch=nb_p)

    xs = x_sample.reshape(nb_s * t_s, D_MODEL)
    proj_s, w_in_b = _matmul(xs, w_in[l], name="in_proj_s", bm=LANES, bn=512, bk=D_MODEL, norm_w=norm_mix_w[l],
                             cast_w=True)
    proj_s3 = proj_s.reshape(nb_s, t_s, IN_WIDTH)
    n_phys = cache_k.shape[1]
    pool = (depth * n_phys, PAGE_SIZE * N_HEADS, HEAD_DIM)
    att_s = _attn_sample(proj_s3, cache_k.reshape(pool), cache_v.reshape(pool), page_table + l * n_phys,
                         sb_bias[l])
    u_t = proj_s3[:, :, 4 * ATT_WIDTH:4 * ATT_WIDTH + SSM_WIDTH].transpose(1, 0, 2)
    g_t = proj_s3[:, :, 4 * ATT_WIDTH + SSM_WIDTH:].transpose(1, 0, 2)
    ssm_t, hs_re, hs_im = _ssm_sample(u_t, g_t, state_ssm_re[l].reshape(nb_s, N_GROUPS * STATE_DIM),
                                      state_ssm_im[l].reshape(nb_s, N_GROUPS * STATE_DIM), *blk)
    ssm_s = ssm_t.transpose(1, 0, 2).reshape(nb_s * t_s, SSM_WIDTH)
    mixed_s = jnp.concatenate([ssm_s, att_s.reshape(nb_s * t_s, ATT_WIDTH)], axis=1).astype(BF16)
    y_s, w_out_b, w_up_b, w_down_b = _dense_tail_sample(xs, mixed_s, w_out[l], norm_mlp_w[l], w_up[l], w_down[l],
                                                        norm_final_w)

    xp = x_prompt.reshape(nb_p * t_p, D_MODEL)
    proj_p = _matmul(xp, w_in_b, name="in_proj", bm=512, bn=1024, bk=D_MODEL, norm_w=norm_mix_w[l])
    proj_p3 = proj_p.reshape(nb_p, t_p, IN_WIDTH)
    att_p, k_p, v_p = _attn_prompt(proj_p3, sb_bias[l])
    npair = N_GBLOCKS // 2
    ssm_p, hT = _ssm_prompt(proj_p3, *pair)
    n_slab = STATE_LANES // LANES
    hT = hT.reshape(npair, 2, n_slab, 2, nb_p, 2, STATE_DIM).transpose(1, 4, 0, 3, 2, 5, 6).reshape(
        2, nb_p, N_GROUPS, STATE_DIM)
    y_p = _dense_tail_prompt(xp, ssm_p.reshape(nb_p * t_p, SSM_WIDTH), att_p.reshape(nb_p * t_p, ATT_WIDTH),
                             w_out_b, norm_mlp_w[l], w_up_b, w_down_b, norm_final_w)

    kv_shape_p = (1, nb_p, t_p, N_HEADS, HEAD_DIM)
    kv_shape_s = (1, nb_s, t_s, N_HEADS, HEAD_DIM)
    return (y_p.reshape(nb_p, t_p, D_MODEL), y_s.reshape(nb_s, t_s, D_MODEL),
            k_p.reshape(kv_shape_p), v_p.reshape(kv_shape_p),
            hT[0][None], hT[1][None],
            proj_s3[:, :, ATT_WIDTH:2 * ATT_WIDTH].reshape(kv_shape_s),
            proj_s3[:, :, 2 * ATT_WIDTH:3 * ATT_WIDTH].reshape(kv_shape_s),
            hs_re.reshape(1, nb_s, N_GROUPS, STATE_DIM), hs_im.reshape(1, nb_s, N_GROUPS, STATE_DIM))
```

```python
import functools
import math

import jax
import jax.numpy as jnp
from jax import lax
from jax.experimental import pallas as pl
from jax.experimental.pallas import tpu as pltpu

F32 = jnp.float32
BF16 = jnp.bfloat16

D_MODEL = 4096
HEAD_DIM = 128
N_HEADS = 16
ATT_WIDTH = N_HEADS * HEAD_DIM
SSM_WIDTH = 2048
SSM_GROUP = 16
N_GROUPS = 128
STATE_DIM = 64
IN_WIDTH = 4 * ATT_WIDTH + 2 * SSM_WIDTH
D_FF = 4 * D_MODEL
PAGE_SIZE = 128
EPS = 1e-6

LANES = 128
SUBLANES = 8
HEAD_HALVES = N_HEADS // SUBLANES
GROUPS_PER_BLOCK = LANES // SSM_GROUP
N_GBLOCKS = N_GROUPS // GROUPS_PER_BLOCK
STATE_LANES = GROUPS_PER_BLOCK * STATE_DIM
VMEM_LIMIT = 62 * 1024 * 1024


def _cparams(sem):
    return pltpu.CompilerParams(dimension_semantics=sem, vmem_limit_bytes=VMEM_LIMIT)


def _rms_rows(x, w):
    return x * lax.rsqrt(jnp.mean(x * x, axis=-1, keepdims=True) + EPS) * w


def _mm_body(*refs, norm, epilogue, nk, cast_w):
    it = iter(refs)
    x_ref = next(it)
    nw_ref = next(it) if norm else None
    w_ref = next(it)
    res_ref = next(it) if epilogue == "res" else None
    o_ref = next(it)
    wb_ref = next(it) if cast_w else None
    xs_ref = next(it) if norm else None
    acc_ref = next(it) if nk > 1 else None
    j = pl.program_id(1)
    k = pl.program_id(2)

    if norm:
        @pl.when(j == 0)
        def _():
            xs_ref[...] = _rms_rows(x_ref[...], nw_ref[...]).astype(BF16)
        lhs = xs_ref[...]
    else:
        lhs = x_ref[...]
    w = w_ref[...]
    if cast_w:
        w = w.astype(BF16)
        wb_ref[...] = w
    part = jnp.dot(lhs, w, preferred_element_type=F32)

    def finish(acc):
        if epilogue == "relu2":
            r = jnp.maximum(acc, 0.0)
            acc = r * r
        elif epilogue == "res":
            acc = acc + res_ref[...]
        o_ref[...] = acc.astype(o_ref.dtype)

    if nk == 1:
        finish(part)
    else:
        @pl.when(k == 0)
        def _():
            acc_ref[...] = part

        @pl.when(k > 0)
        def _():
            acc_ref[...] += part

        @pl.when(k == nk - 1)
        def _():
            finish(acc_ref[...])


def _matmul(x, w, *, name, bm, bn, bk, norm_w=None, res=None, epilogue="none", out_dtype=F32, cast_w=False):
    m, kdim = x.shape
    n = w.shape[1]
    bm = min(bm, m)
    nk = kdim // bk
    norm = norm_w is not None
    assert not (norm and nk != 1)
    assert not (cast_w and m != bm)
    in_specs = [pl.BlockSpec((bm, bk), lambda i, j, k: (i, k))]
    args = [x]
    if norm:
        in_specs.append(pl.BlockSpec((1, kdim), lambda i, j, k: (0, 0)))
        args.append(norm_w.reshape(1, kdim))
    in_specs.append(pl.BlockSpec((bk, bn), lambda i, j, k: (k, j)))
    args.append(w)
    if epilogue == "res":
        in_specs.append(pl.BlockSpec((bm, bn), lambda i, j, k: (i, j)))
        args.append(res)
    scratch = []
    if norm:
        scratch.append(pltpu.VMEM((bm, kdim), BF16))
    if nk > 1:
        scratch.append(pltpu.VMEM((bm, bn), F32))
    out_specs = pl.BlockSpec((bm, bn), lambda i, j, k: (i, j))
    out_shape = jax.ShapeDtypeStruct((m, n), out_dtype)
    if cast_w:
        out_specs = [out_specs, pl.BlockSpec((bk, bn), lambda i, j, k: (k, j))]
        out_shape = [out_shape, jax.ShapeDtypeStruct((kdim, n), BF16)]
    return pl.pallas_call(
        functools.partial(_mm_body, norm=norm, epilogue=epilogue, nk=nk, cast_w=cast_w),
        grid=(m // bm, n // bn, nk),
        in_specs=in_specs,
        out_specs=out_specs,
        out_shape=out_shape,
        scratch_shapes=scratch,
        compiler_params=_cparams(("parallel", "arbitrary", "arbitrary")),
        name=name,
    )(*args)


def _out_proj_body(a_ref, b_ref, wa_ref, wb_ref, res_ref, o_ref):
    acc = jnp.dot(a_ref[...], wa_ref[...], preferred_element_type=F32)
    acc = acc + jnp.dot(b_ref[...], wb_ref[...], preferred_element_type=F32)
    o_ref[...] = acc + res_ref[...]


def _out_proj(a, b, w, res, *, bm=1024, bn=1024):
    m, ka = a.shape
    kb = b.shape[1]
    n = w.shape[1]
    bm = min(bm, m)
    assert ka == kb
    row_spec = lambda wid: pl.BlockSpec((bm, wid), lambda i, j: (i, 0))
    return pl.pallas_call(
        _out_proj_body,
        grid=(m // bm, n // bn),
        in_specs=[row_spec(ka), row_spec(kb),
                  pl.BlockSpec((ka, bn), lambda i, j: (0, j)), pl.BlockSpec((kb, bn), lambda i, j: (1, j)),
                  pl.BlockSpec((bm, bn), lambda i, j: (i, j))],
        out_specs=pl.BlockSpec((bm, bn), lambda i, j: (i, j)),
        out_shape=jax.ShapeDtypeStruct((m, n), F32),
        compiler_params=_cparams(("parallel", "arbitrary")),
        name="out_proj",
    )(a, b, w, w, res)


def _rmsnorm_body(x_ref, w_ref, o_ref):
    o_ref[...] = _rms_rows(x_ref[...], w_ref[...])


def _rmsnorm(x, w, *, bm=256):
    m, d = x.shape
    bm = min(bm, m)
    return pl.pallas_call(
        _rmsnorm_body,
        grid=(m // bm,),
        in_specs=[pl.BlockSpec((bm, d), lambda i: (i, 0)), pl.BlockSpec((1, d), lambda i: (0, 0))],
        out_specs=pl.BlockSpec((bm, d), lambda i: (i, 0)),
        out_shape=jax.ShapeDtypeStruct((m, d), F32),
        compiler_params=_cparams(("parallel",)),
        name="rmsnorm",
    )(x, w.reshape(1, d))


LOG2E = 1.0 / math.log(2.0)
ATT_SCALE2 = HEAD_DIM ** -0.5 * LOG2E


def _softplus2(z2):
    return jnp.maximum(z2, 0.0) + jnp.log(1.0 + jnp.exp2(-jnp.abs(z2))) * LOG2E


def _attn_prompt_body(bias_ref, q_ref, k_ref, v_ref, g_ref, o_ref, ko_ref, vo_ref, kb_ref, vb_ref, acc_ref, carry_ref,
                      *, tk, nsub, nhead):
    hg = pl.program_id(1)
    qi = pl.program_id(2)

    @pl.when(qi == 0)
    def _():
        k = k_ref[...]
        v = v_ref[...]
        ko_ref[...] = k
        vo_ref[...] = v
        kb_ref[...] = k.astype(BF16)
        vb_ref[...] = v.astype(BF16)

    tq = tk * nsub
    row = lax.broadcasted_iota(jnp.int32, (tk, tk), 0)
    col = lax.broadcasted_iota(jnp.int32, (tk, tk), 1)
    later = (row > col).astype(BF16)
    lanes = lambda h: slice(h * HEAD_DIM, (h + 1) * HEAD_DIM)
    q = [(q_ref[:, lanes(h)] * ATT_SCALE2).astype(BF16) for h in range(nhead)]
    bias = [bias_ref[hg * nhead + h] * LOG2E for h in range(nhead)]

    acc_ref[...] = jnp.zeros_like(acc_ref)
    carry_ref[...] = jnp.zeros_like(carry_ref)

    def block(h, lo, kb, valid):
        start = pl.multiple_of(kb * tk, tk)
        kblk = kb_ref[pl.ds(start, tk), lanes(h)]
        vblk = vb_ref[pl.ds(start, tk), lanes(h)]
        z = lax.dot_general(q[h][lo:], kblk, (((1,), (1,)), ((), ())), preferred_element_type=F32) + bias[h]
        sp = _softplus2(z)
        if valid is not None:
            sp = jnp.where(valid, sp, 0.0)
        inner = jnp.dot(sp.astype(BF16), later, preferred_element_type=F32)
        carry = carry_ref[h, lo:, :]
        w = jnp.exp2(z - sp - inner - carry)
        if valid is not None:
            w = jnp.where(valid, w, 0.0)
        acc_ref[lo:, lanes(h)] += jnp.dot(w.astype(BF16), vblk, preferred_element_type=F32)
        carry_ref[h, lo:, :] = carry + inner[:, :1] + sp[:, :1]

    for j in reversed(range(nsub)):
        m = tq - j * tk
        ri = lax.broadcasted_iota(jnp.int32, (m, tk), 0)
        ci = lax.broadcasted_iota(jnp.int32, (m, tk), 1)
        valid = (ri >= tk) | (ci < ri)
        for h in range(nhead):
            block(h, j * tk, nsub * qi + j, valid)

    @pl.loop(0, nsub * qi)
    def _(it):
        for h in range(nhead):
            block(h, 0, nsub * qi - 1 - it, None)

    o_ref[...] = (acc_ref[...] * jax.nn.sigmoid(g_ref[...])).astype(o_ref.dtype)


def _attn_prompt(proj, sb_bias, *, tk=256, nsub=4, nhead=2):
    b, t, _ = proj.shape
    tq = tk * nsub
    assert t % tq == 0 and N_HEADS % nhead == 0
    wid = nhead * HEAD_DIM
    ngrp = N_HEADS // nhead
    qspec = lambda off: pl.BlockSpec((None, tq, wid), lambda bi, h, qi: (bi, qi, off + h))
    kvspec = lambda off: pl.BlockSpec((None, t, wid), lambda bi, h, qi: (bi, 0, off + h))
    return pl.pallas_call(
        functools.partial(_attn_prompt_body, tk=tk, nsub=nsub, nhead=nhead),
        grid=(b, ngrp, t // tq),
        in_specs=[pl.BlockSpec(memory_space=pltpu.SMEM),
                  qspec(0), kvspec(ngrp), kvspec(2 * ngrp), qspec(3 * ngrp)],
        out_specs=[qspec(0), kvspec(0), kvspec(0)],
        out_shape=[jax.ShapeDtypeStruct((b, t, ATT_WIDTH), BF16),
                   jax.ShapeDtypeStruct((b, t, ATT_WIDTH), F32),
                   jax.ShapeDtypeStruct((b, t, ATT_WIDTH), F32)],
        scratch_shapes=[pltpu.VMEM((t, wid), BF16), pltpu.VMEM((t, wid), BF16),
                        pltpu.VMEM((tq, wid), F32), pltpu.VMEM((nhead, tq, 1), F32)],
        compiler_params=_cparams(("parallel", "parallel", "arbitrary")),
        name="attn_prompt",
    )(sb_bias, proj, proj, proj, proj)


SLOTS = PAGE_SIZE * HEAD_HALVES
HALF_WIDTH = SUBLANES * HEAD_DIM


def _attn_sample_body(pt_ref, q_ref, kn_ref, vn_ref, g_ref, bias_ref, *refs, pages_per_step, n_tok, n_slots, n_pages):
    ck_ref, cv_ref, o_ref, qbd_ref, kcat_ref, vcat_ref, acc_ref, carry_ref, kbuf_ref, vbuf_ref, sem_ref = refs
    c = pl.program_id(1)
    steps = pl.num_programs(1)
    step = pl.program_id(0) * steps + c
    n_steps = pl.num_programs(0) * steps

    def page_copy(s, i, pool_ref, buf_ref, which):
        page = pt_ref[s // steps, n_pages - 1 - ((s % steps) * pages_per_step + i)]
        return pltpu.make_async_copy(pool_ref.at[page], buf_ref.at[s % n_slots, i], sem_ref.at[which, s % n_slots, i])

    def group_copies(s):
        return [page_copy(s, i, pool, buf, which) for i in range(pages_per_step)
                for which, (pool, buf) in enumerate(((ck_ref, kbuf_ref), (cv_ref, vbuf_ref)))]

    @pl.when(step == 0)
    def _():
        for s in range(n_slots - 1):
            for cp in group_copies(s):
                cp.start()

    @pl.when(step + (n_slots - 1) < n_steps)
    def _():
        for cp in group_copies(step + (n_slots - 1)):
            cp.start()

    n_rows = n_tok * N_HEADS
    bias = bias_ref[...] * LOG2E
    row = lax.broadcasted_iota(jnp.int32, (SLOTS, SLOTS), 0)
    col = lax.broadcasted_iota(jnp.int32, (SLOTS, SLOTS), 1)
    later = ((row // HEAD_HALVES) > (col // HEAD_HALVES)).astype(BF16)

    def attend(npages, causal):
        n = npages * SLOTS
        z = lax.dot_general(qbd_ref[...], kcat_ref[:n, :], (((1,), (1,)), ((), ())),
                            preferred_element_type=F32) + bias
        rhalf = (lax.broadcasted_iota(jnp.int32, (n_rows, n), 0) % N_HEADS) // SUBLANES
        slot = lax.broadcasted_iota(jnp.int32, (n_rows, n), 1)
        valid = rhalf == slot % HEAD_HALVES
        if causal:
            valid = valid & (slot // HEAD_HALVES < lax.broadcasted_iota(jnp.int32, (n_rows, n), 0) // N_HEADS)
        sp = jnp.where(valid, _softplus2(z), 0.0)
        pages = lambda a: [a[:, i * SLOTS:(i + 1) * SLOTS] for i in range(npages)]
        stacked = jnp.concatenate(pages(sp), axis=0).astype(BF16)
        inner = jnp.dot(stacked, later, preferred_element_type=F32)
        inner = jnp.concatenate([inner[i * n_rows:(i + 1) * n_rows] for i in range(npages)], axis=1)
        carry = carry_ref[...]
        carries = []
        for i in range(npages):
            carries.append(jnp.broadcast_to(carry, (n_rows, SLOTS)))
            f = i * SLOTS
            carry = carry + inner[:, f:f + 1] + sp[:, f:f + 1] + sp[:, f + 1:f + 2]
        carry_ref[...] = carry
        w = jnp.where(valid, jnp.exp2(z - sp - inner - jnp.concatenate(carries, axis=1)), 0.0)
        acc_ref[...] += jnp.dot(w.astype(BF16), vcat_ref[:n, :], preferred_element_type=F32)

    @pl.when(c == 0)
    def _():
        q = q_ref[...] * ATT_SCALE2
        qt = jnp.concatenate([q] * SUBLANES, axis=1)
        rh = lax.broadcasted_iota(jnp.int32, (n_rows, HALF_WIDTH), 0) % SUBLANES
        lh = lax.broadcasted_iota(jnp.int32, (n_rows, HALF_WIDTH), 1) // HEAD_DIM
        qbd_ref[...] = jnp.where(rh == lh, qt, 0.0).astype(BF16)
        acc_ref[...] = jnp.zeros_like(acc_ref)
        carry_ref[...] = jnp.zeros_like(carry_ref)
        n_new = kn_ref.shape[0]
        kcat_ref[:n_new, :] = kn_ref[...].astype(BF16)
        vcat_ref[:n_new, :] = vn_ref[...].astype(BF16)
        kcat_ref[n_new:SLOTS, :] = jnp.zeros((SLOTS - n_new, HALF_WIDTH), BF16)
        vcat_ref[n_new:SLOTS, :] = jnp.zeros((SLOTS - n_new, HALF_WIDTH), BF16)
        attend(1, True)

    for cp in group_copies(step):
        cp.wait()
    slot = step % n_slots
    for i in range(pages_per_step):
        for h8 in range(SUBLANES):
            dst = (slice(i * SLOTS, (i + 1) * SLOTS), slice(h8 * HEAD_DIM, (h8 + 1) * HEAD_DIM))
            kcat_ref[dst] = kbuf_ref.at[slot, i][pl.ds(h8, SLOTS, stride=SUBLANES), :].astype(BF16)
            vcat_ref[dst] = vbuf_ref.at[slot, i][pl.ds(h8, SLOTS, stride=SUBLANES), :].astype(BF16)
    attend(pages_per_step, False)

    @pl.when(c == pl.num_programs(1) - 1)
    def _():
        rh = lax.broadcasted_iota(jnp.int32, (n_rows, HEAD_DIM), 0) % SUBLANES
        out = jnp.zeros((n_rows, HEAD_DIM), F32)
        for h8 in range(SUBLANES):
            out = out + jnp.where(rh == h8, acc_ref[:, h8 * HEAD_DIM:(h8 + 1) * HEAD_DIM], 0.0)
        o_ref[...] = out * jax.nn.sigmoid(g_ref[...])


def _attn_sample(proj_s, cache_k, cache_v, page_table, sb_bias, *, pages_per_step=4, n_slots=4):
    b, n_tok, _ = proj_s.shape
    n_pages = page_table.shape[1]
    n_rows = n_tok * N_HEADS
    assert n_pages % pages_per_step == 0
    steps = n_pages // pages_per_step
    q_rows = proj_s[:, :, :ATT_WIDTH].reshape(b, n_rows, HEAD_DIM)
    g_rows = proj_s[:, :, 3 * ATT_WIDTH:4 * ATT_WIDTH].reshape(b, n_rows, HEAD_DIM)
    bias_col = jnp.tile(sb_bias, n_tok).reshape(n_rows, 1)
    n_new = 2 * SUBLANES
    assert n_tok * HEAD_HALVES <= n_new

    def new_slots(a):
        a = a.reshape(b, n_tok * HEAD_HALVES, HALF_WIDTH)
        return jnp.pad(a, ((0, 0), (0, n_new - n_tok * HEAD_HALVES), (0, 0)))

    k_new = new_slots(proj_s[:, :, ATT_WIDTH:2 * ATT_WIDTH])
    v_new = new_slots(proj_s[:, :, 2 * ATT_WIDTH:3 * ATT_WIDTH])

    assert b * steps >= n_slots - 1
    rows_spec = pl.BlockSpec((None, n_rows, HEAD_DIM), lambda bi, c, pt: (bi, 0, 0))
    new_spec = pl.BlockSpec((None, n_new, HALF_WIDTH), lambda bi, c, pt: (bi, 0, 0))
    pool_spec = pl.BlockSpec(memory_space=pl.ANY)
    page_buf = pltpu.VMEM((n_slots, pages_per_step, PAGE_SIZE * N_HEADS, HEAD_DIM), F32)
    grid_spec = pltpu.PrefetchScalarGridSpec(
        num_scalar_prefetch=1,
        grid=(b, steps),
        in_specs=[rows_spec, new_spec, new_spec, rows_spec,
                  pl.BlockSpec((n_rows, 1), lambda bi, c, pt: (0, 0)), pool_spec, pool_spec],
        out_specs=rows_spec,
        scratch_shapes=[pltpu.VMEM((n_rows, HALF_WIDTH), BF16),
                        pltpu.VMEM((pages_per_step * SLOTS, HALF_WIDTH), BF16),
                        pltpu.VMEM((pages_per_step * SLOTS, HALF_WIDTH), BF16),
                        pltpu.VMEM((n_rows, HALF_WIDTH), F32),
                        pltpu.VMEM((n_rows, 1), F32),
                        page_buf, page_buf,
                        pltpu.SemaphoreType.DMA((2, n_slots, pages_per_step))],
    )
    out = pl.pallas_call(
        functools.partial(_attn_sample_body, pages_per_step=pages_per_step, n_tok=n_tok, n_slots=n_slots,
                          n_pages=n_pages),
        grid_spec=grid_spec,
        out_shape=jax.ShapeDtypeStruct((b, n_rows, HEAD_DIM), F32),
        compiler_params=_cparams(("arbitrary", "arbitrary")),
        name="attn_sample",
    )(page_table, q_rows, k_new, v_new, g_rows, bias_col, cache_k, cache_v)
    return out.reshape(b, n_tok, ATT_WIDTH)


def _ssm_params_body(are_ref, aim_ref, ldt_ref, bre_ref, bim_ref, abr_ref, abi_ref, bbr_ref, bbi_ref):
    a_re = are_ref[...]
    a_im = aim_ref[...]
    dt = jnp.exp(ldt_ref[...])
    mag = jnp.exp(dt * a_re)
    ab_re = mag * jnp.cos(dt * a_im)
    ab_im = mag * jnp.sin(dt * a_im)
    abr_ref[...] = ab_re
    abi_ref[...] = ab_im
    den = a_re * a_re + a_im * a_im
    n_re = ab_re - 1.0
    co_re = (n_re * a_re + ab_im * a_im) / den
    co_im = (ab_im * a_re - n_re * a_im) / den
    b_re = bre_ref[...]
    b_im = bim_ref[...]
    bbr_ref[...] = co_re[:, None, :] * b_re - co_im[:, None, :] * b_im
    bbi_ref[...] = co_re[:, None, :] * b_im + co_im[:, None, :] * b_re


def _ssm_params(a_re, a_im, log_dt, b_re, b_im):
    g, p = a_re.shape
    c = b_re.shape[1]
    return pl.pallas_call(
        _ssm_params_body,
        name="ssm_params",
        out_shape=[jax.ShapeDtypeStruct((g, p), F32), jax.ShapeDtypeStruct((g, p), F32),
                   jax.ShapeDtypeStruct((g, c, p), F32), jax.ShapeDtypeStruct((g, c, p), F32)],
    )(a_re, a_im, log_dt.reshape(g, 1), b_re, b_im)


def _gelu_tanh(y):
    return 0.5 * y * (1.0 + jnp.tanh(math.sqrt(2.0 / math.pi) * (y + 0.044715 * (y * y * y))))


def _ssm_prompt_body(u_ref, g_ref, b_ref, c_ref, gw_ref, ab_ref, dg_ref, o_ref, hT_ref, bu_ref, hh_ref, st_ref, tm_ref,
                     *, lc, nb, nsc):
    ci = pl.program_id(1)
    rows = lc * SUBLANES
    n_slab = STATE_LANES // LANES
    seqs = [(sel, b) for sel in range(2) for b in range(nb)]
    seq_rows = lambda i: pl.ds(i, lc, stride=SUBLANES)
    for i, (sel, b) in enumerate(seqs):
        tm_ref.at[0][seq_rows(i), :] = u_ref[b, :, sel * LANES:(sel + 1) * LANES]
        tm_ref.at[1][seq_rows(i), :] = g_ref[b, :, sel * LANES:(sel + 1) * LANES]

    @pl.when(ci == 0)
    def _():
        st_ref[...] = jnp.zeros_like(st_ref)

    ar = [ab_ref[j] for j in range(n_slab)]
    ai = [ab_ref[n_slab + j] for j in range(n_slab)]
    h = [st_ref[j] for j in range(2 * n_slab)]
    sub = rows // nsc
    first = (lax.broadcasted_iota(jnp.int32, (sub, LANES), 0) % SUBLANES) < (SUBLANES // 2)
    d = jnp.broadcast_to(dg_ref[0][None], (sub // SUBLANES, SUBLANES, LANES)).reshape(sub, LANES)
    gb = jnp.broadcast_to(dg_ref[1][None], (sub // SUBLANES, SUBLANES, LANES)).reshape(sub, LANES)
    for sc in range(nsc):
        lo = sc * sub
        u = tm_ref[0, lo:lo + sub, :]
        lhs = jnp.concatenate([jnp.where(first, u, 0.0), jnp.where(first, 0.0, u)], axis=1).astype(BF16)
        bu_ref[lo:lo + sub, :] = jnp.dot(lhs, b_ref[...], preferred_element_type=F32)
        for r0 in range(lo, lo + sub, SUBLANES):
            for j in range(n_slab):
                re_l = slice(j * LANES, (j + 1) * LANES)
                im_l = slice(STATE_LANES + j * LANES, STATE_LANES + (j + 1) * LANES)
                hr, hi = h[j], h[n_slab + j]
                h[j] = ar[j] * hr - ai[j] * hi + bu_ref[r0:r0 + SUBLANES, re_l]
                h[n_slab + j] = ar[j] * hi + ai[j] * hr + bu_ref[r0:r0 + SUBLANES, im_l]
                hh_ref[r0:r0 + SUBLANES, re_l] = h[j]
                hh_ref[r0:r0 + SUBLANES, im_l] = h[n_slab + j]
        y2 = jnp.dot(hh_ref[lo:lo + sub, :].astype(BF16), c_ref[...], preferred_element_type=F32)
        y = jnp.where(first, y2[:, :LANES], y2[:, LANES:]) + d * u
        z = _gelu_tanh(y)
        g2 = jnp.dot(z.astype(BF16), gw_ref[...], preferred_element_type=F32)
        gate = jax.nn.sigmoid(jnp.where(first, g2[:, :LANES], g2[:, LANES:]) + gb)
        tm_ref[2, lo:lo + sub, :] = z * gate * jax.nn.sigmoid(tm_ref[1, lo:lo + sub, :])
    for j in range(2 * n_slab):
        st_ref[j] = h[j]
        hT_ref[j] = h[j]
    for i, (sel, b) in enumerate(seqs):
        o_ref[b, :, sel * LANES:(sel + 1) * LANES] = tm_ref.at[2][seq_rows(i), :].astype(o_ref.dtype)


def _ssm_prompt(proj, b_st, c_st, gw_st, ab_tm, dg_tm, *, lc=256, nsc=4):
    nb, t, _ = proj.shape
    assert t % lc == 0 and (lc * SUBLANES) % (nsc * SUBLANES) == 0
    npair = N_GBLOCKS // 2
    n_slab2 = 2 * STATE_LANES // LANES
    wid = 2 * LANES
    seq_spec = lambda off: pl.BlockSpec((nb, lc, wid), lambda p, c: (0, c, off + p))
    u_off = 4 * ATT_WIDTH // wid
    par_spec = lambda shp: pl.BlockSpec((None,) + shp, lambda p, c: (p,) + (0,) * len(shp))
    return pl.pallas_call(
        functools.partial(_ssm_prompt_body, lc=lc, nb=nb, nsc=nsc),
        grid=(npair, t // lc),
        in_specs=[seq_spec(u_off), seq_spec(u_off + npair),
                  par_spec((2 * LANES, 2 * STATE_LANES)), par_spec((2 * STATE_LANES, 2 * LANES)),
                  par_spec((LANES, 2 * LANES)), par_spec((n_slab2, SUBLANES, LANES)),
                  par_spec((2, SUBLANES, LANES))],
        out_specs=[seq_spec(0), par_spec((n_slab2, SUBLANES, LANES))],
        out_shape=[jax.ShapeDtypeStruct((nb, t, SSM_WIDTH), BF16),
                   jax.ShapeDtypeStruct((npair, n_slab2, SUBLANES, LANES), F32)],
        scratch_shapes=[pltpu.VMEM((lc * SUBLANES, 2 * STATE_LANES), F32),
                        pltpu.VMEM((lc * SUBLANES, 2 * STATE_LANES), F32),
                        pltpu.VMEM((n_slab2, SUBLANES, LANES), F32),
                        pltpu.VMEM((3, lc * SUBLANES, LANES), F32)],
        compiler_params=_cparams(("parallel", "arbitrary")),
        name="ssm_prompt",
    )(proj, proj, b_st, c_st, gw_st, ab_tm, dg_tm)


def _ssm_sample_body(u_ref, g_ref, hre_ref, him_ref, b_ref, c_ref, gw_ref, ab_ref, dg_ref,
                     o_ref, ore_ref, oim_ref, *, n_tok, nb):
    hp = lax.Precision.HIGHEST
    u = u_ref[...].reshape(n_tok * nb, LANES)
    bu = jnp.dot(u, b_ref[...], preferred_element_type=F32, precision=hp)
    ar = ab_ref[0:1, :]
    ai = ab_ref[1:2, :]
    hr = hre_ref[...]
    hi = him_ref[...]
    hs = []
    for t in range(n_tok):
        bre = bu[t * nb:(t + 1) * nb, :STATE_LANES]
        bim = bu[t * nb:(t + 1) * nb, STATE_LANES:]
        hr, hi = ar * hr - ai * hi + bre, ar * hi + ai * hr + bim
        hs.append(jnp.concatenate([hr, hi], axis=1))
    ore_ref[...] = hr
    oim_ref[...] = hi
    hh = jnp.concatenate(hs, axis=0)
    y = jnp.dot(hh, c_ref[...], preferred_element_type=F32, precision=hp) + dg_ref[0:1, :] * u
    z = _gelu_tanh(y)
    gate = jax.nn.sigmoid(jnp.dot(z, gw_ref[...], preferred_element_type=F32, precision=hp) + dg_ref[1:2, :])
    out = z * gate * jax.nn.sigmoid(g_ref[...].reshape(n_tok * nb, LANES))
    o_ref[...] = out.reshape(n_tok, nb, LANES)


def _ssm_sample(u_t, g_t, h_re, h_im, b_blk, c_blk, gw_blk, ab_row, dg_row):
    n_tok, nb, _ = u_t.shape
    seq_spec = pl.BlockSpec((n_tok, nb, LANES), lambda gb: (0, 0, gb))
    st_spec = pl.BlockSpec((nb, STATE_LANES), lambda gb: (0, gb))
    par_spec = lambda shp: pl.BlockSpec((None,) + shp, lambda gb: (gb,) + (0,) * len(shp))
    return pl.pallas_call(
        functools.partial(_ssm_sample_body, n_tok=n_tok, nb=nb),
        grid=(N_GBLOCKS,),
        in_specs=[seq_spec, seq_spec, st_spec, st_spec,
                  par_spec((LANES, 2 * STATE_LANES)), par_spec((2 * STATE_LANES, LANES)),
                  par_spec((LANES, LANES)), par_spec((2, STATE_LANES)), par_spec((2, LANES))],
        out_specs=[seq_spec, st_spec, st_spec],
        out_shape=[jax.ShapeDtypeStruct(u_t.shape, F32),
                   jax.ShapeDtypeStruct(h_re.shape, F32), jax.ShapeDtypeStruct(h_im.shape, F32)],
        compiler_params=_cparams(("parallel",)),
        name="ssm_sample",
    )(u_t, g_t, h_re, h_im, b_blk, c_blk, gw_blk, ab_row, dg_row)


def _ssm_block_weights(ab_re, ab_im, bb_re, bb_im, c_re, c_im, d, glu_w, glu_b):
    nb, gpb = N_GBLOCKS, GROUPS_PER_BLOCK
    eye = jnp.eye(gpb, dtype=F32)
    bb = jnp.stack([bb_re, bb_im], axis=2).reshape(nb, gpb, SSM_GROUP, 2, STATE_DIM)
    b_blk = jnp.einsum('bgcrp,gh->bgcrhp', bb, eye).reshape(nb, LANES, 2 * STATE_LANES)
    cc = jnp.stack([c_re, -c_im], axis=2).reshape(nb, gpb, SSM_GROUP, 2, STATE_DIM)
    c_blk = jnp.einsum('bgcrp,gh->brgphc', cc, eye).reshape(nb, 2 * STATE_LANES, LANES)
    gw = glu_w.reshape(nb, gpb, SSM_GROUP, SSM_GROUP)
    gw_blk = jnp.einsum('bgce,gh->bgche', gw, eye).reshape(nb, LANES, LANES)
    ab_row = jnp.stack([ab_re.reshape(nb, STATE_LANES), ab_im.reshape(nb, STATE_LANES)], axis=1)
    dg_row = jnp.stack([d.reshape(nb, LANES), glu_b.reshape(nb, LANES)], axis=1)
    return b_blk, c_blk, gw_blk, ab_row, dg_row


def _pair_weights(b_blk, c_blk, gw_blk, ab_row, dg_row, nbatch):
    npair = N_GBLOCKS // 2
    n_slab = STATE_LANES // LANES
    b_st = b_blk.reshape(npair, 2 * LANES, 2 * STATE_LANES).astype(BF16)
    c_st = c_blk.reshape(npair, 2, 2 * STATE_LANES, LANES).transpose(0, 2, 1, 3).reshape(
        npair, 2 * STATE_LANES, 2 * LANES).astype(BF16)
    gw_st = gw_blk.reshape(npair, 2, LANES, LANES).transpose(0, 2, 1, 3).reshape(
        npair, LANES, 2 * LANES).astype(BF16)
    ab = ab_row.reshape(npair, 2, 2, n_slab, LANES).transpose(0, 2, 3, 1, 4)
    ab_tm = jnp.repeat(ab, nbatch, axis=3).reshape(npair, 2 * n_slab, 2 * nbatch, LANES)
    dg = dg_row.reshape(npair, 2, 2, LANES).transpose(0, 2, 1, 3)
    dg_tm = jnp.repeat(dg, nbatch, axis=2)
    return b_st, c_st, gw_st, ab_tm, dg_tm


def _dense_tail_prompt(x2d, ssm, att, w_out, norm_mlp_w, w_up, w_down, norm_final_w):
    x1 = _out_proj(ssm, att, w_out, x2d)
    hid = _matmul(x1, w_up, name="mlp_up", bm=512, bn=1024, bk=D_MODEL, norm_w=norm_mlp_w, epilogue="relu2",
                  out_dtype=BF16)
    x2 = _matmul(hid, w_down, name="mlp_down", bm=1024, bn=1024, bk=4096, res=x1, epilogue="res")
    return _rmsnorm(x2, norm_final_w)


def _dense_tail_sample(x2d, mixed, w_out, norm_mlp_w, w_up, w_down, norm_final_w):
    x1, w_out_b = _matmul(mixed, w_out, name="out_proj_s", bm=LANES, bn=512, bk=D_MODEL, res=x2d, epilogue="res",
                          cast_w=True)
    hid, w_up_b = _matmul(x1, w_up, name="mlp_up_s", bm=LANES, bn=512, bk=D_MODEL, norm_w=norm_mlp_w,
                          epilogue="relu2", out_dtype=BF16, cast_w=True)
    x2, w_down_b = _matmul(hid, w_down, name="mlp_down_s", bm=LANES, bn=1024, bk=2048, res=x1, epilogue="res",
                           cast_w=True)
    return _rmsnorm(x2, norm_final_w), w_out_b, w_up_b, w_down_b


def kernel(x_prompt, x_sample, cache_k, cache_v, state_ssm_re, state_ssm_im, page_table, norm_mix_w, w_in, sb_bias,
           ssm_a_re, ssm_a_im, ssm_b_re, ssm_b_im, ssm_c_re, ssm_c_im, ssm_d, ssm_log_dt, glu_w, glu_b, w_out,
           norm_mlp_w, w_up, w_down, norm_final_w):
    depth = w_in.shape[0]
    assert depth == 1
    nb_p, t_p, _ = x_prompt.shape
    nb_s, t_s, _ = x_sample.shape
    assert nb_p * 2 == SUBLANES
    assert nb_s * t_s == LANES
    l = 0

    ab_re, ab_im, bb_re, bb_im = _ssm_params(ssm_a_re[l], ssm_a_im[l], ssm_log_dt[l],
                                             ssm_b_re[l].transpose(0, 2, 1), ssm_b_im[l].transpose(0, 2, 1))
    blk = _ssm_block_weights(ab_re, ab_im, bb_re, bb_im, ssm_c_re[l], ssm_c_im[l], ssm_d[l], glu_w[l], glu_b[l])
    pair = _pair_weights(*blk, nbatch=nb_p)

    xs = x_sample.reshape(nb_s * t_s, D_MODEL)
    proj_s, w_in_b = _matmul(xs, w_in[l], name="in_proj_s", bm=LANES, bn=512, bk=D_MODEL, norm_w=norm_mix_w[l],
                             cast_w=True)
    proj_s3 = proj_s.reshape(nb_s, t_s, IN_WIDTH)
    n_phys = cache_k.shape[1]
    pool = (depth * n_phys, PAGE_SIZE * N_HEADS, HEAD_DIM)
    att_s = _attn_sample(proj_s3, cache_k.reshape(pool), cache_v.reshape(pool), page_table + l * n_phys,
                         sb_bias[l])
    u_t = proj_s3[:, :, 4 * ATT_WIDTH:4 * ATT_WIDTH + SSM_WIDTH].transpose(1, 0, 2)
    g_t = proj_s3[:, :, 4 * ATT_WIDTH + SSM_WIDTH:].transpose(1, 0, 2)
    ssm_t, hs_re, hs_im = _ssm_sample(u_t, g_t, state_ssm_re[l].reshape(nb_s, N_GROUPS * STATE_DIM),
                                      state_ssm_im[l].reshape(nb_s, N_GROUPS * STATE_DIM), *blk)
    ssm_s = ssm_t.transpose(1, 0, 2).reshape(nb_s * t_s, SSM_WIDTH)
    mixed_s = jnp.concatenate([ssm_s, att_s.reshape(nb_s * t_s, ATT_WIDTH)], axis=1).astype(BF16)
    y_s, w_out_b, w_up_b, w_down_b = _dense_tail_sample(xs, mixed_s, w_out[l], norm_mlp_w[l], w_up[l], w_down[l],
                                                        norm_final_w)

    xp = x_prompt.reshape(nb_p * t_p, D_MODEL)
    proj_p = _matmul(xp, w_in_b, name="in_proj", bm=512, bn=1024, bk=D_MODEL, norm_w=norm_mix_w[l])
    proj_p3 = proj_p.reshape(nb_p, t_p, IN_WIDTH)
    att_p, k_p, v_p = _attn_prompt(proj_p3, sb_bias[l])
    npair = N_GBLOCKS // 2
    ssm_p, hT = _ssm_prompt(proj_p3, *pair)
    n_slab = STATE_LANES // LANES
    hT = hT.reshape(npair, 2, n_slab, 2, nb_p, 2, STATE_DIM).transpose(1, 4, 0, 3, 2, 5, 6).reshape(
        2, nb_p, N_GROUPS, STATE_DIM)
    y_p = _dense_tail_prompt(xp, ssm_p.reshape(nb_p * t_p, SSM_WIDTH), att_p.reshape(nb_p * t_p, ATT_WIDTH),
                             w_out_b, norm_mlp_w[l], w_up_b, w_down_b, norm_final_w)

    kv_shape_p = (1, nb_p, t_p, N_HEADS, HEAD_DIM)
    kv_shape_s = (1, nb_s, t_s, N_HEADS, HEAD_DIM)
    return (y_p.reshape(nb_p, t_p, D_MODEL), y_s.reshape(nb_s, t_s, D_MODEL),
            k_p.reshape(kv_shape_p), v_p.reshape(kv_shape_p),
            hT[0][None], hT[1][None],
            proj_s3[:, :, ATT_WIDTH:2 * ATT_WIDTH].reshape(kv_shape_s),
            proj_s3[:, :, 2 * ATT_WIDTH:3 * ATT_WIDTH].reshape(kv_shape_s),
            hs_re.reshape(1, nb_s, N_GROUPS, STATE_DIM), hs_im.reshape(1, nb_s, N_GROUPS, STATE_DIM))
```

```python
import functools
import math

import jax
import jax.numpy as jnp
from jax import lax
from jax.experimental import pallas as pl
from jax.experimental.pallas import tpu as pltpu

F32 = jnp.float32
BF16 = jnp.bfloat16

D_MODEL = 4096
HEAD_DIM = 128
N_HEADS = 16
ATT_WIDTH = N_HEADS * HEAD_DIM
SSM_WIDTH = 2048
SSM_GROUP = 16
N_GROUPS = 128
STATE_DIM = 64
IN_WIDTH = 4 * ATT_WIDTH + 2 * SSM_WIDTH
D_FF = 4 * D_MODEL
PAGE_SIZE = 128
EPS = 1e-6

LANES = 128
SUBLANES = 8
HEAD_HALVES = N_HEADS // SUBLANES
GROUPS_PER_BLOCK = LANES // SSM_GROUP
N_GBLOCKS = N_GROUPS // GROUPS_PER_BLOCK
STATE_LANES = GROUPS_PER_BLOCK * STATE_DIM
VMEM_LIMIT = 62 * 1024 * 1024


def _cparams(sem):
    return pltpu.CompilerParams(dimension_semantics=sem, vmem_limit_bytes=VMEM_LIMIT)


def _rms_rows(x, w):
    return x * lax.rsqrt(jnp.mean(x * x, axis=-1, keepdims=True) + EPS) * w


def _mm_body(*refs, norm, epilogue, nk, cast_w):
    it = iter(refs)
    x_ref = next(it)
    nw_ref = next(it) if norm else None
    w_ref = next(it)
    res_ref = next(it) if epilogue == "res" else None
    o_ref = next(it)
    wb_ref = next(it) if cast_w else None
    xs_ref = next(it) if norm else None
    acc_ref = next(it) if nk > 1 else None
    j = pl.program_id(1)
    k = pl.program_id(2)

    if norm:
        @pl.when(j == 0)
        def _():
            xs_ref[...] = _rms_rows(x_ref[...], nw_ref[...]).astype(BF16)
        lhs = xs_ref[...]
    else:
        lhs = x_ref[...]
    w = w_ref[...]
    if cast_w:
        w = w.astype(BF16)
        wb_ref[...] = w
    part = jnp.dot(lhs, w, preferred_element_type=F32)

    def finish(acc):
        if epilogue == "relu2":
            r = jnp.maximum(acc, 0.0)
            acc = r * r
        elif epilogue == "res":
            acc = acc + res_ref[...]
        o_ref[...] = acc.astype(o_ref.dtype)

    if nk == 1:
        finish(part)
    else:
        @pl.when(k == 0)
        def _():
            acc_ref[...] = part

        @pl.when(k > 0)
        def _():
            acc_ref[...] += part

        @pl.when(k == nk - 1)
        def _():
            finish(acc_ref[...])


def _matmul(x, w, *, name, bm, bn, bk, norm_w=None, res=None, epilogue="none", out_dtype=F32, cast_w=False):
    m, kdim = x.shape
    n = w.shape[1]
    bm = min(bm, m)
    nk = kdim // bk
    norm = norm_w is not None
    assert not (norm and nk != 1)
    assert not (cast_w and m != bm)
    in_specs = [pl.BlockSpec((bm, bk), lambda i, j, k: (i, k))]
    args = [x]
    if norm:
        in_specs.append(pl.BlockSpec((1, kdim), lambda i, j, k: (0, 0)))
        args.append(norm_w.reshape(1, kdim))
    in_specs.append(pl.BlockSpec((bk, bn), lambda i, j, k: (k, j)))
    args.append(w)
    if epilogue == "res":
        in_specs.append(pl.BlockSpec((bm, bn), lambda i, j, k: (i, j)))
        args.append(res)
    scratch = []
    if norm:
        scratch.append(pltpu.VMEM((bm, kdim), BF16))
    if nk > 1:
        scratch.append(pltpu.VMEM((bm, bn), F32))
    out_specs = pl.BlockSpec((bm, bn), lambda i, j, k: (i, j))
    out_shape = jax.ShapeDtypeStruct((m, n), out_dtype)
    if cast_w:
        out_specs = [out_specs, pl.BlockSpec((bk, bn), lambda i, j, k: (k, j))]
        out_shape = [out_shape, jax.ShapeDtypeStruct((kdim, n), BF16)]
    return pl.pallas_call(
        functools.partial(_mm_body, norm=norm, epilogue=epilogue, nk=nk, cast_w=cast_w),
        grid=(m // bm, n // bn, nk),
        in_specs=in_specs,
        out_specs=out_specs,
        out_shape=out_shape,
        scratch_shapes=scratch,
        compiler_params=_cparams(("parallel", "arbitrary", "arbitrary")),
        name=name,
    )(*args)


def _out_proj_body(a_ref, b_ref, wa_ref, wb_ref, res_ref, o_ref):
    acc = jnp.dot(a_ref[...], wa_ref[...], preferred_element_type=F32)
    acc = acc + jnp.dot(b_ref[...], wb_ref[...], preferred_element_type=F32)
    o_ref[...] = acc + res_ref[...]


def _out_proj(a, b, w, res, *, bm=1024, bn=1024):
    m, ka = a.shape
    kb = b.shape[1]
    n = w.shape[1]
    bm = min(bm, m)
    assert ka == kb
    row_spec = lambda wid: pl.BlockSpec((bm, wid), lambda i, j: (i, 0))
    return pl.pallas_call(
        _out_proj_body,
        grid=(m // bm, n // bn),
        in_specs=[row_spec(ka), row_spec(kb),
                  pl.BlockSpec((ka, bn), lambda i, j: (0, j)), pl.BlockSpec((kb, bn), lambda i, j: (1, j)),
                  pl.BlockSpec((bm, bn), lambda i, j: (i, j))],
        out_specs=pl.BlockSpec((bm, bn), lambda i, j: (i, j)),
        out_shape=jax.ShapeDtypeStruct((m, n), F32),
        compiler_params=_cparams(("parallel", "arbitrary")),
        name="out_proj",
    )(a, b, w, w, res)


def _rmsnorm_body(x_ref, w_ref, o_ref):
    o_ref[...] = _rms_rows(x_ref[...], w_ref[...])


def _rmsnorm(x, w, *, bm=256):
    m, d = x.shape
    bm = min(bm, m)
    return pl.pallas_call(
        _rmsnorm_body,
        grid=(m // bm,),
        in_specs=[pl.BlockSpec((bm, d), lambda i: (i, 0)), pl.BlockSpec((1, d), lambda i: (0, 0))],
        out_specs=pl.BlockSpec((bm, d), lambda i: (i, 0)),
        out_shape=jax.ShapeDtypeStruct((m, d), F32),
        compiler_params=_cparams(("parallel",)),
        name="rmsnorm",
    )(x, w.reshape(1, d))


LOG2E = 1.0 / math.log(2.0)
ATT_SCALE2 = HEAD_DIM ** -0.5 * LOG2E


def _softplus2(z2):
    return jnp.maximum(z2, 0.0) + jnp.log(1.0 + jnp.exp2(-jnp.abs(z2))) * LOG2E


def _attn_prompt_body(bias_ref, q_ref, k_ref, v_ref, g_ref, o_ref, ko_ref, vo_ref, kb_ref, vb_ref, *state_refs,
                      tk, nsub, nhead):
    acc_refs, carry_refs = state_refs[:nhead], state_refs[nhead:]
    hg = pl.program_id(1)
    qi = pl.program_id(2)

    @pl.when(qi == 0)
    def _():
        k = k_ref[...]
        v = v_ref[...]
        ko_ref[...] = k
        vo_ref[...] = v
        kb_ref[...] = k.astype(BF16)
        vb_ref[...] = v.astype(BF16)

    tq = tk * nsub
    row = lax.broadcasted_iota(jnp.int32, (tk, tk), 0)
    col = lax.broadcasted_iota(jnp.int32, (tk, tk), 1)
    later = (row > col).astype(BF16)
    lanes = lambda h: slice(h * HEAD_DIM, (h + 1) * HEAD_DIM)
    q = [(q_ref[:, lanes(h)] * ATT_SCALE2).astype(BF16) for h in range(nhead)]
    bias = [bias_ref[hg * nhead + h] * LOG2E for h in range(nhead)]

    for ref in state_refs:
        ref[...] = jnp.zeros_like(ref)

    def block(h, lo, kb, valid):
        start = pl.multiple_of(kb * tk, tk)
        kblk = kb_ref[pl.ds(start, tk), lanes(h)]
        vblk = vb_ref[pl.ds(start, tk), lanes(h)]
        z = lax.dot_general(q[h][lo:], kblk, (((1,), (1,)), ((), ())), preferred_element_type=F32) + bias[h]
        sp = _softplus2(z)
        if valid is not None:
            sp = jnp.where(valid, sp, 0.0)
        inner = jnp.dot(sp.astype(BF16), later, preferred_element_type=F32)
        carry = carry_refs[h][lo:, :]
        w = jnp.exp2(z - sp - inner - carry)
        if valid is not None:
            w = jnp.where(valid, w, 0.0)
        acc_refs[h][lo:, :] += jnp.dot(w.astype(BF16), vblk, preferred_element_type=F32)
        carry_refs[h][lo:, :] = carry + inner[:, :1] + sp[:, :1]

    for j in reversed(range(nsub)):
        m = tq - j * tk
        ri = lax.broadcasted_iota(jnp.int32, (m, tk), 0)
        ci = lax.broadcasted_iota(jnp.int32, (m, tk), 1)
        valid = (ri >= tk) | (ci < ri)
        for h in range(nhead):
            block(h, j * tk, nsub * qi + j, valid)

    @pl.loop(0, nsub * qi)
    def _(it):
        for h in range(nhead):
            block(h, 0, nsub * qi - 1 - it, None)

    for h in range(nhead):
        o_ref[:, lanes(h)] = (acc_refs[h][...] * jax.nn.sigmoid(g_ref[:, lanes(h)])).astype(o_ref.dtype)


def _attn_prompt(proj, sb_bias, *, tk=256, nsub=4, nhead=2):
    b, t, _ = proj.shape
    tq = tk * nsub
    assert t % tq == 0 and N_HEADS % nhead == 0
    wid = nhead * HEAD_DIM
    ngrp = N_HEADS // nhead
    qspec = lambda off: pl.BlockSpec((None, tq, wid), lambda bi, h, qi: (bi, qi, off + h))
    kvspec = lambda off: pl.BlockSpec((None, t, wid), lambda bi, h, qi: (bi, 0, off + h))
    return pl.pallas_call(
        functools.partial(_attn_prompt_body, tk=tk, nsub=nsub, nhead=nhead),
        grid=(b, ngrp, t // tq),
        in_specs=[pl.BlockSpec(memory_space=pltpu.SMEM),
                  qspec(0), kvspec(ngrp), kvspec(2 * ngrp), qspec(3 * ngrp)],
        out_specs=[qspec(0), kvspec(0), kvspec(0)],
        out_shape=[jax.ShapeDtypeStruct((b, t, ATT_WIDTH), BF16),
                   jax.ShapeDtypeStruct((b, t, ATT_WIDTH), F32),
                   jax.ShapeDtypeStruct((b, t, ATT_WIDTH), F32)],
        scratch_shapes=[pltpu.VMEM((t, wid), BF16), pltpu.VMEM((t, wid), BF16),
                        *[pltpu.VMEM((tq, HEAD_DIM), F32)] * nhead, *[pltpu.VMEM((tq, 1), F32)] * nhead],
        compiler_params=_cparams(("parallel", "parallel", "arbitrary")),
        name="attn_prompt",
    )(sb_bias, proj, proj, proj, proj)


SLOTS = PAGE_SIZE * HEAD_HALVES
HALF_WIDTH = SUBLANES * HEAD_DIM


def _attn_sample_body(pt_ref, q_ref, kn_ref, vn_ref, g_ref, bias_ref, *refs, pages_per_step, n_tok):
    kp_refs = refs[:pages_per_step]
    vp_refs = refs[pages_per_step:2 * pages_per_step]
    o_ref = refs[2 * pages_per_step]
    qbd_ref, kcat_ref, vcat_ref, acc_ref, carry_ref = refs[2 * pages_per_step + 1:]
    c = pl.program_id(1)
    n_rows = n_tok * N_HEADS
    bias = bias_ref[...] * LOG2E
    row = lax.broadcasted_iota(jnp.int32, (SLOTS, SLOTS), 0)
    col = lax.broadcasted_iota(jnp.int32, (SLOTS, SLOTS), 1)
    later = ((row // HEAD_HALVES) > (col // HEAD_HALVES)).astype(BF16)

    def attend(npages, causal):
        n = npages * SLOTS
        z = lax.dot_general(qbd_ref[...], kcat_ref[:n, :], (((1,), (1,)), ((), ())),
                            preferred_element_type=F32) + bias
        rhalf = (lax.broadcasted_iota(jnp.int32, (n_rows, n), 0) % N_HEADS) // SUBLANES
        slot = lax.broadcasted_iota(jnp.int32, (n_rows, n), 1)
        valid = rhalf == slot % HEAD_HALVES
        if causal:
            valid = valid & (slot // HEAD_HALVES < lax.broadcasted_iota(jnp.int32, (n_rows, n), 0) // N_HEADS)
        sp = jnp.where(valid, _softplus2(z), 0.0)
        pages = lambda a: [a[:, i * SLOTS:(i + 1) * SLOTS] for i in range(npages)]
        stacked = jnp.concatenate(pages(sp), axis=0).astype(BF16)
        inner = jnp.dot(stacked, later, preferred_element_type=F32)
        inner = jnp.concatenate([inner[i * n_rows:(i + 1) * n_rows] for i in range(npages)], axis=1)
        carry = carry_ref[...]
        carries = []
        for i in range(npages):
            carries.append(jnp.broadcast_to(carry, (n_rows, SLOTS)))
            f = i * SLOTS
            carry = carry + inner[:, f:f + 1] + sp[:, f:f + 1] + sp[:, f + 1:f + 2]
        carry_ref[...] = carry
        w = jnp.where(valid, jnp.exp2(z - sp - inner - jnp.concatenate(carries, axis=1)), 0.0)
        acc_ref[...] += jnp.dot(w.astype(BF16), vcat_ref[:n, :], preferred_element_type=F32)

    @pl.when(c == 0)
    def _():
        q = q_ref[...] * ATT_SCALE2
        qt = jnp.concatenate([q] * SUBLANES, axis=1)
        rh = lax.broadcasted_iota(jnp.int32, (n_rows, HALF_WIDTH), 0) % SUBLANES
        lh = lax.broadcasted_iota(jnp.int32, (n_rows, HALF_WIDTH), 1) // HEAD_DIM
        qbd_ref[...] = jnp.where(rh == lh, qt, 0.0).astype(BF16)
        acc_ref[...] = jnp.zeros_like(acc_ref)
        carry_ref[...] = jnp.zeros_like(carry_ref)
        n_new = kn_ref.shape[0]
        kcat_ref[:n_new, :] = kn_ref[...].astype(BF16)
        vcat_ref[:n_new, :] = vn_ref[...].astype(BF16)
        kcat_ref[n_new:SLOTS, :] = jnp.zeros((SLOTS - n_new, HALF_WIDTH), BF16)
        vcat_ref[n_new:SLOTS, :] = jnp.zeros((SLOTS - n_new, HALF_WIDTH), BF16)
        attend(1, True)

    for i in range(pages_per_step):
        for h8 in range(SUBLANES):
            dst = (slice(i * SLOTS, (i + 1) * SLOTS), slice(h8 * HEAD_DIM, (h8 + 1) * HEAD_DIM))
            kcat_ref[dst] = kp_refs[i][pl.ds(h8, SLOTS, stride=SUBLANES), :].astype(BF16)
            vcat_ref[dst] = vp_refs[i][pl.ds(h8, SLOTS, stride=SUBLANES), :].astype(BF16)
    attend(pages_per_step, False)

    @pl.when(c == pl.num_programs(1) - 1)
    def _():
        rh = lax.broadcasted_iota(jnp.int32, (n_rows, HEAD_DIM), 0) % SUBLANES
        out = jnp.zeros((n_rows, HEAD_DIM), F32)
        for h8 in range(SUBLANES):
            out = out + jnp.where(rh == h8, acc_ref[:, h8 * HEAD_DIM:(h8 + 1) * HEAD_DIM], 0.0)
        o_ref[...] = out * jax.nn.sigmoid(g_ref[...])


def _attn_sample(proj_s, cache_k, cache_v, page_table, sb_bias, *, pages_per_step=8):
    b, n_tok, _ = proj_s.shape
    n_pages = page_table.shape[1]
    n_rows = n_tok * N_HEADS
    assert n_pages % pages_per_step == 0
    steps = n_pages // pages_per_step
    q_rows = proj_s[:, :, :ATT_WIDTH].reshape(b, n_rows, HEAD_DIM)
    g_rows = proj_s[:, :, 3 * ATT_WIDTH:4 * ATT_WIDTH].reshape(b, n_rows, HEAD_DIM)
    bias_col = jnp.tile(sb_bias, n_tok).reshape(n_rows, 1)
    n_new = 2 * SUBLANES
    assert n_tok * HEAD_HALVES <= n_new

    def new_slots(a):
        a = a.reshape(b, n_tok * HEAD_HALVES, HALF_WIDTH)
        return jnp.pad(a, ((0, 0), (0, n_new - n_tok * HEAD_HALVES), (0, 0)))

    k_new = new_slots(proj_s[:, :, ATT_WIDTH:2 * ATT_WIDTH])
    v_new = new_slots(proj_s[:, :, 2 * ATT_WIDTH:3 * ATT_WIDTH])

    def page_spec(i):
        def imap(bi, c, pt):
            return (pt[bi, n_pages - 1 - (c * pages_per_step + i)], 0, 0)
        return pl.BlockSpec((None, PAGE_SIZE * N_HEADS, HEAD_DIM), imap)

    page_specs = [page_spec(i) for i in range(pages_per_step)]
    rows_spec = pl.BlockSpec((None, n_rows, HEAD_DIM), lambda bi, c, pt: (bi, 0, 0))
    new_spec = pl.BlockSpec((None, n_new, HALF_WIDTH), lambda bi, c, pt: (bi, 0, 0))
    grid_spec = pltpu.PrefetchScalarGridSpec(
        num_scalar_prefetch=1,
        grid=(b, steps),
        in_specs=[rows_spec, new_spec, new_spec, rows_spec,
                  pl.BlockSpec((n_rows, 1), lambda bi, c, pt: (0, 0))]
                 + page_specs * 2,
        out_specs=rows_spec,
        scratch_shapes=[pltpu.VMEM((n_rows, HALF_WIDTH), BF16),
                        pltpu.VMEM((pages_per_step * SLOTS, HALF_WIDTH), BF16),
                        pltpu.VMEM((pages_per_step * SLOTS, HALF_WIDTH), BF16),
                        pltpu.VMEM((n_rows, HALF_WIDTH), F32),
                        pltpu.VMEM((n_rows, 1), F32)],
    )
    out = pl.pallas_call(
        functools.partial(_attn_sample_body, pages_per_step=pages_per_step, n_tok=n_tok),
        grid_spec=grid_spec,
        out_shape=jax.ShapeDtypeStruct((b, n_rows, HEAD_DIM), F32),
        compiler_params=_cparams(("parallel", "arbitrary")),
        name="attn_sample",
    )(page_table, q_rows, k_new, v_new, g_rows, bias_col,
      *([cache_k] * pages_per_step), *([cache_v] * pages_per_step))
    return out.reshape(b, n_tok, ATT_WIDTH)


def _ssm_params_body(are_ref, aim_ref, ldt_ref, bre_ref, bim_ref, abr_ref, abi_ref, bbr_ref, bbi_ref):
    a_re = are_ref[...]
    a_im = aim_ref[...]
    dt = jnp.exp(ldt_ref[...])
    mag = jnp.exp(dt * a_re)
    ab_re = mag * jnp.cos(dt * a_im)
    ab_im = mag * jnp.sin(dt * a_im)
    abr_ref[...] = ab_re
    abi_ref[...] = ab_im
    den = a_re * a_re + a_im * a_im
    n_re = ab_re - 1.0
    co_re = (n_re * a_re + ab_im * a_im) / den
    co_im = (ab_im * a_re - n_re * a_im) / den
    b_re = bre_ref[...]
    b_im = bim_ref[...]
    bbr_ref[...] = co_re[:, None, :] * b_re - co_im[:, None, :] * b_im
    bbi_ref[...] = co_re[:, None, :] * b_im + co_im[:, None, :] * b_re


def _ssm_params(a_re, a_im, log_dt, b_re, b_im):
    g, p = a_re.shape
    c = b_re.shape[1]
    return pl.pallas_call(
        _ssm_params_body,
        name="ssm_params",
        out_shape=[jax.ShapeDtypeStruct((g, p), F32), jax.ShapeDtypeStruct((g, p), F32),
                   jax.ShapeDtypeStruct((g, c, p), F32), jax.ShapeDtypeStruct((g, c, p), F32)],
    )(a_re, a_im, log_dt.reshape(g, 1), b_re, b_im)


def _gelu_tanh(y):
    return 0.5 * y * (1.0 + jnp.tanh(math.sqrt(2.0 / math.pi) * (y + 0.044715 * (y * y * y))))


def _ssm_prompt_body(u_ref, g_ref, b_ref, c_ref, gw_ref, ab_ref, dg_ref, o_ref, hT_ref, st_ref, tm_ref, *sub_refs,
                     lc, nb, nsc):
    ci = pl.program_id(1)
    rows = lc * SUBLANES
    n_slab = STATE_LANES // LANES
    seqs = [(sel, b) for sel in range(2) for b in range(nb)]
    seq_rows = lambda i: pl.ds(i, lc, stride=SUBLANES)
    for i, (sel, b) in enumerate(seqs):
        tm_ref.at[0][seq_rows(i), :] = u_ref[b, :, sel * LANES:(sel + 1) * LANES]
        tm_ref.at[1][seq_rows(i), :] = g_ref[b, :, sel * LANES:(sel + 1) * LANES]

    @pl.when(ci == 0)
    def _():
        st_ref[...] = jnp.zeros_like(st_ref)

    ar = [ab_ref[j] for j in range(n_slab)]
    ai = [ab_ref[n_slab + j] for j in range(n_slab)]
    h = [st_ref[j] for j in range(2 * n_slab)]
    sub = rows // nsc
    first = (lax.broadcasted_iota(jnp.int32, (sub, LANES), 0) % SUBLANES) < (SUBLANES // 2)
    d = jnp.broadcast_to(dg_ref[0][None], (sub // SUBLANES, SUBLANES, LANES)).reshape(sub, LANES)
    gb = jnp.broadcast_to(dg_ref[1][None], (sub // SUBLANES, SUBLANES, LANES)).reshape(sub, LANES)
    bu_refs, hh_refs = sub_refs[:nsc], sub_refs[nsc:]

    def drive(sc):
        u = tm_ref[0, sc * sub:(sc + 1) * sub, :]
        lhs = jnp.concatenate([jnp.where(first, u, 0.0), jnp.where(first, 0.0, u)], axis=1).astype(BF16)
        bu_refs[sc][...] = jnp.dot(lhs, b_ref[...], preferred_element_type=F32)

    drive(0)
    for sc in range(nsc):
        lo = sc * sub
        if sc + 1 < nsc:
            drive(sc + 1)
        bu_ref, hh_ref = bu_refs[sc], hh_refs[sc]
        for r0 in range(0, sub, SUBLANES):
            for j in range(n_slab):
                re_l = slice(j * LANES, (j + 1) * LANES)
                im_l = slice(STATE_LANES + j * LANES, STATE_LANES + (j + 1) * LANES)
                hr, hi = h[j], h[n_slab + j]
                h[j] = ar[j] * hr - ai[j] * hi + bu_ref[r0:r0 + SUBLANES, re_l]
                h[n_slab + j] = ar[j] * hi + ai[j] * hr + bu_ref[r0:r0 + SUBLANES, im_l]
                hh_ref[r0:r0 + SUBLANES, re_l] = h[j]
                hh_ref[r0:r0 + SUBLANES, im_l] = h[n_slab + j]
        u = tm_ref[0, lo:lo + sub, :]
        y2 = lax.dot_general(hh_ref[...].astype(BF16), c_ref[...], (((1,), (1,)), ((), ())),
                             preferred_element_type=F32)
        y = jnp.where(first, y2[:, :LANES], y2[:, LANES:]) + d * u
        z = _gelu_tanh(y)
        g2 = jnp.dot(z.astype(BF16), gw_ref[...], preferred_element_type=F32)
        gate = jax.nn.sigmoid(jnp.where(first, g2[:, :LANES], g2[:, LANES:]) + gb)
        tm_ref[2, lo:lo + sub, :] = z * gate * jax.nn.sigmoid(tm_ref[1, lo:lo + sub, :])
    for j in range(2 * n_slab):
        st_ref[j] = h[j]
        hT_ref[j] = h[j]
    for i, (sel, b) in enumerate(seqs):
        o_ref[b, :, sel * LANES:(sel + 1) * LANES] = tm_ref.at[2][seq_rows(i), :].astype(o_ref.dtype)


def _ssm_prompt(proj, b_st, c_st, gw_st, ab_tm, dg_tm, *, lc=256, nsc=4):
    nb, t, _ = proj.shape
    assert t % lc == 0 and (lc * SUBLANES) % (nsc * SUBLANES) == 0
    npair = N_GBLOCKS // 2
    n_slab2 = 2 * STATE_LANES // LANES
    wid = 2 * LANES
    seq_spec = lambda off: pl.BlockSpec((nb, lc, wid), lambda p, c: (0, c, off + p))
    u_off = 4 * ATT_WIDTH // wid
    par_spec = lambda shp: pl.BlockSpec((None,) + shp, lambda p, c: (p,) + (0,) * len(shp))
    return pl.pallas_call(
        functools.partial(_ssm_prompt_body, lc=lc, nb=nb, nsc=nsc),
        grid=(npair, t // lc),
        in_specs=[seq_spec(u_off), seq_spec(u_off + npair),
                  par_spec((2 * LANES, 2 * STATE_LANES)), par_spec((2 * LANES, 2 * STATE_LANES)),
                  par_spec((LANES, 2 * LANES)), par_spec((n_slab2, SUBLANES, LANES)),
                  par_spec((2, SUBLANES, LANES))],
        out_specs=[seq_spec(0), par_spec((n_slab2, SUBLANES, LANES))],
        out_shape=[jax.ShapeDtypeStruct((nb, t, SSM_WIDTH), BF16),
                   jax.ShapeDtypeStruct((npair, n_slab2, SUBLANES, LANES), F32)],
        scratch_shapes=[pltpu.VMEM((n_slab2, SUBLANES, LANES), F32),
                        pltpu.VMEM((3, lc * SUBLANES, LANES), F32)]
                       + [pltpu.VMEM((lc * SUBLANES // nsc, 2 * STATE_LANES), F32)] * (2 * nsc),
        compiler_params=_cparams(("parallel", "arbitrary")),
        name="ssm_prompt",
    )(proj, proj, b_st, c_st, gw_st, ab_tm, dg_tm)


def _ssm_sample_body(u_ref, g_ref, hre_ref, him_ref, b_ref, c_ref, gw_ref, ab_ref, dg_ref,
                     o_ref, ore_ref, oim_ref, *, n_tok, nb):
    hp = lax.Precision.HIGHEST
    u = u_ref[...].reshape(n_tok * nb, LANES)
    bu = jnp.dot(u, b_ref[...], preferred_element_type=F32, precision=hp)
    ar = ab_ref[0:1, :]
    ai = ab_ref[1:2, :]
    hr = hre_ref[...]
    hi = him_ref[...]
    hs = []
    for t in range(n_tok):
        bre = bu[t * nb:(t + 1) * nb, :STATE_LANES]
        bim = bu[t * nb:(t + 1) * nb, STATE_LANES:]
        hr, hi = ar * hr - ai * hi + bre, ar * hi + ai * hr + bim
        hs.append(jnp.concatenate([hr, hi], axis=1))
    ore_ref[...] = hr
    oim_ref[...] = hi
    hh = jnp.concatenate(hs, axis=0)
    y = lax.dot_general(hh, c_ref[...], (((1,), (1,)), ((), ())), preferred_element_type=F32, precision=hp)
    y = y + dg_ref[0:1, :] * u
    z = _gelu_tanh(y)
    gate = jax.nn.sigmoid(jnp.dot(z, gw_ref[...], preferred_element_type=F32, precision=hp) + dg_ref[1:2, :])
    out = z * gate * jax.nn.sigmoid(g_ref[...].reshape(n_tok * nb, LANES))
    o_ref[...] = out.reshape(n_tok, nb, LANES)


def _ssm_sample(u_t, g_t, h_re, h_im, b_blk, c_blk, gw_blk, ab_row, dg_row):
    n_tok, nb, _ = u_t.shape
    seq_spec = pl.BlockSpec((n_tok, nb, LANES), lambda gb: (0, 0, gb))
    st_spec = pl.BlockSpec((nb, STATE_LANES), lambda gb: (0, gb))
    par_spec = lambda shp: pl.BlockSpec((None,) + shp, lambda gb: (gb,) + (0,) * len(shp))
    return pl.pallas_call(
        functools.partial(_ssm_sample_body, n_tok=n_tok, nb=nb),
        grid=(N_GBLOCKS,),
        in_specs=[seq_spec, seq_spec, st_spec, st_spec,
                  par_spec((LANES, 2 * STATE_LANES)), par_spec((LANES, 2 * STATE_LANES)),
                  par_spec((LANES, LANES)), par_spec((2, STATE_LANES)), par_spec((2, LANES))],
        out_specs=[seq_spec, st_spec, st_spec],
        out_shape=[jax.ShapeDtypeStruct(u_t.shape, F32),
                   jax.ShapeDtypeStruct(h_re.shape, F32), jax.ShapeDtypeStruct(h_im.shape, F32)],
        compiler_params=_cparams(("parallel",)),
        name="ssm_sample",
    )(u_t, g_t, h_re, h_im, b_blk, c_blk, gw_blk, ab_row, dg_row)


def _expand_body(re_ref, im_ref, o32_ref, o16_ref, *, im_sign):
    rows = re_ref.shape[0]
    src = lax.broadcasted_iota(jnp.int32, (STATE_DIM, STATE_LANES), 0)
    dst = lax.broadcasted_iota(jnp.int32, (STATE_DIM, STATE_LANES), 1)
    spread = (src == dst % STATE_DIM).astype(F32)
    own = (lax.broadcasted_iota(jnp.int32, (rows, STATE_LANES), 0) // SSM_GROUP
           == lax.broadcasted_iota(jnp.int32, (rows, STATE_LANES), 1) // STATE_DIM)
    halves = []
    for ref, sign in ((re_ref, 1.0), (im_ref, im_sign)):
        full = jnp.dot(ref[...], spread, preferred_element_type=F32, precision=lax.Precision.HIGHEST)
        halves.append(jnp.where(own, sign * full, 0.0))
    out = jnp.concatenate(halves, axis=1)
    o32_ref[...] = out
    o16_ref[...] = out.astype(BF16)


def _expand(x_re, x_im, *, im_sign):
    g, c, p = x_re.shape
    in_spec = pl.BlockSpec((LANES, p), lambda i: (i, 0))
    out_spec = pl.BlockSpec((None, LANES, 2 * STATE_LANES), lambda i: (i, 0, 0))
    shape = (N_GBLOCKS, LANES, 2 * STATE_LANES)
    return pl.pallas_call(
        functools.partial(_expand_body, im_sign=im_sign),
        grid=(N_GBLOCKS,),
        in_specs=[in_spec, in_spec],
        out_specs=[out_spec, out_spec],
        out_shape=[jax.ShapeDtypeStruct(shape, F32), jax.ShapeDtypeStruct(shape, BF16)],
        compiler_params=_cparams(("parallel",)),
        name="ssm_expand",
    )(x_re.reshape(g * c, p), x_im.reshape(g * c, p))


def _ssm_block_weights(ab_re, ab_im, bb_re, bb_im, c_re, c_im, d, glu_w, glu_b):
    nb, gpb = N_GBLOCKS, GROUPS_PER_BLOCK
    eye = jnp.eye(gpb, dtype=F32)
    b_blk, b_blk16 = _expand(bb_re, bb_im, im_sign=1.0)
    c_blk, c_blk16 = _expand(c_re, c_im, im_sign=-1.0)
    gw = glu_w.reshape(nb, gpb, SSM_GROUP, SSM_GROUP)
    gw_blk = jnp.einsum('bgce,gh->bgche', gw, eye).reshape(nb, LANES, LANES)
    ab_row = jnp.stack([ab_re.reshape(nb, STATE_LANES), ab_im.reshape(nb, STATE_LANES)], axis=1)
    dg_row = jnp.stack([d.reshape(nb, LANES), glu_b.reshape(nb, LANES)], axis=1)
    return (b_blk, c_blk, gw_blk, ab_row, dg_row), (b_blk16, c_blk16)


def _pair_weights(b_blk16, c_blk16, gw_blk, ab_row, dg_row, nbatch):
    npair = N_GBLOCKS // 2
    n_slab = STATE_LANES // LANES
    b_st = b_blk16.reshape(npair, 2 * LANES, 2 * STATE_LANES)
    c_st = c_blk16.reshape(npair, 2 * LANES, 2 * STATE_LANES)
    gw_st = gw_blk.reshape(npair, 2, LANES, LANES).transpose(0, 2, 1, 3).reshape(
        npair, LANES, 2 * LANES).astype(BF16)
    ab = ab_row.reshape(npair, 2, 2, n_slab, LANES).transpose(0, 2, 3, 1, 4)
    ab_tm = jnp.repeat(ab, nbatch, axis=3).reshape(npair, 2 * n_slab, 2 * nbatch, LANES)
    dg = dg_row.reshape(npair, 2, 2, LANES).transpose(0, 2, 1, 3)
    dg_tm = jnp.repeat(dg, nbatch, axis=2)
    return b_st, c_st, gw_st, ab_tm, dg_tm


def _dense_tail_prompt(x2d, ssm, att, w_out, norm_mlp_w, w_up, w_down, norm_final_w):
    x1 = _out_proj(ssm, att, w_out, x2d)
    hid = _matmul(x1, w_up, name="mlp_up", bm=512, bn=1024, bk=D_MODEL, norm_w=norm_mlp_w, epilogue="relu2",
                  out_dtype=BF16)
    x2 = _matmul(hid, w_down, name="mlp_down", bm=1024, bn=1024, bk=4096, res=x1, epilogue="res")
    return _rmsnorm(x2, norm_final_w)


def _dense_tail_sample(x2d, mixed, w_out, norm_mlp_w, w_up, w_down, norm_final_w):
    x1, w_out_b = _matmul(mixed, w_out, name="out_proj_s", bm=LANES, bn=512, bk=D_MODEL, res=x2d, epilogue="res",
                          cast_w=True)
    hid, w_up_b = _matmul(x1, w_up, name="mlp_up_s", bm=LANES, bn=512, bk=D_MODEL, norm_w=norm_mlp_w,
                          epilogue="relu2", out_dtype=BF16, cast_w=True)
    x2, w_down_b = _matmul(hid, w_down, name="mlp_down_s", bm=LANES, bn=1024, bk=2048, res=x1, epilogue="res",
                           cast_w=True)
    return _rmsnorm(x2, norm_final_w), w_out_b, w_up_b, w_down_b


def kernel(x_prompt, x_sample, cache_k, cache_v, state_ssm_re, state_ssm_im, page_table, norm_mix_w, w_in, sb_bias,
           ssm_a_re, ssm_a_im, ssm_b_re, ssm_b_im, ssm_c_re, ssm_c_im, ssm_d, ssm_log_dt, glu_w, glu_b, w_out,
           norm_mlp_w, w_up, w_down, norm_final_w):
    depth = w_in.shape[0]
    assert depth == 1
    nb_p, t_p, _ = x_prompt.shape
    nb_s, t_s, _ = x_sample.shape
    assert nb_p * 2 == SUBLANES
    assert nb_s * t_s == LANES
    l = 0

    ab_re, ab_im, bb_re, bb_im = _ssm_params(ssm_a_re[l], ssm_a_im[l], ssm_log_dt[l],
                                             ssm_b_re[l].transpose(0, 2, 1), ssm_b_im[l].transpose(0, 2, 1))
    blk, blk16 = _ssm_block_weights(ab_re, ab_im, bb_re, bb_im, ssm_c_re[l], ssm_c_im[l], ssm_d[l], glu_w[l],
                                    glu_b[l])
    pair = _pair_weights(*blk16, *blk[2:], nbatch=nb_p)

    xs = x_sample.reshape(nb_s * t_s, D_MODEL)
    proj_s, w_in_b = _matmul(xs, w_in[l], name="in_proj_s", bm=LANES, bn=512, bk=D_MODEL, norm_w=norm_mix_w[l],
                             cast_w=True)
    proj_s3 = proj_s.reshape(nb_s, t_s, IN_WIDTH)
    n_phys = cache_k.shape[1]
    pool = (depth * n_phys, PAGE_SIZE * N_HEADS, HEAD_DIM)
    att_s = _attn_sample(proj_s3, cache_k.reshape(pool), cache_v.reshape(pool), page_table + l * n_phys,
                         sb_bias[l])
    u_t = proj_s3[:, :, 4 * ATT_WIDTH:4 * ATT_WIDTH + SSM_WIDTH].transpose(1, 0, 2)
    g_t = proj_s3[:, :, 4 * ATT_WIDTH + SSM_WIDTH:].transpose(1, 0, 2)
    ssm_t, hs_re, hs_im = _ssm_sample(u_t, g_t, state_ssm_re[l].reshape(nb_s, N_GROUPS * STATE_DIM),
                                      state_ssm_im[l].reshape(nb_s, N_GROUPS * STATE_DIM), *blk)
    ssm_s = ssm_t.transpose(1, 0, 2).reshape(nb_s * t_s, SSM_WIDTH)
    mixed_s = jnp.concatenate([ssm_s, att_s.reshape(nb_s * t_s, ATT_WIDTH)], axis=1).astype(BF16)
    y_s, w_out_b, w_up_b, w_down_b = _dense_tail_sample(xs, mixed_s, w_out[l], norm_mlp_w[l], w_up[l], w_down[l],
                                                        norm_final_w)

    xp = x_prompt.reshape(nb_p * t_p, D_MODEL)
    proj_p = _matmul(xp, w_in_b, name="in_proj", bm=512, bn=1024, bk=D_MODEL, norm_w=norm_mix_w[l])
    proj_p3 = proj_p.reshape(nb_p, t_p, IN_WIDTH)
    att_p, k_p, v_p = _attn_prompt(proj_p3, sb_bias[l])
    npair = N_GBLOCKS // 2
    ssm_p, hT = _ssm_prompt(proj_p3, *pair)
    n_slab = STATE_LANES // LANES
    hT = hT.reshape(npair, 2, n_slab, 2, nb_p, 2, STATE_DIM).transpose(1, 4, 0, 3, 2, 5, 6).reshape(
        2, nb_p, N_GROUPS, STATE_DIM)
    y_p = _dense_tail_prompt(xp, ssm_p.reshape(nb_p * t_p, SSM_WIDTH), att_p.reshape(nb_p * t_p, ATT_WIDTH),
                             w_out_b, norm_mlp_w[l], w_up_b, w_down_b, norm_final_w)

    kv_shape_p = (1, nb_p, t_p, N_HEADS, HEAD_DIM)
    kv_shape_s = (1, nb_s, t_s, N_HEADS, HEAD_DIM)
    return (y_p.reshape(nb_p, t_p, D_MODEL), y_s.reshape(nb_s, t_s, D_MODEL),
            k_p.reshape(kv_shape_p), v_p.reshape(kv_shape_p),
            hT[0][None], hT[1][None],
            proj_s3[:, :, ATT_WIDTH:2 * ATT_WIDTH].reshape(kv_shape_s),
            proj_s3[:, :, 2 * ATT_WIDTH:3 * ATT_WIDTH].reshape(kv_shape_s),
            hs_re.reshape(1, nb_s, N_GROUPS, STATE_DIM), hs_im.reshape(1, nb_s, N_GROUPS, STATE_DIM))
```

```python
import functools
import math

import jax
import jax.numpy as jnp
from jax import lax
from jax.experimental import pallas as pl
from jax.experimental.pallas import tpu as pltpu

F32 = jnp.float32
BF16 = jnp.bfloat16

D_MODEL = 4096
HEAD_DIM = 128
N_HEADS = 16
ATT_WIDTH = N_HEADS * HEAD_DIM
SSM_WIDTH = 2048
SSM_GROUP = 16
N_GROUPS = 128
STATE_DIM = 64
IN_WIDTH = 4 * ATT_WIDTH + 2 * SSM_WIDTH
D_FF = 4 * D_MODEL
PAGE_SIZE = 128
EPS = 1e-6

LANES = 128
SUBLANES = 8
HEAD_HALVES = N_HEADS // SUBLANES
GROUPS_PER_BLOCK = LANES // SSM_GROUP
N_GBLOCKS = N_GROUPS // GROUPS_PER_BLOCK
STATE_LANES = GROUPS_PER_BLOCK * STATE_DIM
VMEM_LIMIT = 62 * 1024 * 1024


def _cparams(sem):
    return pltpu.CompilerParams(dimension_semantics=sem, vmem_limit_bytes=VMEM_LIMIT)


def _rms_rows(x, w):
    return x * lax.rsqrt(jnp.mean(x * x, axis=-1, keepdims=True) + EPS) * w


def _mm_body(*refs, norm, epilogue, nk, cast_w):
    it = iter(refs)
    x_ref = next(it)
    nw_ref = next(it) if norm else None
    w_ref = next(it)
    res_ref = next(it) if epilogue == "res" else None
    o_ref = next(it)
    wb_ref = next(it) if cast_w else None
    xs_ref = next(it) if norm else None
    acc_ref = next(it) if nk > 1 else None
    j = pl.program_id(1)
    k = pl.program_id(2)

    if norm:
        @pl.when(j == 0)
        def _():
            xs_ref[...] = _rms_rows(x_ref[...], nw_ref[...]).astype(BF16)
        lhs = xs_ref[...]
    else:
        lhs = x_ref[...]
    w = w_ref[...]
    if cast_w:
        w = w.astype(BF16)
        wb_ref[...] = w
    part = jnp.dot(lhs, w, preferred_element_type=F32)

    def finish(acc):
        if epilogue == "relu2":
            r = jnp.maximum(acc, 0.0)
            acc = r * r
        elif epilogue == "res":
            acc = acc + res_ref[...]
        o_ref[...] = acc.astype(o_ref.dtype)

    if nk == 1:
        finish(part)
    else:
        @pl.when(k == 0)
        def _():
            acc_ref[...] = part

        @pl.when(k > 0)
        def _():
            acc_ref[...] += part

        @pl.when(k == nk - 1)
        def _():
            finish(acc_ref[...])


def _matmul(x, w, *, name, bm, bn, bk, norm_w=None, res=None, epilogue="none", out_dtype=F32, cast_w=False):
    m, kdim = x.shape
    n = w.shape[1]
    bm = min(bm, m)
    nk = kdim // bk
    norm = norm_w is not None
    assert not (norm and nk != 1)
    assert not (cast_w and m != bm)
    in_specs = [pl.BlockSpec((bm, bk), lambda i, j, k: (i, k))]
    args = [x]
    if norm:
        in_specs.append(pl.BlockSpec((1, kdim), lambda i, j, k: (0, 0)))
        args.append(norm_w.reshape(1, kdim))
    in_specs.append(pl.BlockSpec((bk, bn), lambda i, j, k: (k, j)))
    args.append(w)
    if epilogue == "res":
        in_specs.append(pl.BlockSpec((bm, bn), lambda i, j, k: (i, j)))
        args.append(res)
    scratch = []
    if norm:
        scratch.append(pltpu.VMEM((bm, kdim), BF16))
    if nk > 1:
        scratch.append(pltpu.VMEM((bm, bn), F32))
    out_specs = pl.BlockSpec((bm, bn), lambda i, j, k: (i, j))
    out_shape = jax.ShapeDtypeStruct((m, n), out_dtype)
    if cast_w:
        out_specs = [out_specs, pl.BlockSpec((bk, bn), lambda i, j, k: (k, j))]
        out_shape = [out_shape, jax.ShapeDtypeStruct((kdim, n), BF16)]
    return pl.pallas_call(
        functools.partial(_mm_body, norm=norm, epilogue=epilogue, nk=nk, cast_w=cast_w),
        grid=(m // bm, n // bn, nk),
        in_specs=in_specs,
        out_specs=out_specs,
        out_shape=out_shape,
        scratch_shapes=scratch,
        compiler_params=_cparams(("parallel", "arbitrary", "arbitrary")),
        name=name,
    )(*args)


def _out_proj_body(a_ref, b_ref, wa_ref, wb_ref, res_ref, o_ref):
    acc = jnp.dot(a_ref[...], wa_ref[...], preferred_element_type=F32)
    acc = acc + jnp.dot(b_ref[...], wb_ref[...], preferred_element_type=F32)
    o_ref[...] = acc + res_ref[...]


def _out_proj(a, b, w, res, *, bm=1024, bn=1024):
    m, ka = a.shape
    kb = b.shape[1]
    n = w.shape[1]
    bm = min(bm, m)
    assert ka == kb
    row_spec = lambda wid: pl.BlockSpec((bm, wid), lambda i, j: (i, 0))
    return pl.pallas_call(
        _out_proj_body,
        grid=(m // bm, n // bn),
        in_specs=[row_spec(ka), row_spec(kb),
                  pl.BlockSpec((ka, bn), lambda i, j: (0, j)), pl.BlockSpec((kb, bn), lambda i, j: (1, j)),
                  pl.BlockSpec((bm, bn), lambda i, j: (i, j))],
        out_specs=pl.BlockSpec((bm, bn), lambda i, j: (i, j)),
        out_shape=jax.ShapeDtypeStruct((m, n), F32),
        compiler_params=_cparams(("parallel", "arbitrary")),
        name="out_proj",
    )(a, b, w, w, res)


def _rmsnorm_body(x_ref, w_ref, o_ref):
    o_ref[...] = _rms_rows(x_ref[...], w_ref[...])


def _rmsnorm(x, w, *, bm=256):
    m, d = x.shape
    bm = min(bm, m)
    return pl.pallas_call(
        _rmsnorm_body,
        grid=(m // bm,),
        in_specs=[pl.BlockSpec((bm, d), lambda i: (i, 0)), pl.BlockSpec((1, d), lambda i: (0, 0))],
        out_specs=pl.BlockSpec((bm, d), lambda i: (i, 0)),
        out_shape=jax.ShapeDtypeStruct((m, d), F32),
        compiler_params=_cparams(("parallel",)),
        name="rmsnorm",
    )(x, w.reshape(1, d))


LOG2E = 1.0 / math.log(2.0)
ATT_SCALE2 = HEAD_DIM ** -0.5 * LOG2E


def _softplus2(z2):
    return jnp.maximum(z2, 0.0) + jnp.log(1.0 + jnp.exp2(-jnp.abs(z2))) * LOG2E


def _attn_prompt_body(bias_ref, q_ref, k_ref, v_ref, g_ref, o_ref, ko_ref, vo_ref, kb_ref, vb_ref, *state_refs,
                      tk, nsub, nhead):
    acc_refs, carry_refs = state_refs[:nhead], state_refs[nhead:]
    hg = pl.program_id(1)
    qi = pl.program_id(2)

    @pl.when(qi == 0)
    def _():
        k = k_ref[...]
        v = v_ref[...]
        ko_ref[...] = k
        vo_ref[...] = v
        kb_ref[...] = k.astype(BF16)
        vb_ref[...] = v.astype(BF16)

    tq = tk * nsub
    row = lax.broadcasted_iota(jnp.int32, (tk, tk), 0)
    col = lax.broadcasted_iota(jnp.int32, (tk, tk), 1)
    later = (row > col).astype(BF16)
    lanes = lambda h: slice(h * HEAD_DIM, (h + 1) * HEAD_DIM)
    q = [(q_ref[:, lanes(h)] * ATT_SCALE2).astype(BF16) for h in range(nhead)]
    bias = [bias_ref[hg * nhead + h] * LOG2E for h in range(nhead)]

    for ref in state_refs:
        ref[...] = jnp.zeros_like(ref)

    def block(h, lo, kb, valid):
        start = pl.multiple_of(kb * tk, tk)
        kblk = kb_ref[pl.ds(start, tk), lanes(h)]
        vblk = vb_ref[pl.ds(start, tk), lanes(h)]
        z = lax.dot_general(q[h][lo:], kblk, (((1,), (1,)), ((), ())), preferred_element_type=F32) + bias[h]
        sp = _softplus2(z)
        if valid is not None:
            sp = jnp.where(valid, sp, 0.0)
        inner = jnp.dot(sp.astype(BF16), later, preferred_element_type=F32)
        carry = carry_refs[h][lo:, :]
        w = jnp.exp2(z - sp - inner - carry)
        if valid is not None:
            w = jnp.where(valid, w, 0.0)
        acc_refs[h][lo:, :] += jnp.dot(w.astype(BF16), vblk, preferred_element_type=F32)
        carry_refs[h][lo:, :] = carry + inner[:, :1] + sp[:, :1]

    for j in reversed(range(nsub)):
        m = tq - j * tk
        ri = lax.broadcasted_iota(jnp.int32, (m, tk), 0)
        ci = lax.broadcasted_iota(jnp.int32, (m, tk), 1)
        valid = (ri >= tk) | (ci < ri)
        for h in range(nhead):
            block(h, j * tk, nsub * qi + j, valid)

    @pl.loop(0, qi)
    def _(it):
        for j in range(nsub):
            for h in range(nhead):
                block(h, 0, nsub * (qi - it) - 1 - j, None)

    for h in range(nhead):
        o_ref[:, lanes(h)] = (acc_refs[h][...] * jax.nn.sigmoid(g_ref[:, lanes(h)])).astype(o_ref.dtype)


def _attn_prompt(proj, sb_bias, *, tk=256, nsub=4, nhead=2):
    b, t, _ = proj.shape
    tq = tk * nsub
    assert t % tq == 0 and N_HEADS % nhead == 0
    wid = nhead * HEAD_DIM
    ngrp = N_HEADS // nhead
    qspec = lambda off: pl.BlockSpec((None, tq, wid), lambda bi, h, qi: (bi, qi, off + h))
    kvspec = lambda off: pl.BlockSpec((None, t, wid), lambda bi, h, qi: (bi, 0, off + h))
    return pl.pallas_call(
        functools.partial(_attn_prompt_body, tk=tk, nsub=nsub, nhead=nhead),
        grid=(b, ngrp, t // tq),
        in_specs=[pl.BlockSpec(memory_space=pltpu.SMEM),
                  qspec(0), kvspec(ngrp), kvspec(2 * ngrp), qspec(3 * ngrp)],
        out_specs=[qspec(0), kvspec(0), kvspec(0)],
        out_shape=[jax.ShapeDtypeStruct((b, t, ATT_WIDTH), BF16),
                   jax.ShapeDtypeStruct((b, t, ATT_WIDTH), F32),
                   jax.ShapeDtypeStruct((b, t, ATT_WIDTH), F32)],
        scratch_shapes=[pltpu.VMEM((t, wid), BF16), pltpu.VMEM((t, wid), BF16),
                        *[pltpu.VMEM((tq, HEAD_DIM), F32)] * nhead, *[pltpu.VMEM((tq, 1), F32)] * nhead],
        compiler_params=_cparams(("parallel", "parallel", "arbitrary")),
        name="attn_prompt",
    )(sb_bias, proj, proj, proj, proj)


SLOTS = PAGE_SIZE * HEAD_HALVES
HALF_WIDTH = SUBLANES * HEAD_DIM


def _attn_sample_body(pt_ref, q_ref, kn_ref, vn_ref, g_ref, bias_ref, *refs, pages_per_step, n_tok):
    kp_refs = refs[:pages_per_step]
    vp_refs = refs[pages_per_step:2 * pages_per_step]
    o_ref = refs[2 * pages_per_step]
    qbd_ref, kcat_ref, vcat_ref, acc_ref, carry_ref = refs[2 * pages_per_step + 1:]
    c = pl.program_id(1)
    n_rows = n_tok * N_HEADS
    bias = bias_ref[...] * LOG2E
    row = lax.broadcasted_iota(jnp.int32, (SLOTS, SLOTS), 0)
    col = lax.broadcasted_iota(jnp.int32, (SLOTS, SLOTS), 1)
    later = ((row // HEAD_HALVES) > (col // HEAD_HALVES)).astype(BF16)

    def attend(npages, causal):
        n = npages * SLOTS
        z = lax.dot_general(qbd_ref[...], kcat_ref[:n, :], (((1,), (1,)), ((), ())),
                            preferred_element_type=F32) + bias
        rhalf = (lax.broadcasted_iota(jnp.int32, (n_rows, n), 0) % N_HEADS) // SUBLANES
        slot = lax.broadcasted_iota(jnp.int32, (n_rows, n), 1)
        valid = rhalf == slot % HEAD_HALVES
        if causal:
            valid = valid & (slot // HEAD_HALVES < lax.broadcasted_iota(jnp.int32, (n_rows, n), 0) // N_HEADS)
        sp = jnp.where(valid, _softplus2(z), 0.0)
        pages = lambda a: [a[:, i * SLOTS:(i + 1) * SLOTS] for i in range(npages)]
        stacked = jnp.concatenate(pages(sp), axis=0).astype(BF16)
        inner = jnp.dot(stacked, later, preferred_element_type=F32)
        inner = jnp.concatenate([inner[i * n_rows:(i + 1) * n_rows] for i in range(npages)], axis=1)
        carry = carry_ref[...]
        carries = []
        for i in range(npages):
            carries.append(jnp.broadcast_to(carry, (n_rows, SLOTS)))
            f = i * SLOTS
            carry = carry + inner[:, f:f + 1] + sp[:, f:f + 1] + sp[:, f + 1:f + 2]
        carry_ref[...] = carry
        w = jnp.where(valid, jnp.exp2(z - sp - inner - jnp.concatenate(carries, axis=1)), 0.0)
        acc_ref[...] += jnp.dot(w.astype(BF16), vcat_ref[:n, :], preferred_element_type=F32)

    @pl.when(c == 0)
    def _():
        q = q_ref[...] * ATT_SCALE2
        qt = jnp.concatenate([q] * SUBLANES, axis=1)
        rh = lax.broadcasted_iota(jnp.int32, (n_rows, HALF_WIDTH), 0) % SUBLANES
        lh = lax.broadcasted_iota(jnp.int32, (n_rows, HALF_WIDTH), 1) // HEAD_DIM
        qbd_ref[...] = jnp.where(rh == lh, qt, 0.0).astype(BF16)
        acc_ref[...] = jnp.zeros_like(acc_ref)
        carry_ref[...] = jnp.zeros_like(carry_ref)
        n_new = kn_ref.shape[0]
        kcat_ref[:n_new, :] = kn_ref[...].astype(BF16)
        vcat_ref[:n_new, :] = vn_ref[...].astype(BF16)
        kcat_ref[n_new:SLOTS, :] = jnp.zeros((SLOTS - n_new, HALF_WIDTH), BF16)
        vcat_ref[n_new:SLOTS, :] = jnp.zeros((SLOTS - n_new, HALF_WIDTH), BF16)
        attend(1, True)

    for i in range(pages_per_step):
        for h8 in range(SUBLANES):
            dst = (slice(i * SLOTS, (i + 1) * SLOTS), slice(h8 * HEAD_DIM, (h8 + 1) * HEAD_DIM))
            kcat_ref[dst] = kp_refs[i][pl.ds(h8, SLOTS, stride=SUBLANES), :].astype(BF16)
            vcat_ref[dst] = vp_refs[i][pl.ds(h8, SLOTS, stride=SUBLANES), :].astype(BF16)
    attend(pages_per_step, False)

    @pl.when(c == pl.num_programs(1) - 1)
    def _():
        rh = lax.broadcasted_iota(jnp.int32, (n_rows, HEAD_DIM), 0) % SUBLANES
        out = jnp.zeros((n_rows, HEAD_DIM), F32)
        for h8 in range(SUBLANES):
            out = out + jnp.where(rh == h8, acc_ref[:, h8 * HEAD_DIM:(h8 + 1) * HEAD_DIM], 0.0)
        o_ref[...] = out * jax.nn.sigmoid(g_ref[...])


def _attn_sample(proj_s, cache_k, cache_v, page_table, sb_bias, *, pages_per_step=8):
    b, n_tok, _ = proj_s.shape
    n_pages = page_table.shape[1]
    n_rows = n_tok * N_HEADS
    assert n_pages % pages_per_step == 0
    steps = n_pages // pages_per_step
    q_rows = proj_s[:, :, :ATT_WIDTH].reshape(b, n_rows, HEAD_DIM)
    g_rows = proj_s[:, :, 3 * ATT_WIDTH:4 * ATT_WIDTH].reshape(b, n_rows, HEAD_DIM)
    bias_col = jnp.tile(sb_bias, n_tok).reshape(n_rows, 1)
    n_new = 2 * SUBLANES
    assert n_tok * HEAD_HALVES <= n_new

    def new_slots(a):
        a = a.reshape(b, n_tok * HEAD_HALVES, HALF_WIDTH)
        return jnp.pad(a, ((0, 0), (0, n_new - n_tok * HEAD_HALVES), (0, 0)))

    k_new = new_slots(proj_s[:, :, ATT_WIDTH:2 * ATT_WIDTH])
    v_new = new_slots(proj_s[:, :, 2 * ATT_WIDTH:3 * ATT_WIDTH])

    def page_spec(i):
        def imap(bi, c, pt):
            return (pt[bi, n_pages - 1 - (c * pages_per_step + i)], 0, 0)
        return pl.BlockSpec((None, PAGE_SIZE * N_HEADS, HEAD_DIM), imap)

    page_specs = [page_spec(i) for i in range(pages_per_step)]
    rows_spec = pl.BlockSpec((None, n_rows, HEAD_DIM), lambda bi, c, pt: (bi, 0, 0))
    new_spec = pl.BlockSpec((None, n_new, HALF_WIDTH), lambda bi, c, pt: (bi, 0, 0))
    grid_spec = pltpu.PrefetchScalarGridSpec(
        num_scalar_prefetch=1,
        grid=(b, steps),
        in_specs=[rows_spec, new_spec, new_spec, rows_spec,
                  pl.BlockSpec((n_rows, 1), lambda bi, c, pt: (0, 0))]
                 + page_specs * 2,
        out_specs=rows_spec,
        scratch_shapes=[pltpu.VMEM((n_rows, HALF_WIDTH), BF16),
                        pltpu.VMEM((pages_per_step * SLOTS, HALF_WIDTH), BF16),
                        pltpu.VMEM((pages_per_step * SLOTS, HALF_WIDTH), BF16),
                        pltpu.VMEM((n_rows, HALF_WIDTH), F32),
                        pltpu.VMEM((n_rows, 1), F32)],
    )
    out = pl.pallas_call(
        functools.partial(_attn_sample_body, pages_per_step=pages_per_step, n_tok=n_tok),
        grid_spec=grid_spec,
        out_shape=jax.ShapeDtypeStruct((b, n_rows, HEAD_DIM), F32),
        compiler_params=_cparams(("parallel", "arbitrary")),
        name="attn_sample",
    )(page_table, q_rows, k_new, v_new, g_rows, bias_col,
      *([cache_k] * pages_per_step), *([cache_v] * pages_per_step))
    return out.reshape(b, n_tok, ATT_WIDTH)


def _ssm_params_body(are_ref, aim_ref, ldt_ref, bre_ref, bim_ref, abr_ref, abi_ref, bbr_ref, bbi_ref):
    a_re = are_ref[...]
    a_im = aim_ref[...]
    dt = jnp.exp(ldt_ref[...])
    mag = jnp.exp(dt * a_re)
    ab_re = mag * jnp.cos(dt * a_im)
    ab_im = mag * jnp.sin(dt * a_im)
    abr_ref[...] = ab_re
    abi_ref[...] = ab_im
    den = a_re * a_re + a_im * a_im
    n_re = ab_re - 1.0
    co_re = (n_re * a_re + ab_im * a_im) / den
    co_im = (ab_im * a_re - n_re * a_im) / den
    b_re = bre_ref[...]
    b_im = bim_ref[...]
    bbr_ref[...] = co_re[:, None, :] * b_re - co_im[:, None, :] * b_im
    bbi_ref[...] = co_re[:, None, :] * b_im + co_im[:, None, :] * b_re


def _ssm_params(a_re, a_im, log_dt, b_re, b_im):
    g, p = a_re.shape
    c = b_re.shape[1]
    return pl.pallas_call(
        _ssm_params_body,
        name="ssm_params",
        out_shape=[jax.ShapeDtypeStruct((g, p), F32), jax.ShapeDtypeStruct((g, p), F32),
                   jax.ShapeDtypeStruct((g, c, p), F32), jax.ShapeDtypeStruct((g, c, p), F32)],
    )(a_re, a_im, log_dt.reshape(g, 1), b_re, b_im)


def _gelu_tanh(y):
    return 0.5 * y * (1.0 + jnp.tanh(math.sqrt(2.0 / math.pi) * (y + 0.044715 * (y * y * y))))


def _ssm_prompt_body(u_ref, g_ref, b_ref, c_ref, gw_ref, ab_ref, dg_ref, o_ref, hT_ref, st_ref, tm_ref, *sub_refs,
                     lc, nb, nsc):
    ci = pl.program_id(1)
    rows = lc * SUBLANES
    n_slab = STATE_LANES // LANES
    seqs = [(sel, b) for sel in range(2) for b in range(nb)]
    seq_rows = lambda i: pl.ds(i, lc, stride=SUBLANES)
    for i, (sel, b) in enumerate(seqs):
        tm_ref.at[0][seq_rows(i), :] = u_ref[b, :, sel * LANES:(sel + 1) * LANES]
        tm_ref.at[1][seq_rows(i), :] = g_ref[b, :, sel * LANES:(sel + 1) * LANES]

    @pl.when(ci == 0)
    def _():
        st_ref[...] = jnp.zeros_like(st_ref)

    ar = [ab_ref[j] for j in range(n_slab)]
    ai = [ab_ref[n_slab + j] for j in range(n_slab)]
    h = [st_ref[j] for j in range(2 * n_slab)]
    sub = rows // nsc
    first = (lax.broadcasted_iota(jnp.int32, (sub, LANES), 0) % SUBLANES) < (SUBLANES // 2)
    d = jnp.broadcast_to(dg_ref[0][None], (sub // SUBLANES, SUBLANES, LANES)).reshape(sub, LANES)
    gb = jnp.broadcast_to(dg_ref[1][None], (sub // SUBLANES, SUBLANES, LANES)).reshape(sub, LANES)
    bu_refs, hh_refs = sub_refs[:nsc], sub_refs[nsc:]

    def drive(sc):
        u = tm_ref[0, sc * sub:(sc + 1) * sub, :]
        lhs = jnp.concatenate([jnp.where(first, u, 0.0), jnp.where(first, 0.0, u)], axis=1).astype(BF16)
        bu_refs[sc][...] = jnp.dot(lhs, b_ref[...], preferred_element_type=F32)

    drive(0)
    for sc in range(nsc):
        lo = sc * sub
        if sc + 1 < nsc:
            drive(sc + 1)
        bu_ref, hh_ref = bu_refs[sc], hh_refs[sc]
        for r0 in range(0, sub, SUBLANES):
            for j in range(n_slab):
                re_l = slice(j * LANES, (j + 1) * LANES)
                im_l = slice(STATE_LANES + j * LANES, STATE_LANES + (j + 1) * LANES)
                hr, hi = h[j], h[n_slab + j]
                h[j] = ar[j] * hr - ai[j] * hi + bu_ref[r0:r0 + SUBLANES, re_l]
                h[n_slab + j] = ar[j] * hi + ai[j] * hr + bu_ref[r0:r0 + SUBLANES, im_l]
                hh_ref[r0:r0 + SUBLANES, re_l] = h[j]
                hh_ref[r0:r0 + SUBLANES, im_l] = h[n_slab + j]
        u = tm_ref[0, lo:lo + sub, :]
        y2 = lax.dot_general(hh_ref[...].astype(BF16), c_ref[...], (((1,), (1,)), ((), ())),
                             preferred_element_type=F32)
        y = jnp.where(first, y2[:, :LANES], y2[:, LANES:]) + d * u
        z = _gelu_tanh(y)
        g2 = jnp.dot(z.astype(BF16), gw_ref[...], preferred_element_type=F32)
        gate = jax.nn.sigmoid(jnp.where(first, g2[:, :LANES], g2[:, LANES:]) + gb)
        tm_ref[2, lo:lo + sub, :] = z * gate * jax.nn.sigmoid(tm_ref[1, lo:lo + sub, :])
    for j in range(2 * n_slab):
        st_ref[j] = h[j]
        hT_ref[j] = h[j]
    for i, (sel, b) in enumerate(seqs):
        o_ref[b, :, sel * LANES:(sel + 1) * LANES] = tm_ref.at[2][seq_rows(i), :].astype(o_ref.dtype)


def _ssm_prompt(proj, b_st, c_st, gw_st, ab_tm, dg_tm, *, lc=256, nsc=4):
    nb, t, _ = proj.shape
    assert t % lc == 0 and (lc * SUBLANES) % (nsc * SUBLANES) == 0
    npair = N_GBLOCKS // 2
    n_slab2 = 2 * STATE_LANES // LANES
    wid = 2 * LANES
    seq_spec = lambda off: pl.BlockSpec((nb, lc, wid), lambda p, c: (0, c, off + p))
    u_off = 4 * ATT_WIDTH // wid
    par_spec = lambda shp: pl.BlockSpec((None,) + shp, lambda p, c: (p,) + (0,) * len(shp))
    return pl.pallas_call(
        functools.partial(_ssm_prompt_body, lc=lc, nb=nb, nsc=nsc),
        grid=(npair, t // lc),
        in_specs=[seq_spec(u_off), seq_spec(u_off + npair),
                  par_spec((2 * LANES, 2 * STATE_LANES)), par_spec((2 * LANES, 2 * STATE_LANES)),
                  par_spec((LANES, 2 * LANES)), par_spec((n_slab2, SUBLANES, LANES)),
                  par_spec((2, SUBLANES, LANES))],
        out_specs=[seq_spec(0), par_spec((n_slab2, SUBLANES, LANES))],
        out_shape=[jax.ShapeDtypeStruct((nb, t, SSM_WIDTH), BF16),
                   jax.ShapeDtypeStruct((npair, n_slab2, SUBLANES, LANES), F32)],
        scratch_shapes=[pltpu.VMEM((n_slab2, SUBLANES, LANES), F32),
                        pltpu.VMEM((3, lc * SUBLANES, LANES), F32)]
                       + [pltpu.VMEM((lc * SUBLANES // nsc, 2 * STATE_LANES), F32)] * (2 * nsc),
        compiler_params=_cparams(("parallel", "arbitrary")),
        name="ssm_prompt",
    )(proj, proj, b_st, c_st, gw_st, ab_tm, dg_tm)


def _ssm_sample_body(u_ref, g_ref, hre_ref, him_ref, b_ref, c_ref, gw_ref, ab_ref, dg_ref,
                     o_ref, ore_ref, oim_ref, *, n_tok, nb):
    hp = lax.Precision.HIGHEST
    u = u_ref[...].reshape(n_tok * nb, LANES)
    bu = jnp.dot(u, b_ref[...], preferred_element_type=F32, precision=hp)
    ar = ab_ref[0:1, :]
    ai = ab_ref[1:2, :]
    hr = hre_ref[...]
    hi = him_ref[...]
    hs = []
    for t in range(n_tok):
        bre = bu[t * nb:(t + 1) * nb, :STATE_LANES]
        bim = bu[t * nb:(t + 1) * nb, STATE_LANES:]
        hr, hi = ar * hr - ai * hi + bre, ar * hi + ai * hr + bim
        hs.append(jnp.concatenate([hr, hi], axis=1))
    ore_ref[...] = hr
    oim_ref[...] = hi
    hh = jnp.concatenate(hs, axis=0)
    y = lax.dot_general(hh, c_ref[...], (((1,), (1,)), ((), ())), preferred_element_type=F32, precision=hp)
    y = y + dg_ref[0:1, :] * u
    z = _gelu_tanh(y)
    gate = jax.nn.sigmoid(jnp.dot(z, gw_ref[...], preferred_element_type=F32, precision=hp) + dg_ref[1:2, :])
    out = z * gate * jax.nn.sigmoid(g_ref[...].reshape(n_tok * nb, LANES))
    o_ref[...] = out.reshape(n_tok, nb, LANES)


def _ssm_sample(u_t, g_t, h_re, h_im, b_blk, c_blk, gw_blk, ab_row, dg_row):
    n_tok, nb, _ = u_t.shape
    seq_spec = pl.BlockSpec((n_tok, nb, LANES), lambda gb: (0, 0, gb))
    st_spec = pl.BlockSpec((nb, STATE_LANES), lambda gb: (0, gb))
    par_spec = lambda shp: pl.BlockSpec((None,) + shp, lambda gb: (gb,) + (0,) * len(shp))
    return pl.pallas_call(
        functools.partial(_ssm_sample_body, n_tok=n_tok, nb=nb),
        grid=(N_GBLOCKS,),
        in_specs=[seq_spec, seq_spec, st_spec, st_spec,
                  par_spec((LANES, 2 * STATE_LANES)), par_spec((LANES, 2 * STATE_LANES)),
                  par_spec((LANES, LANES)), par_spec((2, STATE_LANES)), par_spec((2, LANES))],
        out_specs=[seq_spec, st_spec, st_spec],
        out_shape=[jax.ShapeDtypeStruct(u_t.shape, F32),
                   jax.ShapeDtypeStruct(h_re.shape, F32), jax.ShapeDtypeStruct(h_im.shape, F32)],
        compiler_params=_cparams(("parallel",)),
        name="ssm_sample",
    )(u_t, g_t, h_re, h_im, b_blk, c_blk, gw_blk, ab_row, dg_row)


def _expand_body(re_ref, im_ref, o32_ref, o16_ref, *, im_sign):
    rows = re_ref.shape[0]
    src = lax.broadcasted_iota(jnp.int32, (STATE_DIM, STATE_LANES), 0)
    dst = lax.broadcasted_iota(jnp.int32, (STATE_DIM, STATE_LANES), 1)
    spread = (src == dst % STATE_DIM).astype(F32)
    own = (lax.broadcasted_iota(jnp.int32, (rows, STATE_LANES), 0) // SSM_GROUP
           == lax.broadcasted_iota(jnp.int32, (rows, STATE_LANES), 1) // STATE_DIM)
    halves = []
    for ref, sign in ((re_ref, 1.0), (im_ref, im_sign)):
        full = jnp.dot(ref[...], spread, preferred_element_type=F32, precision=lax.Precision.HIGHEST)
        halves.append(jnp.where(own, sign * full, 0.0))
    out = jnp.concatenate(halves, axis=1)
    o32_ref[...] = out
    o16_ref[...] = out.astype(BF16)


def _expand(x_re, x_im, *, im_sign):
    g, c, p = x_re.shape
    in_spec = pl.BlockSpec((LANES, p), lambda i: (i, 0))
    out_spec = pl.BlockSpec((None, LANES, 2 * STATE_LANES), lambda i: (i, 0, 0))
    shape = (N_GBLOCKS, LANES, 2 * STATE_LANES)
    return pl.pallas_call(
        functools.partial(_expand_body, im_sign=im_sign),
        grid=(N_GBLOCKS,),
        in_specs=[in_spec, in_spec],
        out_specs=[out_spec, out_spec],
        out_shape=[jax.ShapeDtypeStruct(shape, F32), jax.ShapeDtypeStruct(shape, BF16)],
        compiler_params=_cparams(("parallel",)),
        name="ssm_expand",
    )(x_re.reshape(g * c, p), x_im.reshape(g * c, p))


def _ssm_block_weights(ab_re, ab_im, bb_re, bb_im, c_re, c_im, d, glu_w, glu_b):
    nb, gpb = N_GBLOCKS, GROUPS_PER_BLOCK
    eye = jnp.eye(gpb, dtype=F32)
    b_blk, b_blk16 = _expand(bb_re, bb_im, im_sign=1.0)
    c_blk, c_blk16 = _expand(c_re, c_im, im_sign=-1.0)
    gw = glu_w.reshape(nb, gpb, SSM_GROUP, SSM_GROUP)
    gw_blk = jnp.einsum('bgce,gh->bgche', gw, eye).reshape(nb, LANES, LANES)
    ab_row = jnp.stack([ab_re.reshape(nb, STATE_LANES), ab_im.reshape(nb, STATE_LANES)], axis=1)
    dg_row = jnp.stack([d.reshape(nb, LANES), glu_b.reshape(nb, LANES)], axis=1)
    return (b_blk, c_blk, gw_blk, ab_row, dg_row), (b_blk16, c_blk16)


def _pair_weights(b_blk16, c_blk16, gw_blk, ab_row, dg_row, nbatch):
    npair = N_GBLOCKS // 2
    n_slab = STATE_LANES // LANES
    b_st = b_blk16.reshape(npair, 2 * LANES, 2 * STATE_LANES)
    c_st = c_blk16.reshape(npair, 2 * LANES, 2 * STATE_LANES)
    gw_st = gw_blk.reshape(npair, 2, LANES, LANES).transpose(0, 2, 1, 3).reshape(
        npair, LANES, 2 * LANES).astype(BF16)
    ab = ab_row.reshape(npair, 2, 2, n_slab, LANES).transpose(0, 2, 3, 1, 4)
    ab_tm = jnp.repeat(ab, nbatch, axis=3).reshape(npair, 2 * n_slab, 2 * nbatch, LANES)
    dg = dg_row.reshape(npair, 2, 2, LANES).transpose(0, 2, 1, 3)
    dg_tm = jnp.repeat(dg, nbatch, axis=2)
    return b_st, c_st, gw_st, ab_tm, dg_tm


def _dense_tail_prompt(x2d, ssm, att, w_out, norm_mlp_w, w_up, w_down, norm_final_w):
    x1 = _out_proj(ssm, att, w_out, x2d)
    hid = _matmul(x1, w_up, name="mlp_up", bm=512, bn=1024, bk=D_MODEL, norm_w=norm_mlp_w, epilogue="relu2",
                  out_dtype=BF16)
    x2 = _matmul(hid, w_down, name="mlp_down", bm=1024, bn=1024, bk=4096, res=x1, epilogue="res")
    return _rmsnorm(x2, norm_final_w)


def _dense_tail_sample(x2d, mixed, w_out, norm_mlp_w, w_up, w_down, norm_final_w):
    x1, w_out_b = _matmul(mixed, w_out, name="out_proj_s", bm=LANES, bn=512, bk=D_MODEL, res=x2d, epilogue="res",
                          cast_w=True)
    hid, w_up_b = _matmul(x1, w_up, name="mlp_up_s", bm=LANES, bn=512, bk=D_MODEL, norm_w=norm_mlp_w,
                          epilogue="relu2", out_dtype=BF16, cast_w=True)
    x2, w_down_b = _matmul(hid, w_down, name="mlp_down_s", bm=LANES, bn=1024, bk=2048, res=x1, epilogue="res",
                           cast_w=True)
    return _rmsnorm(x2, norm_final_w), w_out_b, w_up_b, w_down_b


def kernel(x_prompt, x_sample, cache_k, cache_v, state_ssm_re, state_ssm_im, page_table, norm_mix_w, w_in, sb_bias,
           ssm_a_re, ssm_a_im, ssm_b_re, ssm_b_im, ssm_c_re, ssm_c_im, ssm_d, ssm_log_dt, glu_w, glu_b, w_out,
           norm_mlp_w, w_up, w_down, norm_final_w):
    depth = w_in.shape[0]
    assert depth == 1
    nb_p, t_p, _ = x_prompt.shape
    nb_s, t_s, _ = x_sample.shape
    assert nb_p * 2 == SUBLANES
    assert nb_s * t_s == LANES
    l = 0

    ab_re, ab_im, bb_re, bb_im = _ssm_params(ssm_a_re[l], ssm_a_im[l], ssm_log_dt[l],
                                             ssm_b_re[l].transpose(0, 2, 1), ssm_b_im[l].transpose(0, 2, 1))
    blk, blk16 = _ssm_block_weights(ab_re, ab_im, bb_re, bb_im, ssm_c_re[l], ssm_c_im[l], ssm_d[l], glu_w[l],
                                    glu_b[l])
    pair = _pair_weights(*blk16, *blk[2:], nbatch=nb_p)

    xs = x_sample.reshape(nb_s * t_s, D_MODEL)
    proj_s, w_in_b = _matmul(xs, w_in[l], name="in_proj_s", bm=LANES, bn=512, bk=D_MODEL, norm_w=norm_mix_w[l],
                             cast_w=True)
    proj_s3 = proj_s.reshape(nb_s, t_s, IN_WIDTH)
    n_phys = cache_k.shape[1]
    pool = (depth * n_phys, PAGE_SIZE * N_HEADS, HEAD_DIM)
    att_s = _attn_sample(proj_s3, cache_k.reshape(pool), cache_v.reshape(pool), page_table + l * n_phys,
                         sb_bias[l])
    u_t = proj_s3[:, :, 4 * ATT_WIDTH:4 * ATT_WIDTH + SSM_WIDTH].transpose(1, 0, 2)
    g_t = proj_s3[:, :, 4 * ATT_WIDTH + SSM_WIDTH:].transpose(1, 0, 2)
    ssm_t, hs_re, hs_im = _ssm_sample(u_t, g_t, state_ssm_re[l].reshape(nb_s, N_GROUPS * STATE_DIM),
                                      state_ssm_im[l].reshape(nb_s, N_GROUPS * STATE_DIM), *blk)
    ssm_s = ssm_t.transpose(1, 0, 2).reshape(nb_s * t_s, SSM_WIDTH)
    mixed_s = jnp.concatenate([ssm_s, att_s.reshape(nb_s * t_s, ATT_WIDTH)], axis=1).astype(BF16)
    y_s, w_out_b, w_up_b, w_down_b = _dense_tail_sample(xs, mixed_s, w_out[l], norm_mlp_w[l], w_up[l], w_down[l],
                                                        norm_final_w)

    xp = x_prompt.reshape(nb_p * t_p, D_MODEL)
    proj_p = _matmul(xp, w_in_b, name="in_proj", bm=512, bn=1024, bk=D_MODEL, norm_w=norm_mix_w[l])
    proj_p3 = proj_p.reshape(nb_p, t_p, IN_WIDTH)
    att_p, k_p, v_p = _attn_prompt(proj_p3, sb_bias[l])
    npair = N_GBLOCKS // 2
    ssm_p, hT = _ssm_prompt(proj_p3, *pair)
    n_slab = STATE_LANES // LANES
    hT = hT.reshape(npair, 2, n_slab, 2, nb_p, 2, STATE_DIM).transpose(1, 4, 0, 3, 2, 5, 6).reshape(
        2, nb_p, N_GROUPS, STATE_DIM)
    y_p = _dense_tail_prompt(xp, ssm_p.reshape(nb_p * t_p, SSM_WIDTH), att_p.reshape(nb_p * t_p, ATT_WIDTH),
                             w_out_b, norm_mlp_w[l], w_up_b, w_down_b, norm_final_w)

    kv_shape_p = (1, nb_p, t_p, N_HEADS, HEAD_DIM)
    kv_shape_s = (1, nb_s, t_s, N_HEADS, HEAD_DIM)
    return (y_p.reshape(nb_p, t_p, D_MODEL), y_s.reshape(nb_s, t_s, D_MODEL),
            k_p.reshape(kv_shape_p), v_p.reshape(kv_shape_p),
            hT[0][None], hT[1][None],
            proj_s3[:, :, ATT_WIDTH:2 * ATT_WIDTH].reshape(kv_shape_s),
            proj_s3[:, :, 2 * ATT_WIDTH:3 * ATT_WIDTH].reshape(kv_shape_s),
            hs_re.reshape(1, nb_s, N_GROUPS, STATE_DIM), hs_im.reshape(1, nb_s, N_GROUPS, STATE_DIM))
```

```python
import functools
import math

import jax
import jax.numpy as jnp
from jax import lax
from jax.experimental import pallas as pl
from jax.experimental.pallas import tpu as pltpu

F32 = jnp.float32
BF16 = jnp.bfloat16

D_MODEL = 4096
HEAD_DIM = 128
N_HEADS = 16
ATT_WIDTH = N_HEADS * HEAD_DIM
SSM_WIDTH = 2048
SSM_GROUP = 16
N_GROUPS = 128
STATE_DIM = 64
IN_WIDTH = 4 * ATT_WIDTH + 2 * SSM_WIDTH
D_FF = 4 * D_MODEL
PAGE_SIZE = 128
EPS = 1e-6

LANES = 128
SUBLANES = 8
HEAD_HALVES = N_HEADS // SUBLANES
GROUPS_PER_BLOCK = LANES // SSM_GROUP
N_GBLOCKS = N_GROUPS // GROUPS_PER_BLOCK
STATE_LANES = GROUPS_PER_BLOCK * STATE_DIM
VMEM_LIMIT = 62 * 1024 * 1024


def _cparams(sem):
    return pltpu.CompilerParams(dimension_semantics=sem, vmem_limit_bytes=VMEM_LIMIT)


def _rms_rows(x, w):
    return x * lax.rsqrt(jnp.mean(x * x, axis=-1, keepdims=True) + EPS) * w


def _mm_body(*refs, norm, epilogue, nk, cast_w):
    it = iter(refs)
    x_ref = next(it)
    nw_ref = next(it) if norm else None
    w_ref = next(it)
    res_ref = next(it) if epilogue == "res" else None
    o_ref = next(it)
    wb_ref = next(it) if cast_w else None
    xs_ref = next(it) if norm else None
    acc_ref = next(it) if nk > 1 else None
    j = pl.program_id(1)
    k = pl.program_id(2)

    if norm:
        @pl.when(j == 0)
        def _():
            xs_ref[...] = _rms_rows(x_ref[...], nw_ref[...]).astype(BF16)
        lhs = xs_ref[...]
    else:
        lhs = x_ref[...]
    w = w_ref[...]
    if cast_w:
        w = w.astype(BF16)
        wb_ref[...] = w
    part = jnp.dot(lhs, w, preferred_element_type=F32)

    def finish(acc):
        if epilogue == "relu2":
            r = jnp.maximum(acc, 0.0)
            acc = r * r
        elif epilogue == "res":
            acc = acc + res_ref[...]
        o_ref[...] = acc.astype(o_ref.dtype)

    if nk == 1:
        finish(part)
    else:
        @pl.when(k == 0)
        def _():
            acc_ref[...] = part

        @pl.when(k > 0)
        def _():
            acc_ref[...] += part

        @pl.when(k == nk - 1)
        def _():
            finish(acc_ref[...])


def _matmul(x, w, *, name, bm, bn, bk, norm_w=None, res=None, epilogue="none", out_dtype=F32, cast_w=False):
    m, kdim = x.shape
    n = w.shape[1]
    bm = min(bm, m)
    nk = kdim // bk
    norm = norm_w is not None
    assert not (norm and nk != 1)
    assert not (cast_w and m != bm)
    in_specs = [pl.BlockSpec((bm, bk), lambda i, j, k: (i, k))]
    args = [x]
    if norm:
        in_specs.append(pl.BlockSpec((1, kdim), lambda i, j, k: (0, 0)))
        args.append(norm_w.reshape(1, kdim))
    in_specs.append(pl.BlockSpec((bk, bn), lambda i, j, k: (k, j)))
    args.append(w)
    if epilogue == "res":
        in_specs.append(pl.BlockSpec((bm, bn), lambda i, j, k: (i, j)))
        args.append(res)
    scratch = []
    if norm:
        scratch.append(pltpu.VMEM((bm, kdim), BF16))
    if nk > 1:
        scratch.append(pltpu.VMEM((bm, bn), F32))
    out_specs = pl.BlockSpec((bm, bn), lambda i, j, k: (i, j))
    out_shape = jax.ShapeDtypeStruct((m, n), out_dtype)
    if cast_w:
        out_specs = [out_specs, pl.BlockSpec((bk, bn), lambda i, j, k: (k, j))]
        out_shape = [out_shape, jax.ShapeDtypeStruct((kdim, n), BF16)]
    return pl.pallas_call(
        functools.partial(_mm_body, norm=norm, epilogue=epilogue, nk=nk, cast_w=cast_w),
        grid=(m // bm, n // bn, nk),
        in_specs=in_specs,
        out_specs=out_specs,
        out_shape=out_shape,
        scratch_shapes=scratch,
        compiler_params=_cparams(("parallel", "arbitrary", "arbitrary")),
        name=name,
    )(*args)


def _out_proj_body(a_ref, b_ref, wa_ref, wb_ref, res_ref, o_ref):
    acc = jnp.dot(a_ref[...], wa_ref[...], preferred_element_type=F32)
    acc = acc + jnp.dot(b_ref[...], wb_ref[...], preferred_element_type=F32)
    o_ref[...] = acc + res_ref[...]


def _out_proj(a, b, w, res, *, bm=1024, bn=1024):
    m, ka = a.shape
    kb = b.shape[1]
    n = w.shape[1]
    bm = min(bm, m)
    assert ka == kb
    row_spec = lambda wid: pl.BlockSpec((bm, wid), lambda i, j: (i, 0))
    return pl.pallas_call(
        _out_proj_body,
        grid=(m // bm, n // bn),
        in_specs=[row_spec(ka), row_spec(kb),
                  pl.BlockSpec((ka, bn), lambda i, j: (0, j)), pl.BlockSpec((kb, bn), lambda i, j: (1, j)),
                  pl.BlockSpec((bm, bn), lambda i, j: (i, j))],
        out_specs=pl.BlockSpec((bm, bn), lambda i, j: (i, j)),
        out_shape=jax.ShapeDtypeStruct((m, n), F32),
        compiler_params=_cparams(("parallel", "arbitrary")),
        name="out_proj",
    )(a, b, w, w, res)


def _rmsnorm_body(x_ref, w_ref, o_ref):
    o_ref[...] = _rms_rows(x_ref[...], w_ref[...])


def _rmsnorm(x, w, *, bm=256):
    m, d = x.shape
    bm = min(bm, m)
    return pl.pallas_call(
        _rmsnorm_body,
        grid=(m // bm,),
        in_specs=[pl.BlockSpec((bm, d), lambda i: (i, 0)), pl.BlockSpec((1, d), lambda i: (0, 0))],
        out_specs=pl.BlockSpec((bm, d), lambda i: (i, 0)),
        out_shape=jax.ShapeDtypeStruct((m, d), F32),
        compiler_params=_cparams(("parallel",)),
        name="rmsnorm",
    )(x, w.reshape(1, d))


LOG2E = 1.0 / math.log(2.0)
ATT_SCALE2 = HEAD_DIM ** -0.5 * LOG2E


def _softplus2(z2):
    return jnp.maximum(z2, 0.0) + jnp.log(1.0 + jnp.exp2(-jnp.abs(z2))) * LOG2E


def _ride_along_specs(ws, n_steps, step_index):
    specs = [pl.BlockSpec((w.shape[0] // n_steps, w.shape[1]), lambda *ids: (step_index(*ids), 0)) for w in ws]
    for w in ws:
        assert w.shape[0] % (n_steps * 2 * SUBLANES) == 0
    return specs, [jax.ShapeDtypeStruct(w.shape, BF16) for w in ws]


def _ride_along_casts(src_refs, dst_refs):
    for src, dst in zip(src_refs, dst_refs):
        dst[...] = src[...].astype(BF16)


def _attn_prompt_body(bias_ref, q_ref, k_ref, v_ref, g_ref, *refs, tk, nsub, nhead, n_cast):
    cast_src, (o_ref, ko_ref, vo_ref), refs = refs[:n_cast], refs[n_cast:n_cast + 3], refs[n_cast + 3:]
    cast_dst, (kb_ref, vb_ref), state_refs = refs[:n_cast], refs[n_cast:n_cast + 2], refs[n_cast + 2:]
    _ride_along_casts(cast_src, cast_dst)
    acc_refs, carry_refs = state_refs[:nhead], state_refs[nhead:]
    hg = pl.program_id(1)
    qi = pl.program_id(2)

    @pl.when(qi == 0)
    def _():
        k = k_ref[...]
        v = v_ref[...]
        ko_ref[...] = k
        vo_ref[...] = v
        kb_ref[...] = k.astype(BF16)
        vb_ref[...] = v.astype(BF16)

    tq = tk * nsub
    row = lax.broadcasted_iota(jnp.int32, (tk, tk), 0)
    col = lax.broadcasted_iota(jnp.int32, (tk, tk), 1)
    later = (row > col).astype(BF16)
    lanes = lambda h: slice(h * HEAD_DIM, (h + 1) * HEAD_DIM)
    q = [(q_ref[:, lanes(h)] * ATT_SCALE2).astype(BF16) for h in range(nhead)]
    bias = [bias_ref[hg * nhead + h] * LOG2E for h in range(nhead)]

    for ref in state_refs:
        ref[...] = jnp.zeros_like(ref)

    def block(h, lo, kb, valid):
        start = pl.multiple_of(kb * tk, tk)
        kblk = kb_ref[pl.ds(start, tk), lanes(h)]
        vblk = vb_ref[pl.ds(start, tk), lanes(h)]
        z = lax.dot_general(q[h][lo:], kblk, (((1,), (1,)), ((), ())), preferred_element_type=F32) + bias[h]
        sp = _softplus2(z)
        if valid is not None:
            sp = jnp.where(valid, sp, 0.0)
        inner = jnp.dot(sp.astype(BF16), later, preferred_element_type=F32)
        carry = carry_refs[h][lo:, :]
        w = jnp.exp2(z - sp - inner - carry)
        if valid is not None:
            w = jnp.where(valid, w, 0.0)
        acc_refs[h][lo:, :] += jnp.dot(w.astype(BF16), vblk, preferred_element_type=F32)
        carry_refs[h][lo:, :] = carry + inner[:, :1] + sp[:, :1]

    for j in reversed(range(nsub)):
        m = tq - j * tk
        ri = lax.broadcasted_iota(jnp.int32, (m, tk), 0)
        ci = lax.broadcasted_iota(jnp.int32, (m, tk), 1)
        valid = (ri >= tk) | (ci < ri)
        for h in range(nhead):
            block(h, j * tk, nsub * qi + j, valid)

    @pl.loop(0, qi)
    def _(it):
        for j in range(nsub):
            for h in range(nhead):
                block(h, 0, nsub * (qi - it) - 1 - j, None)

    for h in range(nhead):
        o_ref[:, lanes(h)] = (acc_refs[h][...] * jax.nn.sigmoid(g_ref[:, lanes(h)])).astype(o_ref.dtype)


def _attn_prompt(proj, sb_bias, cast_ws=(), *, tk=256, nsub=4, nhead=2):
    b, t, _ = proj.shape
    tq = tk * nsub
    assert t % tq == 0 and N_HEADS % nhead == 0
    wid = nhead * HEAD_DIM
    ngrp = N_HEADS // nhead
    nq = t // tq
    qspec = lambda off: pl.BlockSpec((None, tq, wid), lambda bi, h, qi: (bi, qi, off + h))
    kvspec = lambda off: pl.BlockSpec((None, t, wid), lambda bi, h, qi: (bi, 0, off + h))
    cast_specs, cast_shapes = _ride_along_specs(cast_ws, b * ngrp * nq, lambda bi, h, qi: (bi * ngrp + h) * nq + qi)
    return pl.pallas_call(
        functools.partial(_attn_prompt_body, tk=tk, nsub=nsub, nhead=nhead, n_cast=len(cast_ws)),
        grid=(b, ngrp, nq),
        in_specs=[pl.BlockSpec(memory_space=pltpu.SMEM),
                  qspec(0), kvspec(ngrp), kvspec(2 * ngrp), qspec(3 * ngrp), *cast_specs],
        out_specs=[qspec(0), kvspec(0), kvspec(0), *cast_specs],
        out_shape=[jax.ShapeDtypeStruct((b, t, ATT_WIDTH), BF16),
                   jax.ShapeDtypeStruct((b, t, ATT_WIDTH), F32),
                   jax.ShapeDtypeStruct((b, t, ATT_WIDTH), F32), *cast_shapes],
        scratch_shapes=[pltpu.VMEM((t, wid), BF16), pltpu.VMEM((t, wid), BF16),
                        *[pltpu.VMEM((tq, HEAD_DIM), F32)] * nhead, *[pltpu.VMEM((tq, 1), F32)] * nhead],
        compiler_params=_cparams(("parallel", "parallel", "arbitrary")),
        name="attn_prompt",
    )(sb_bias, proj, proj, proj, proj, *cast_ws)


SLOTS = PAGE_SIZE * HEAD_HALVES
HALF_WIDTH = SUBLANES * HEAD_DIM


def _attn_sample_body(pt_ref, q_ref, kn_ref, vn_ref, g_ref, bias_ref, *refs, pages_per_step, n_tok):
    kp_refs = refs[:pages_per_step]
    vp_refs = refs[pages_per_step:2 * pages_per_step]
    o_ref = refs[2 * pages_per_step]
    qbd_ref, kcat_ref, vcat_ref, acc_ref, carry_ref = refs[2 * pages_per_step + 1:]
    c = pl.program_id(1)
    n_rows = n_tok * N_HEADS
    bias = bias_ref[...] * LOG2E
    row = lax.broadcasted_iota(jnp.int32, (SLOTS, SLOTS), 0)
    col = lax.broadcasted_iota(jnp.int32, (SLOTS, SLOTS), 1)
    later = ((row // HEAD_HALVES) > (col // HEAD_HALVES)).astype(BF16)

    def attend(npages, causal):
        n = npages * SLOTS
        z = lax.dot_general(qbd_ref[...], kcat_ref[:n, :], (((1,), (1,)), ((), ())),
                            preferred_element_type=F32) + bias
        rhalf = (lax.broadcasted_iota(jnp.int32, (n_rows, n), 0) % N_HEADS) // SUBLANES
        slot = lax.broadcasted_iota(jnp.int32, (n_rows, n), 1)
        valid = rhalf == slot % HEAD_HALVES
        if causal:
            valid = valid & (slot // HEAD_HALVES < lax.broadcasted_iota(jnp.int32, (n_rows, n), 0) // N_HEADS)
        sp = jnp.where(valid, _softplus2(z), 0.0)
        pages = lambda a: [a[:, i * SLOTS:(i + 1) * SLOTS] for i in range(npages)]
        stacked = jnp.concatenate(pages(sp), axis=0).astype(BF16)
        inner = jnp.dot(stacked, later, preferred_element_type=F32)
        inner = jnp.concatenate([inner[i * n_rows:(i + 1) * n_rows] for i in range(npages)], axis=1)
        carry = carry_ref[...]
        carries = []
        for i in range(npages):
            carries.append(jnp.broadcast_to(carry, (n_rows, SLOTS)))
            f = i * SLOTS
            carry = carry + inner[:, f:f + 1] + sp[:, f:f + 1] + sp[:, f + 1:f + 2]
        carry_ref[...] = carry
        w = jnp.where(valid, jnp.exp2(z - sp - inner - jnp.concatenate(carries, axis=1)), 0.0)
        acc_ref[...] += jnp.dot(w.astype(BF16), vcat_ref[:n, :], preferred_element_type=F32)

    @pl.when(c == 0)
    def _():
        q = q_ref[...] * ATT_SCALE2
        qt = jnp.concatenate([q] * SUBLANES, axis=1)
        rh = lax.broadcasted_iota(jnp.int32, (n_rows, HALF_WIDTH), 0) % SUBLANES
        lh = lax.broadcasted_iota(jnp.int32, (n_rows, HALF_WIDTH), 1) // HEAD_DIM
        qbd_ref[...] = jnp.where(rh == lh, qt, 0.0).astype(BF16)
        acc_ref[...] = jnp.zeros_like(acc_ref)
        carry_ref[...] = jnp.zeros_like(carry_ref)
        n_new = kn_ref.shape[0]
        kcat_ref[:n_new, :] = kn_ref[...].astype(BF16)
        vcat_ref[:n_new, :] = vn_ref[...].astype(BF16)
        kcat_ref[n_new:SLOTS, :] = jnp.zeros((SLOTS - n_new, HALF_WIDTH), BF16)
        vcat_ref[n_new:SLOTS, :] = jnp.zeros((SLOTS - n_new, HALF_WIDTH), BF16)
        attend(1, True)

    for i in range(pages_per_step):
        for h8 in range(SUBLANES):
            dst = (slice(i * SLOTS, (i + 1) * SLOTS), slice(h8 * HEAD_DIM, (h8 + 1) * HEAD_DIM))
            kcat_ref[dst] = kp_refs[i][pl.ds(h8, SLOTS, stride=SUBLANES), :].astype(BF16)
            vcat_ref[dst] = vp_refs[i][pl.ds(h8, SLOTS, stride=SUBLANES), :].astype(BF16)
    attend(pages_per_step, False)

    @pl.when(c == pl.num_programs(1) - 1)
    def _():
        rh = lax.broadcasted_iota(jnp.int32, (n_rows, HEAD_DIM), 0) % SUBLANES
        out = jnp.zeros((n_rows, HEAD_DIM), F32)
        for h8 in range(SUBLANES):
            out = out + jnp.where(rh == h8, acc_ref[:, h8 * HEAD_DIM:(h8 + 1) * HEAD_DIM], 0.0)
        o_ref[...] = out * jax.nn.sigmoid(g_ref[...])


def _attn_sample(proj_s, cache_k, cache_v, page_table, sb_bias, *, pages_per_step=8):
    b, n_tok, _ = proj_s.shape
    n_pages = page_table.shape[1]
    n_rows = n_tok * N_HEADS
    assert n_pages % pages_per_step == 0
    steps = n_pages // pages_per_step
    q_rows = proj_s[:, :, :ATT_WIDTH].reshape(b, n_rows, HEAD_DIM)
    g_rows = proj_s[:, :, 3 * ATT_WIDTH:4 * ATT_WIDTH].reshape(b, n_rows, HEAD_DIM)
    bias_col = jnp.tile(sb_bias, n_tok).reshape(n_rows, 1)
    n_new = 2 * SUBLANES
    assert n_tok * HEAD_HALVES <= n_new

    def new_slots(a):
        a = a.reshape(b, n_tok * HEAD_HALVES, HALF_WIDTH)
        return jnp.pad(a, ((0, 0), (0, n_new - n_tok * HEAD_HALVES), (0, 0)))

    k_new = new_slots(proj_s[:, :, ATT_WIDTH:2 * ATT_WIDTH])
    v_new = new_slots(proj_s[:, :, 2 * ATT_WIDTH:3 * ATT_WIDTH])

    def page_spec(i):
        def imap(bi, c, pt):
            return (pt[bi, n_pages - 1 - (c * pages_per_step + i)], 0, 0)
        return pl.BlockSpec((None, PAGE_SIZE * N_HEADS, HEAD_DIM), imap)

    page_specs = [page_spec(i) for i in range(pages_per_step)]
    rows_spec = pl.BlockSpec((None, n_rows, HEAD_DIM), lambda bi, c, pt: (bi, 0, 0))
    new_spec = pl.BlockSpec((None, n_new, HALF_WIDTH), lambda bi, c, pt: (bi, 0, 0))
    grid_spec = pltpu.PrefetchScalarGridSpec(
        num_scalar_prefetch=1,
        grid=(b, steps),
        in_specs=[rows_spec, new_spec, new_spec, rows_spec,
                  pl.BlockSpec((n_rows, 1), lambda bi, c, pt: (0, 0))]
                 + page_specs * 2,
        out_specs=rows_spec,
        scratch_shapes=[pltpu.VMEM((n_rows, HALF_WIDTH), BF16),
                        pltpu.VMEM((pages_per_step * SLOTS, HALF_WIDTH), BF16),
                        pltpu.VMEM((pages_per_step * SLOTS, HALF_WIDTH), BF16),
                        pltpu.VMEM((n_rows, HALF_WIDTH), F32),
                        pltpu.VMEM((n_rows, 1), F32)],
    )
    out = pl.pallas_call(
        functools.partial(_attn_sample_body, pages_per_step=pages_per_step, n_tok=n_tok),
        grid_spec=grid_spec,
        out_shape=jax.ShapeDtypeStruct((b, n_rows, HEAD_DIM), F32),
        compiler_params=_cparams(("parallel", "arbitrary")),
        name="attn_sample",
    )(page_table, q_rows, k_new, v_new, g_rows, bias_col,
      *([cache_k] * pages_per_step), *([cache_v] * pages_per_step))
    return out.reshape(b, n_tok, ATT_WIDTH)


def _ssm_params_body(are_ref, aim_ref, ldt_ref, bre_ref, bim_ref, abr_ref, abi_ref, bbr_ref, bbi_ref):
    a_re = are_ref[...]
    a_im = aim_ref[...]
    dt = jnp.exp(ldt_ref[...])
    mag = jnp.exp(dt * a_re)
    ab_re = mag * jnp.cos(dt * a_im)
    ab_im = mag * jnp.sin(dt * a_im)
    abr_ref[...] = ab_re
    abi_ref[...] = ab_im
    den = a_re * a_re + a_im * a_im
    n_re = ab_re - 1.0
    co_re = (n_re * a_re + ab_im * a_im) / den
    co_im = (ab_im * a_re - n_re * a_im) / den
    b_re = bre_ref[...]
    b_im = bim_ref[...]
    bbr_ref[...] = co_re[:, None, :] * b_re - co_im[:, None, :] * b_im
    bbi_ref[...] = co_re[:, None, :] * b_im + co_im[:, None, :] * b_re


def _ssm_params(a_re, a_im, log_dt, b_re, b_im):
    g, p = a_re.shape
    c = b_re.shape[1]
    return pl.pallas_call(
        _ssm_params_body,
        name="ssm_params",
        out_shape=[jax.ShapeDtypeStruct((g, p), F32), jax.ShapeDtypeStruct((g, p), F32),
                   jax.ShapeDtypeStruct((g, c, p), F32), jax.ShapeDtypeStruct((g, c, p), F32)],
    )(a_re, a_im, log_dt.reshape(g, 1), b_re, b_im)


def _gelu_tanh(y):
    return 0.5 * y * (1.0 + jnp.tanh(math.sqrt(2.0 / math.pi) * (y + 0.044715 * (y * y * y))))


def _ssm_prompt_body(u_ref, g_ref, b_ref, c_ref, gw_ref, ab_ref, dg_ref, *refs, lc, nb, nsc, n_cast):
    cast_src, (o_ref, hT_ref), refs = refs[:n_cast], refs[n_cast:n_cast + 2], refs[n_cast + 2:]
    cast_dst, (st_ref, tm_ref), sub_refs = refs[:n_cast], refs[n_cast:n_cast + 2], refs[n_cast + 2:]
    _ride_along_casts(cast_src, cast_dst)
    ci = pl.program_id(1)
    rows = lc * SUBLANES
    n_slab = STATE_LANES // LANES
    seqs = [(sel, b) for sel in range(2) for b in range(nb)]
    seq_rows = lambda i: pl.ds(i, lc, stride=SUBLANES)
    for i, (sel, b) in enumerate(seqs):
        tm_ref.at[0][seq_rows(i), :] = u_ref[b, :, sel * LANES:(sel + 1) * LANES]
        tm_ref.at[1][seq_rows(i), :] = g_ref[b, :, sel * LANES:(sel + 1) * LANES]

    @pl.when(ci == 0)
    def _():
        st_ref[...] = jnp.zeros_like(st_ref)

    ar = [ab_ref[j] for j in range(n_slab)]
    ai = [ab_ref[n_slab + j] for j in range(n_slab)]
    h = [st_ref[j] for j in range(2 * n_slab)]
    sub = rows // nsc
    first = (lax.broadcasted_iota(jnp.int32, (sub, LANES), 0) % SUBLANES) < (SUBLANES // 2)
    d = jnp.broadcast_to(dg_ref[0][None], (sub // SUBLANES, SUBLANES, LANES)).reshape(sub, LANES)
    gb = jnp.broadcast_to(dg_ref[1][None], (sub // SUBLANES, SUBLANES, LANES)).reshape(sub, LANES)
    bu_refs, hh_refs = sub_refs[:nsc], sub_refs[nsc:]

    def drive(sc):
        u = tm_ref[0, sc * sub:(sc + 1) * sub, :]
        lhs = jnp.concatenate([jnp.where(first, u, 0.0), jnp.where(first, 0.0, u)], axis=1).astype(BF16)
        bu_refs[sc][...] = jnp.dot(lhs, b_ref[...], preferred_element_type=F32)

    drive(0)
    for sc in range(nsc):
        lo = sc * sub
        if sc + 1 < nsc:
            drive(sc + 1)
        bu_ref, hh_ref = bu_refs[sc], hh_refs[sc]
        for r0 in range(0, sub, SUBLANES):
            for j in range(n_slab):
                re_l = slice(j * LANES, (j + 1) * LANES)
                im_l = slice(STATE_LANES + j * LANES, STATE_LANES + (j + 1) * LANES)
                hr, hi = h[j], h[n_slab + j]
                h[j] = ar[j] * hr - ai[j] * hi + bu_ref[r0:r0 + SUBLANES, re_l]
                h[n_slab + j] = ar[j] * hi + ai[j] * hr + bu_ref[r0:r0 + SUBLANES, im_l]
                hh_ref[r0:r0 + SUBLANES, re_l] = h[j]
                hh_ref[r0:r0 + SUBLANES, im_l] = h[n_slab + j]
        u = tm_ref[0, lo:lo + sub, :]
        y2 = lax.dot_general(hh_ref[...].astype(BF16), c_ref[...], (((1,), (1,)), ((), ())),
                             preferred_element_type=F32)
        y = jnp.where(first, y2[:, :LANES], y2[:, LANES:]) + d * u
        z = _gelu_tanh(y)
        g2 = jnp.dot(z.astype(BF16), gw_ref[...], preferred_element_type=F32)
        gate = jax.nn.sigmoid(jnp.where(first, g2[:, :LANES], g2[:, LANES:]) + gb)
        tm_ref[2, lo:lo + sub, :] = z * gate * jax.nn.sigmoid(tm_ref[1, lo:lo + sub, :])
    for j in range(2 * n_slab):
        st_ref[j] = h[j]
        hT_ref[j] = h[j]
    for i, (sel, b) in enumerate(seqs):
        o_ref[b, :, sel * LANES:(sel + 1) * LANES] = tm_ref.at[2][seq_rows(i), :].astype(o_ref.dtype)


def _ssm_prompt(proj, b_st, c_st, gw_st, ab_tm, dg_tm, cast_ws=(), *, lc=256, nsc=4):
    nb, t, _ = proj.shape
    assert t % lc == 0 and (lc * SUBLANES) % (nsc * SUBLANES) == 0
    npair = N_GBLOCKS // 2
    n_slab2 = 2 * STATE_LANES // LANES
    wid = 2 * LANES
    nchunk = t // lc
    seq_spec = lambda off: pl.BlockSpec((nb, lc, wid), lambda p, c: (0, c, off + p))
    u_off = 4 * ATT_WIDTH // wid
    par_spec = lambda shp: pl.BlockSpec((None,) + shp, lambda p, c: (p,) + (0,) * len(shp))
    cast_specs, cast_shapes = _ride_along_specs(cast_ws, npair * nchunk, lambda p, c: p * nchunk + c)
    return pl.pallas_call(
        functools.partial(_ssm_prompt_body, lc=lc, nb=nb, nsc=nsc, n_cast=len(cast_ws)),
        grid=(npair, nchunk),
        in_specs=[seq_spec(u_off), seq_spec(u_off + npair),
                  par_spec((2 * LANES, 2 * STATE_LANES)), par_spec((2 * LANES, 2 * STATE_LANES)),
                  par_spec((LANES, 2 * LANES)), par_spec((n_slab2, SUBLANES, LANES)),
                  par_spec((2, SUBLANES, LANES)), *cast_specs],
        out_specs=[seq_spec(0), par_spec((n_slab2, SUBLANES, LANES)), *cast_specs],
        out_shape=[jax.ShapeDtypeStruct((nb, t, SSM_WIDTH), BF16),
                   jax.ShapeDtypeStruct((npair, n_slab2, SUBLANES, LANES), F32), *cast_shapes],
        scratch_shapes=[pltpu.VMEM((n_slab2, SUBLANES, LANES), F32),
                        pltpu.VMEM((3, lc * SUBLANES, LANES), F32)]
                       + [pltpu.VMEM((lc * SUBLANES // nsc, 2 * STATE_LANES), F32)] * (2 * nsc),
        compiler_params=_cparams(("parallel", "arbitrary")),
        name="ssm_prompt",
    )(proj, proj, b_st, c_st, gw_st, ab_tm, dg_tm, *cast_ws)


def _ssm_sample_body(u_ref, g_ref, hre_ref, him_ref, b_ref, c_ref, gw_ref, ab_ref, dg_ref,
                     o_ref, ore_ref, oim_ref, *, n_tok, nb):
    hp = lax.Precision.HIGHEST
    u = u_ref[...].reshape(n_tok * nb, LANES)
    bu = jnp.dot(u, b_ref[...], preferred_element_type=F32, precision=hp)
    ar = ab_ref[0:1, :]
    ai = ab_ref[1:2, :]
    hr = hre_ref[...]
    hi = him_ref[...]
    hs = []
    for t in range(n_tok):
        bre = bu[t * nb:(t + 1) * nb, :STATE_LANES]
        bim = bu[t * nb:(t + 1) * nb, STATE_LANES:]
        hr, hi = ar * hr - ai * hi + bre, ar * hi + ai * hr + bim
        hs.append(jnp.concatenate([hr, hi], axis=1))
    ore_ref[...] = hr
    oim_ref[...] = hi
    hh = jnp.concatenate(hs, axis=0)
    y = lax.dot_general(hh, c_ref[...], (((1,), (1,)), ((), ())), preferred_element_type=F32, precision=hp)
    y = y + dg_ref[0:1, :] * u
    z = _gelu_tanh(y)
    gate = jax.nn.sigmoid(jnp.dot(z, gw_ref[...], preferred_element_type=F32, precision=hp) + dg_ref[1:2, :])
    out = z * gate * jax.nn.sigmoid(g_ref[...].reshape(n_tok * nb, LANES))
    o_ref[...] = out.reshape(n_tok, nb, LANES)


def _ssm_sample(u_t, g_t, h_re, h_im, b_blk, c_blk, gw_blk, ab_row, dg_row):
    n_tok, nb, _ = u_t.shape
    seq_spec = pl.BlockSpec((n_tok, nb, LANES), lambda gb: (0, 0, gb))
    st_spec = pl.BlockSpec((nb, STATE_LANES), lambda gb: (0, gb))
    par_spec = lambda shp: pl.BlockSpec((None,) + shp, lambda gb: (gb,) + (0,) * len(shp))
    return pl.pallas_call(
        functools.partial(_ssm_sample_body, n_tok=n_tok, nb=nb),
        grid=(N_GBLOCKS,),
        in_specs=[seq_spec, seq_spec, st_spec, st_spec,
                  par_spec((LANES, 2 * STATE_LANES)), par_spec((LANES, 2 * STATE_LANES)),
                  par_spec((LANES, LANES)), par_spec((2, STATE_LANES)), par_spec((2, LANES))],
        out_specs=[seq_spec, st_spec, st_spec],
        out_shape=[jax.ShapeDtypeStruct(u_t.shape, F32),
                   jax.ShapeDtypeStruct(h_re.shape, F32), jax.ShapeDtypeStruct(h_im.shape, F32)],
        compiler_params=_cparams(("parallel",)),
        name="ssm_sample",
    )(u_t, g_t, h_re, h_im, b_blk, c_blk, gw_blk, ab_row, dg_row)


def _expand_body(re_ref, im_ref, o32_ref, o16_ref, *, im_sign):
    rows = re_ref.shape[0]
    src = lax.broadcasted_iota(jnp.int32, (STATE_DIM, STATE_LANES), 0)
    dst = lax.broadcasted_iota(jnp.int32, (STATE_DIM, STATE_LANES), 1)
    spread = (src == dst % STATE_DIM).astype(F32)
    own = (lax.broadcasted_iota(jnp.int32, (rows, STATE_LANES), 0) // SSM_GROUP
           == lax.broadcasted_iota(jnp.int32, (rows, STATE_LANES), 1) // STATE_DIM)
    halves = []
    for ref, sign in ((re_ref, 1.0), (im_ref, im_sign)):
        full = jnp.dot(ref[...], spread, preferred_element_type=F32, precision=lax.Precision.HIGHEST)
        halves.append(jnp.where(own, sign * full, 0.0))
    out = jnp.concatenate(halves, axis=1)
    o32_ref[...] = out
    o16_ref[...] = out.astype(BF16)


def _expand(x_re, x_im, *, im_sign):
    g, c, p = x_re.shape
    in_spec = pl.BlockSpec((LANES, p), lambda i: (i, 0))
    out_spec = pl.BlockSpec((None, LANES, 2 * STATE_LANES), lambda i: (i, 0, 0))
    shape = (N_GBLOCKS, LANES, 2 * STATE_LANES)
    return pl.pallas_call(
        functools.partial(_expand_body, im_sign=im_sign),
        grid=(N_GBLOCKS,),
        in_specs=[in_spec, in_spec],
        out_specs=[out_spec, out_spec],
        out_shape=[jax.ShapeDtypeStruct(shape, F32), jax.ShapeDtypeStruct(shape, BF16)],
        compiler_params=_cparams(("parallel",)),
        name="ssm_expand",
    )(x_re.reshape(g * c, p), x_im.reshape(g * c, p))


def _ssm_block_weights(ab_re, ab_im, bb_re, bb_im, c_re, c_im, d, glu_w, glu_b):
    nb, gpb = N_GBLOCKS, GROUPS_PER_BLOCK
    eye = jnp.eye(gpb, dtype=F32)
    b_blk, b_blk16 = _expand(bb_re, bb_im, im_sign=1.0)
    c_blk, c_blk16 = _expand(c_re, c_im, im_sign=-1.0)
    gw = glu_w.reshape(nb, gpb, SSM_GROUP, SSM_GROUP)
    gw_blk = jnp.einsum('bgce,gh->bgche', gw, eye).reshape(nb, LANES, LANES)
    ab_row = jnp.stack([ab_re.reshape(nb, STATE_LANES), ab_im.reshape(nb, STATE_LANES)], axis=1)
    dg_row = jnp.stack([d.reshape(nb, LANES), glu_b.reshape(nb, LANES)], axis=1)
    return (b_blk, c_blk, gw_blk, ab_row, dg_row), (b_blk16, c_blk16)


def _pair_weights(b_blk16, c_blk16, gw_blk, ab_row, dg_row, nbatch):
    npair = N_GBLOCKS // 2
    n_slab = STATE_LANES // LANES
    b_st = b_blk16.reshape(npair, 2 * LANES, 2 * STATE_LANES)
    c_st = c_blk16.reshape(npair, 2 * LANES, 2 * STATE_LANES)
    gw_st = gw_blk.reshape(npair, 2, LANES, LANES).transpose(0, 2, 1, 3).reshape(
        npair, LANES, 2 * LANES).astype(BF16)
    ab = ab_row.reshape(npair, 2, 2, n_slab, LANES).transpose(0, 2, 3, 1, 4)
    ab_tm = jnp.repeat(ab, nbatch, axis=3).reshape(npair, 2 * n_slab, 2 * nbatch, LANES)
    dg = dg_row.reshape(npair, 2, 2, LANES).transpose(0, 2, 1, 3)
    dg_tm = jnp.repeat(dg, nbatch, axis=2)
    return b_st, c_st, gw_st, ab_tm, dg_tm


def _dense_tail_prompt(x2d, ssm, att, w_out, norm_mlp_w, w_up, w_down, norm_final_w):
    x1 = _out_proj(ssm, att, w_out, x2d)
    hid = _matmul(x1, w_up, name="mlp_up", bm=512, bn=1024, bk=D_MODEL, norm_w=norm_mlp_w, epilogue="relu2",
                  out_dtype=BF16)
    x2 = _matmul(hid, w_down, name="mlp_down", bm=1024, bn=1024, bk=4096, res=x1, epilogue="res")
    return _rmsnorm(x2, norm_final_w)


def _dense_tail_sample(x2d, mixed, w_out, norm_mlp_w, w_up, w_down, norm_final_w):
    x1 = _matmul(mixed, w_out, name="out_proj_s", bm=LANES, bn=1024, bk=D_MODEL, res=x2d, epilogue="res")
    hid = _matmul(x1, w_up, name="mlp_up_s", bm=LANES, bn=1024, bk=D_MODEL, norm_w=norm_mlp_w, epilogue="relu2",
                  out_dtype=BF16)
    x2 = _matmul(hid, w_down, name="mlp_down_s", bm=LANES, bn=1024, bk=4096, res=x1, epilogue="res")
    return _rmsnorm(x2, norm_final_w)


def kernel(x_prompt, x_sample, cache_k, cache_v, state_ssm_re, state_ssm_im, page_table, norm_mix_w, w_in, sb_bias,
           ssm_a_re, ssm_a_im, ssm_b_re, ssm_b_im, ssm_c_re, ssm_c_im, ssm_d, ssm_log_dt, glu_w, glu_b, w_out,
           norm_mlp_w, w_up, w_down, norm_final_w):
    depth = w_in.shape[0]
    assert depth == 1
    nb_p, t_p, _ = x_prompt.shape
    nb_s, t_s, _ = x_sample.shape
    assert nb_p * 2 == SUBLANES
    assert nb_s * t_s == LANES
    l = 0

    ab_re, ab_im, bb_re, bb_im = _ssm_params(ssm_a_re[l], ssm_a_im[l], ssm_log_dt[l],
                                             ssm_b_re[l].transpose(0, 2, 1), ssm_b_im[l].transpose(0, 2, 1))
    blk, blk16 = _ssm_block_weights(ab_re, ab_im, bb_re, bb_im, ssm_c_re[l], ssm_c_im[l], ssm_d[l], glu_w[l],
                                    glu_b[l])
    pair = _pair_weights(*blk16, *blk[2:], nbatch=nb_p)

    xs = x_sample.reshape(nb_s * t_s, D_MODEL)
    proj_s, w_in_b = _matmul(xs, w_in[l], name="in_proj_s", bm=LANES, bn=512, bk=D_MODEL, norm_w=norm_mix_w[l],
                             cast_w=True)
    proj_s3 = proj_s.reshape(nb_s, t_s, IN_WIDTH)
    n_phys = cache_k.shape[1]
    pool = (depth * n_phys, PAGE_SIZE * N_HEADS, HEAD_DIM)
    att_s = _attn_sample(proj_s3, cache_k.reshape(pool), cache_v.reshape(pool), page_table + l * n_phys,
                         sb_bias[l])
    u_t = proj_s3[:, :, 4 * ATT_WIDTH:4 * ATT_WIDTH + SSM_WIDTH].transpose(1, 0, 2)
    g_t = proj_s3[:, :, 4 * ATT_WIDTH + SSM_WIDTH:].transpose(1, 0, 2)
    ssm_t, hs_re, hs_im = _ssm_sample(u_t, g_t, state_ssm_re[l].reshape(nb_s, N_GROUPS * STATE_DIM),
                                      state_ssm_im[l].reshape(nb_s, N_GROUPS * STATE_DIM), *blk)
    ssm_s = ssm_t.transpose(1, 0, 2).reshape(nb_s * t_s, SSM_WIDTH)
    mixed_s = jnp.concatenate([ssm_s, att_s.reshape(nb_s * t_s, ATT_WIDTH)], axis=1).astype(BF16)

    xp = x_prompt.reshape(nb_p * t_p, D_MODEL)
    proj_p = _matmul(xp, w_in_b, name="in_proj", bm=512, bn=1024, bk=D_MODEL, norm_w=norm_mix_w[l])
    proj_p3 = proj_p.reshape(nb_p, t_p, IN_WIDTH)
    att_p, k_p, v_p, w_out_b, w_up_b = _attn_prompt(proj_p3, sb_bias[l], (w_out[l], w_up[l]))
    npair = N_GBLOCKS // 2
    ssm_p, hT, w_down_b = _ssm_prompt(proj_p3, *pair, (w_down[l],))
    y_s = _dense_tail_sample(xs, mixed_s, w_out_b, norm_mlp_w[l], w_up_b, w_down_b, norm_final_w)
    n_slab = STATE_LANES // LANES
    hT = hT.reshape(npair, 2, n_slab, 2, nb_p, 2, STATE_DIM).transpose(1, 4, 0, 3, 2, 5, 6).reshape(
        2, nb_p, N_GROUPS, STATE_DIM)
    y_p = _dense_tail_prompt(xp, ssm_p.reshape(nb_p * t_p, SSM_WIDTH), att_p.reshape(nb_p * t_p, ATT_WIDTH),
                             w_out_b, norm_mlp_w[l], w_up_b, w_down_b, norm_final_w)

    kv_shape_p = (1, nb_p, t_p, N_HEADS, HEAD_DIM)
    kv_shape_s = (1, nb_s, t_s, N_HEADS, HEAD_DIM)
    return (y_p.reshape(nb_p, t_p, D_MODEL), y_s.reshape(nb_s, t_s, D_MODEL),
            k_p.reshape(kv_shape_p), v_p.reshape(kv_shape_p),
            hT[0][None], hT[1][None],
            proj_s3[:, :, ATT_WIDTH:2 * ATT_WIDTH].reshape(kv_shape_s),
            proj_s3[:, :, 2 * ATT_WIDTH:3 * ATT_WIDTH].reshape(kv_shape_s),
            hs_re.reshape(1, nb_s, N_GROUPS, STATE_DIM), hs_im.reshape(1, nb_s, N_GROUPS, STATE_DIM))
```

```python
import functools
import math

import jax
import jax.numpy as jnp
from jax import lax
from jax.experimental import pallas as pl
from jax.experimental.pallas import tpu as pltpu

F32 = jnp.float32
BF16 = jnp.bfloat16

D_MODEL = 4096
HEAD_DIM = 128
N_HEADS = 16
ATT_WIDTH = N_HEADS * HEAD_DIM
SSM_WIDTH = 2048
SSM_GROUP = 16
N_GROUPS = 128
STATE_DIM = 64
IN_WIDTH = 4 * ATT_WIDTH + 2 * SSM_WIDTH
D_FF = 4 * D_MODEL
PAGE_SIZE = 128
EPS = 1e-6

LANES = 128
SUBLANES = 8
HEAD_HALVES = N_HEADS // SUBLANES
GROUPS_PER_BLOCK = LANES // SSM_GROUP
N_GBLOCKS = N_GROUPS // GROUPS_PER_BLOCK
STATE_LANES = GROUPS_PER_BLOCK * STATE_DIM
VMEM_LIMIT = 62 * 1024 * 1024


def _cparams(sem):
    return pltpu.CompilerParams(dimension_semantics=sem, vmem_limit_bytes=VMEM_LIMIT)


def _rms_rows(x, w):
    return x * lax.rsqrt(jnp.mean(x * x, axis=-1, keepdims=True) + EPS) * w


def _mm_body(*refs, norm, epilogue, nk, cast_w):
    it = iter(refs)
    x_ref = next(it)
    nw_ref = next(it) if norm else None
    w_ref = next(it)
    res_ref = next(it) if epilogue == "res" else None
    o_ref = next(it)
    wb_ref = next(it) if cast_w else None
    xs_ref = next(it) if norm else None
    acc_ref = next(it) if nk > 1 else None
    j = pl.program_id(1)
    k = pl.program_id(2)

    if norm:
        @pl.when(j == 0)
        def _():
            xs_ref[...] = _rms_rows(x_ref[...], nw_ref[...]).astype(BF16)
        lhs = xs_ref[...]
    else:
        lhs = x_ref[...]
    w = w_ref[...]
    if cast_w:
        w = w.astype(BF16)
        wb_ref[...] = w
    part = jnp.dot(lhs, w, preferred_element_type=F32)

    def finish(acc):
        if epilogue == "relu2":
            r = jnp.maximum(acc, 0.0)
            acc = r * r
        elif epilogue == "res":
            acc = acc + res_ref[...]
        o_ref[...] = acc.astype(o_ref.dtype)

    if nk == 1:
        finish(part)
    else:
        @pl.when(k == 0)
        def _():
            acc_ref[...] = part

        @pl.when(k > 0)
        def _():
            acc_ref[...] += part

        @pl.when(k == nk - 1)
        def _():
            finish(acc_ref[...])


def _matmul(x, w, *, name, bm, bn, bk, norm_w=None, res=None, epilogue="none", out_dtype=F32, cast_w=False):
    m, kdim = x.shape
    n = w.shape[1]
    bm = min(bm, m)
    nk = kdim // bk
    norm = norm_w is not None
    assert not (norm and nk != 1)
    assert not (cast_w and m != bm)
    in_specs = [pl.BlockSpec((bm, bk), lambda i, j, k: (i, k))]
    args = [x]
    if norm:
        in_specs.append(pl.BlockSpec((1, kdim), lambda i, j, k: (0, 0)))
        args.append(norm_w.reshape(1, kdim))
    in_specs.append(pl.BlockSpec((bk, bn), lambda i, j, k: (k, j)))
    args.append(w)
    if epilogue == "res":
        in_specs.append(pl.BlockSpec((bm, bn), lambda i, j, k: (i, j)))
        args.append(res)
    scratch = []
    if norm:
        scratch.append(pltpu.VMEM((bm, kdim), BF16))
    if nk > 1:
        scratch.append(pltpu.VMEM((bm, bn), F32))
    out_specs = pl.BlockSpec((bm, bn), lambda i, j, k: (i, j))
    out_shape = jax.ShapeDtypeStruct((m, n), out_dtype)
    if cast_w:
        out_specs = [out_specs, pl.BlockSpec((bk, bn), lambda i, j, k: (k, j))]
        out_shape = [out_shape, jax.ShapeDtypeStruct((kdim, n), BF16)]
    return pl.pallas_call(
        functools.partial(_mm_body, norm=norm, epilogue=epilogue, nk=nk, cast_w=cast_w),
        grid=(m // bm, n // bn, nk),
        in_specs=in_specs,
        out_specs=out_specs,
        out_shape=out_shape,
        scratch_shapes=scratch,
        compiler_params=_cparams(("parallel", "arbitrary", "arbitrary")),
        name=name,
    )(*args)


def _out_proj_body(a_ref, b_ref, wa_ref, wb_ref, res_ref, o_ref):
    acc = jnp.dot(a_ref[...], wa_ref[...], preferred_element_type=F32)
    acc = acc + jnp.dot(b_ref[...], wb_ref[...], preferred_element_type=F32)
    o_ref[...] = acc + res_ref[...]


def _out_proj(a, b, w, res, *, bm=1024, bn=1024):
    m, ka = a.shape
    kb = b.shape[1]
    n = w.shape[1]
    bm = min(bm, m)
    assert ka == kb
    row_spec = lambda wid: pl.BlockSpec((bm, wid), lambda i, j: (i, 0))
    return pl.pallas_call(
        _out_proj_body,
        grid=(m // bm, n // bn),
        in_specs=[row_spec(ka), row_spec(kb),
                  pl.BlockSpec((ka, bn), lambda i, j: (0, j)), pl.BlockSpec((kb, bn), lambda i, j: (1, j)),
                  pl.BlockSpec((bm, bn), lambda i, j: (i, j))],
        out_specs=pl.BlockSpec((bm, bn), lambda i, j: (i, j)),
        out_shape=jax.ShapeDtypeStruct((m, n), F32),
        compiler_params=_cparams(("parallel", "arbitrary")),
        name="out_proj",
    )(a, b, w, w, res)


def _rmsnorm_body(x_ref, w_ref, o_ref):
    o_ref[...] = _rms_rows(x_ref[...], w_ref[...])


def _rmsnorm(x, w, *, bm=256):
    m, d = x.shape
    bm = min(bm, m)
    return pl.pallas_call(
        _rmsnorm_body,
        grid=(m // bm,),
        in_specs=[pl.BlockSpec((bm, d), lambda i: (i, 0)), pl.BlockSpec((1, d), lambda i: (0, 0))],
        out_specs=pl.BlockSpec((bm, d), lambda i: (i, 0)),
        out_shape=jax.ShapeDtypeStruct((m, d), F32),
        compiler_params=_cparams(("parallel",)),
        name="rmsnorm",
    )(x, w.reshape(1, d))


LOG2E = 1.0 / math.log(2.0)
ATT_SCALE2 = HEAD_DIM ** -0.5 * LOG2E


def _softplus2(z2):
    return jnp.maximum(z2, 0.0) + jnp.log(1.0 + jnp.exp2(-jnp.abs(z2))) * LOG2E


def _ride_along_specs(ws, n_steps, step_index):
    specs = [pl.BlockSpec((w.shape[0] // n_steps, w.shape[1]), lambda *ids: (step_index(*ids), 0)) for w in ws]
    for w in ws:
        assert w.shape[0] % (n_steps * 2 * SUBLANES) == 0
    return specs, [jax.ShapeDtypeStruct(w.shape, BF16) for w in ws]


def _ride_along_casts(src_refs, dst_refs):
    for src, dst in zip(src_refs, dst_refs):
        dst[...] = src[...].astype(BF16)


def _attn_prompt_body(bias_ref, q_ref, k_ref, v_ref, g_ref, *refs, tk, nsub, nhead, n_cast):
    cast_src, (o_ref, ko_ref, vo_ref), refs = refs[:n_cast], refs[n_cast:n_cast + 3], refs[n_cast + 3:]
    cast_dst, (kb_ref, vb_ref), state_refs = refs[:n_cast], refs[n_cast:n_cast + 2], refs[n_cast + 2:]
    _ride_along_casts(cast_src, cast_dst)
    acc_refs, carry_refs = state_refs[:nhead], state_refs[nhead:]
    hg = pl.program_id(1)
    qi = pl.program_id(2)

    @pl.when(qi == 0)
    def _():
        k = k_ref[...]
        v = v_ref[...]
        ko_ref[...] = k
        vo_ref[...] = v
        kb_ref[...] = k.astype(BF16)
        vb_ref[...] = v.astype(BF16)

    tq = tk * nsub
    row = lax.broadcasted_iota(jnp.int32, (tk, tk), 0)
    col = lax.broadcasted_iota(jnp.int32, (tk, tk), 1)
    later = (row > col).astype(BF16)
    lanes = lambda h: slice(h * HEAD_DIM, (h + 1) * HEAD_DIM)
    q = [(q_ref[:, lanes(h)] * ATT_SCALE2).astype(BF16) for h in range(nhead)]
    bias = [bias_ref[hg * nhead + h] * LOG2E for h in range(nhead)]

    for ref in state_refs:
        ref[...] = jnp.zeros_like(ref)

    def block(h, lo, kb, valid):
        start = pl.multiple_of(kb * tk, tk)
        kblk = kb_ref[pl.ds(start, tk), lanes(h)]
        vblk = vb_ref[pl.ds(start, tk), lanes(h)]
        z = lax.dot_general(q[h][lo:], kblk, (((1,), (1,)), ((), ())), preferred_element_type=F32) + bias[h]
        sp = _softplus2(z)
        if valid is not None:
            sp = jnp.where(valid, sp, 0.0)
        inner = jnp.dot(sp.astype(BF16), later, preferred_element_type=F32)
        carry = carry_refs[h][lo:, :]
        w = jnp.exp2(z - sp - inner - carry)
        if valid is not None:
            w = jnp.where(valid, w, 0.0)
        acc_refs[h][lo:, :] += jnp.dot(w.astype(BF16), vblk, preferred_element_type=F32)
        carry_refs[h][lo:, :] = carry + inner[:, :1] + sp[:, :1]

    for j in reversed(range(nsub)):
        m = tq - j * tk
        ri = lax.broadcasted_iota(jnp.int32, (m, tk), 0)
        ci = lax.broadcasted_iota(jnp.int32, (m, tk), 1)
        valid = (ri >= tk) | (ci < ri)
        for h in range(nhead):
            block(h, j * tk, nsub * qi + j, valid)

    @pl.loop(0, qi)
    def _(it):
        for j in range(nsub):
            for h in range(nhead):
                block(h, 0, nsub * (qi - it) - 1 - j, None)

    for h in range(nhead):
        o_ref[:, lanes(h)] = (acc_refs[h][...] * jax.nn.sigmoid(g_ref[:, lanes(h)])).astype(o_ref.dtype)


def _attn_prompt(proj, sb_bias, cast_ws=(), *, tk=256, nsub=4, nhead=2):
    b, t, _ = proj.shape
    tq = tk * nsub
    assert t % tq == 0 and N_HEADS % nhead == 0
    wid = nhead * HEAD_DIM
    ngrp = N_HEADS // nhead
    nq = t // tq
    qspec = lambda off: pl.BlockSpec((None, tq, wid), lambda bi, h, qi: (bi, qi, off + h))
    kvspec = lambda off: pl.BlockSpec((None, t, wid), lambda bi, h, qi: (bi, 0, off + h))
    cast_specs, cast_shapes = _ride_along_specs(cast_ws, b * ngrp * nq, lambda bi, h, qi: (bi * ngrp + h) * nq + qi)
    return pl.pallas_call(
        functools.partial(_attn_prompt_body, tk=tk, nsub=nsub, nhead=nhead, n_cast=len(cast_ws)),
        grid=(b, ngrp, nq),
        in_specs=[pl.BlockSpec(memory_space=pltpu.SMEM),
                  qspec(0), kvspec(ngrp), kvspec(2 * ngrp), qspec(3 * ngrp), *cast_specs],
        out_specs=[qspec(0), kvspec(0), kvspec(0), *cast_specs],
        out_shape=[jax.ShapeDtypeStruct((b, t, ATT_WIDTH), BF16),
                   jax.ShapeDtypeStruct((b, t, ATT_WIDTH), F32),
                   jax.ShapeDtypeStruct((b, t, ATT_WIDTH), F32), *cast_shapes],
        scratch_shapes=[pltpu.VMEM((t, wid), BF16), pltpu.VMEM((t, wid), BF16),
                        *[pltpu.VMEM((tq, HEAD_DIM), F32)] * nhead, *[pltpu.VMEM((tq, 1), F32)] * nhead],
        compiler_params=_cparams(("parallel", "parallel", "arbitrary")),
        name="attn_prompt",
    )(sb_bias, proj, proj, proj, proj, *cast_ws)


SLOTS = PAGE_SIZE * HEAD_HALVES
HALF_WIDTH = SUBLANES * HEAD_DIM


def _attn_sample_body(pt_ref, q_ref, kn_ref, vn_ref, g_ref, bias_ref, *refs, pages_per_step, n_tok):
    kp_refs = refs[:pages_per_step]
    vp_refs = refs[pages_per_step:2 * pages_per_step]
    o_ref = refs[2 * pages_per_step]
    qbd_ref, kcat_ref, vcat_ref, acc_ref, carry_ref = refs[2 * pages_per_step + 1:]
    c = pl.program_id(1)
    n_rows = n_tok * N_HEADS
    bias = bias_ref[...] * LOG2E
    row = lax.broadcasted_iota(jnp.int32, (SLOTS, SLOTS), 0)
    col = lax.broadcasted_iota(jnp.int32, (SLOTS, SLOTS), 1)
    later = ((row // HEAD_HALVES) > (col // HEAD_HALVES)).astype(BF16)

    def attend(npages, causal):
        n = npages * SLOTS
        z = lax.dot_general(qbd_ref[...], kcat_ref[:n, :], (((1,), (1,)), ((), ())),
                            preferred_element_type=F32) + bias
        rhalf = (lax.broadcasted_iota(jnp.int32, (n_rows, n), 0) % N_HEADS) // SUBLANES
        slot = lax.broadcasted_iota(jnp.int32, (n_rows, n), 1)
        valid = rhalf == slot % HEAD_HALVES
        if causal:
            valid = valid & (slot // HEAD_HALVES < lax.broadcasted_iota(jnp.int32, (n_rows, n), 0) // N_HEADS)
        sp = jnp.where(valid, _softplus2(z), 0.0)
        pages = lambda a: [a[:, i * SLOTS:(i + 1) * SLOTS] for i in range(npages)]
        stacked = jnp.concatenate(pages(sp), axis=0).astype(BF16)
        inner = jnp.dot(stacked, later, preferred_element_type=F32)
        inner = jnp.concatenate([inner[i * n_rows:(i + 1) * n_rows] for i in range(npages)], axis=1)
        carry = carry_ref[...]
        carries = []
        for i in range(npages):
            carries.append(jnp.broadcast_to(carry, (n_rows, SLOTS)))
            f = i * SLOTS
            carry = carry + inner[:, f:f + 1] + sp[:, f:f + 1] + sp[:, f + 1:f + 2]
        carry_ref[...] = carry
        w = jnp.where(valid, jnp.exp2(z - sp - inner - jnp.concatenate(carries, axis=1)), 0.0)
        acc_ref[...] += jnp.dot(w.astype(BF16), vcat_ref[:n, :], preferred_element_type=F32)

    @pl.when(c == 0)
    def _():
        q = q_ref[...] * ATT_SCALE2
        qt = jnp.concatenate([q] * SUBLANES, axis=1)
        rh = lax.broadcasted_iota(jnp.int32, (n_rows, HALF_WIDTH), 0) % SUBLANES
        lh = lax.broadcasted_iota(jnp.int32, (n_rows, HALF_WIDTH), 1) // HEAD_DIM
        qbd_ref[...] = jnp.where(rh == lh, qt, 0.0).astype(BF16)
        acc_ref[...] = jnp.zeros_like(acc_ref)
        carry_ref[...] = jnp.zeros_like(carry_ref)
        n_new = kn_ref.shape[0]
        kcat_ref[:n_new, :] = kn_ref[...].astype(BF16)
        vcat_ref[:n_new, :] = vn_ref[...].astype(BF16)
        kcat_ref[n_new:SLOTS, :] = jnp.zeros((SLOTS - n_new, HALF_WIDTH), BF16)
        vcat_ref[n_new:SLOTS, :] = jnp.zeros((SLOTS - n_new, HALF_WIDTH), BF16)
        attend(1, True)

    for i in range(pages_per_step):
        for h8 in range(SUBLANES):
            dst = (slice(i * SLOTS, (i + 1) * SLOTS), slice(h8 * HEAD_DIM, (h8 + 1) * HEAD_DIM))
            kcat_ref[dst] = kp_refs[i][pl.ds(h8, SLOTS, stride=SUBLANES), :].astype(BF16)
            vcat_ref[dst] = vp_refs[i][pl.ds(h8, SLOTS, stride=SUBLANES), :].astype(BF16)
    attend(pages_per_step, False)

    @pl.when(c == pl.num_programs(1) - 1)
    def _():
        rh = lax.broadcasted_iota(jnp.int32, (n_rows, HEAD_DIM), 0) % SUBLANES
        out = jnp.zeros((n_rows, HEAD_DIM), F32)
        for h8 in range(SUBLANES):
            out = out + jnp.where(rh == h8, acc_ref[:, h8 * HEAD_DIM:(h8 + 1) * HEAD_DIM], 0.0)
        o_ref[...] = out * jax.nn.sigmoid(g_ref[...])


def _attn_sample(proj_s, n_tok, cache_k, cache_v, page_table, sb_bias, *, pages_per_step=8):
    b = proj_s.shape[0] // n_tok
    n_pages = page_table.shape[1]
    n_rows = n_tok * N_HEADS
    assert n_pages % pages_per_step == 0
    steps = n_pages // pages_per_step
    q_rows = proj_s[:, :ATT_WIDTH].reshape(b, n_rows, HEAD_DIM)
    g_rows = proj_s[:, 3 * ATT_WIDTH:4 * ATT_WIDTH].reshape(b, n_rows, HEAD_DIM)
    bias_col = jnp.tile(sb_bias, n_tok).reshape(n_rows, 1)
    n_new = 2 * SUBLANES
    assert n_tok * HEAD_HALVES <= n_new

    def new_slots(a):
        a = a.reshape(b, n_tok * HEAD_HALVES, HALF_WIDTH)
        return jnp.pad(a, ((0, 0), (0, n_new - n_tok * HEAD_HALVES), (0, 0)))

    k_new = new_slots(proj_s[:, ATT_WIDTH:2 * ATT_WIDTH])
    v_new = new_slots(proj_s[:, 2 * ATT_WIDTH:3 * ATT_WIDTH])

    def page_spec(i):
        def imap(bi, c, pt):
            return (pt[bi, n_pages - 1 - (c * pages_per_step + i)], 0, 0)
        return pl.BlockSpec((None, PAGE_SIZE * N_HEADS, HEAD_DIM), imap)

    page_specs = [page_spec(i) for i in range(pages_per_step)]
    rows_spec = pl.BlockSpec((None, n_rows, HEAD_DIM), lambda bi, c, pt: (bi, 0, 0))
    new_spec = pl.BlockSpec((None, n_new, HALF_WIDTH), lambda bi, c, pt: (bi, 0, 0))
    grid_spec = pltpu.PrefetchScalarGridSpec(
        num_scalar_prefetch=1,
        grid=(b, steps),
        in_specs=[rows_spec, new_spec, new_spec, rows_spec,
                  pl.BlockSpec((n_rows, 1), lambda bi, c, pt: (0, 0))]
                 + page_specs * 2,
        out_specs=rows_spec,
        scratch_shapes=[pltpu.VMEM((n_rows, HALF_WIDTH), BF16),
                        pltpu.VMEM((pages_per_step * SLOTS, HALF_WIDTH), BF16),
                        pltpu.VMEM((pages_per_step * SLOTS, HALF_WIDTH), BF16),
                        pltpu.VMEM((n_rows, HALF_WIDTH), F32),
                        pltpu.VMEM((n_rows, 1), F32)],
    )
    out = pl.pallas_call(
        functools.partial(_attn_sample_body, pages_per_step=pages_per_step, n_tok=n_tok),
        grid_spec=grid_spec,
        out_shape=jax.ShapeDtypeStruct((b, n_rows, HEAD_DIM), F32),
        compiler_params=_cparams(("parallel", "arbitrary")),
        name="attn_sample",
    )(page_table, q_rows, k_new, v_new, g_rows, bias_col,
      *([cache_k] * pages_per_step), *([cache_v] * pages_per_step))
    return out.reshape(b * n_tok, ATT_WIDTH)


def _ssm_params_body(are_ref, aim_ref, ldt_ref, bre_ref, bim_ref, abr_ref, abi_ref, bbr_ref, bbi_ref):
    a_re = are_ref[...]
    a_im = aim_ref[...]
    dt = jnp.exp(ldt_ref[...])
    mag = jnp.exp(dt * a_re)
    ab_re = mag * jnp.cos(dt * a_im)
    ab_im = mag * jnp.sin(dt * a_im)
    abr_ref[...] = ab_re
    abi_ref[...] = ab_im
    den = a_re * a_re + a_im * a_im
    n_re = ab_re - 1.0
    co_re = (n_re * a_re + ab_im * a_im) / den
    co_im = (ab_im * a_re - n_re * a_im) / den
    b_re = bre_ref[...]
    b_im = bim_ref[...]
    bbr_ref[...] = co_re[:, None, :] * b_re - co_im[:, None, :] * b_im
    bbi_ref[...] = co_re[:, None, :] * b_im + co_im[:, None, :] * b_re


def _ssm_params(a_re, a_im, log_dt, b_re, b_im):
    g, p = a_re.shape
    c = b_re.shape[1]
    return pl.pallas_call(
        _ssm_params_body,
        name="ssm_params",
        out_shape=[jax.ShapeDtypeStruct((g, p), F32), jax.ShapeDtypeStruct((g, p), F32),
                   jax.ShapeDtypeStruct((g, c, p), F32), jax.ShapeDtypeStruct((g, c, p), F32)],
    )(a_re, a_im, log_dt.reshape(g, 1), b_re, b_im)


def _gelu_tanh(y):
    return 0.5 * y * (1.0 + jnp.tanh(math.sqrt(2.0 / math.pi) * (y + 0.044715 * (y * y * y))))


def _ssm_prompt_body(u_ref, g_ref, b_ref, c_ref, gw_ref, ab_ref, dg_ref, *refs, lc, nb, nsc, n_cast):
    cast_src, (o_ref, hT_ref), refs = refs[:n_cast], refs[n_cast:n_cast + 2], refs[n_cast + 2:]
    cast_dst, (st_ref, tm_ref), sub_refs = refs[:n_cast], refs[n_cast:n_cast + 2], refs[n_cast + 2:]
    _ride_along_casts(cast_src, cast_dst)
    ci = pl.program_id(1)
    rows = lc * SUBLANES
    n_slab = STATE_LANES // LANES
    seqs = [(sel, b) for sel in range(2) for b in range(nb)]
    seq_rows = lambda i: pl.ds(i, lc, stride=SUBLANES)
    for i, (sel, b) in enumerate(seqs):
        tm_ref.at[0][seq_rows(i), :] = u_ref[b, :, sel * LANES:(sel + 1) * LANES]
        tm_ref.at[1][seq_rows(i), :] = g_ref[b, :, sel * LANES:(sel + 1) * LANES]

    @pl.when(ci == 0)
    def _():
        st_ref[...] = jnp.zeros_like(st_ref)

    ar = [ab_ref[j] for j in range(n_slab)]
    ai = [ab_ref[n_slab + j] for j in range(n_slab)]
    h = [st_ref[j] for j in range(2 * n_slab)]
    sub = rows // nsc
    first = (lax.broadcasted_iota(jnp.int32, (sub, LANES), 0) % SUBLANES) < (SUBLANES // 2)
    d = jnp.broadcast_to(dg_ref[0][None], (sub // SUBLANES, SUBLANES, LANES)).reshape(sub, LANES)
    gb = jnp.broadcast_to(dg_ref[1][None], (sub // SUBLANES, SUBLANES, LANES)).reshape(sub, LANES)
    bu_refs, hh_refs = sub_refs[:nsc], sub_refs[nsc:]

    def drive(sc):
        u = tm_ref[0, sc * sub:(sc + 1) * sub, :]
        lhs = jnp.concatenate([jnp.where(first, u, 0.0), jnp.where(first, 0.0, u)], axis=1).astype(BF16)
        bu_refs[sc][...] = jnp.dot(lhs, b_ref[...], preferred_element_type=F32)

    drive(0)
    for sc in range(nsc):
        lo = sc * sub
        if sc + 1 < nsc:
            drive(sc + 1)
        bu_ref, hh_ref = bu_refs[sc], hh_refs[sc]
        for r0 in range(0, sub, SUBLANES):
            for j in range(n_slab):
                re_l = slice(j * LANES, (j + 1) * LANES)
                im_l = slice(STATE_LANES + j * LANES, STATE_LANES + (j + 1) * LANES)
                hr, hi = h[j], h[n_slab + j]
                h[j] = ar[j] * hr - ai[j] * hi + bu_ref[r0:r0 + SUBLANES, re_l]
                h[n_slab + j] = ar[j] * hi + ai[j] * hr + bu_ref[r0:r0 + SUBLANES, im_l]
                hh_ref[r0:r0 + SUBLANES, re_l] = h[j]
                hh_ref[r0:r0 + SUBLANES, im_l] = h[n_slab + j]
        u = tm_ref[0, lo:lo + sub, :]
        y2 = lax.dot_general(hh_ref[...].astype(BF16), c_ref[...], (((1,), (1,)), ((), ())),
                             preferred_element_type=F32)
        y = jnp.where(first, y2[:, :LANES], y2[:, LANES:]) + d * u
        z = _gelu_tanh(y)
        g2 = lax.dot_general(z.astype(BF16), gw_ref[...], (((1,), (1,)), ((), ())), preferred_element_type=F32)
        gate = jax.nn.sigmoid(jnp.where(first, g2[:, :LANES], g2[:, LANES:]) + gb)
        tm_ref[2, lo:lo + sub, :] = z * gate * jax.nn.sigmoid(tm_ref[1, lo:lo + sub, :])
    for j in range(2 * n_slab):
        st_ref[j] = h[j]
        hT_ref[j] = h[j]
    for i, (sel, b) in enumerate(seqs):
        o_ref[b, :, sel * LANES:(sel + 1) * LANES] = tm_ref.at[2][seq_rows(i), :].astype(o_ref.dtype)


def _ssm_prompt(proj, b_st, c_st, gw_st, ab_tm, dg_tm, cast_ws=(), *, lc=256, nsc=4):
    nb, t, _ = proj.shape
    assert t % lc == 0 and (lc * SUBLANES) % (nsc * SUBLANES) == 0
    npair = N_GBLOCKS // 2
    n_slab2 = 2 * STATE_LANES // LANES
    wid = 2 * LANES
    nchunk = t // lc
    seq_spec = lambda off: pl.BlockSpec((nb, lc, wid), lambda p, c: (0, c, off + p))
    u_off = 4 * ATT_WIDTH // wid
    par_spec = lambda shp: pl.BlockSpec((None,) + shp, lambda p, c: (p,) + (0,) * len(shp))
    cast_specs, cast_shapes = _ride_along_specs(cast_ws, npair * nchunk, lambda p, c: p * nchunk + c)
    return pl.pallas_call(
        functools.partial(_ssm_prompt_body, lc=lc, nb=nb, nsc=nsc, n_cast=len(cast_ws)),
        grid=(npair, nchunk),
        in_specs=[seq_spec(u_off), seq_spec(u_off + npair),
                  par_spec((2 * LANES, 2 * STATE_LANES)), par_spec((2 * LANES, 2 * STATE_LANES)),
                  par_spec((2 * LANES, LANES)), par_spec((n_slab2, SUBLANES, LANES)),
                  par_spec((2, SUBLANES, LANES)), *cast_specs],
        out_specs=[seq_spec(0), par_spec((n_slab2, SUBLANES, LANES)), *cast_specs],
        out_shape=[jax.ShapeDtypeStruct((nb, t, SSM_WIDTH), BF16),
                   jax.ShapeDtypeStruct((npair, n_slab2, SUBLANES, LANES), F32), *cast_shapes],
        scratch_shapes=[pltpu.VMEM((n_slab2, SUBLANES, LANES), F32),
                        pltpu.VMEM((3, lc * SUBLANES, LANES), F32)]
                       + [pltpu.VMEM((lc * SUBLANES // nsc, 2 * STATE_LANES), F32)] * (2 * nsc),
        compiler_params=_cparams(("parallel", "arbitrary")),
        name="ssm_prompt",
    )(proj, proj, b_st, c_st, gw_st, ab_tm, dg_tm, *cast_ws)


def _ssm_sample_body(u_ref, g_ref, hre_ref, him_ref, b_ref, c_ref, gw_ref, ab_ref, dg_ref,
                     o_ref, ore_ref, oim_ref, *, n_tok, nb):
    hp = lax.Precision.HIGHEST
    u = u_ref[...].reshape(n_tok * nb, LANES)
    bu = jnp.dot(u, b_ref[...], preferred_element_type=F32, precision=hp)
    ar = ab_ref[0:1, :]
    ai = ab_ref[1:2, :]
    hr = hre_ref[...]
    hi = him_ref[...]
    hs = []
    for t in range(n_tok):
        bre = bu[t * nb:(t + 1) * nb, :STATE_LANES]
        bim = bu[t * nb:(t + 1) * nb, STATE_LANES:]
        hr, hi = ar * hr - ai * hi + bre, ar * hi + ai * hr + bim
        hs.append(jnp.concatenate([hr, hi], axis=1))
    ore_ref[...] = hr
    oim_ref[...] = hi
    hh = jnp.concatenate(hs, axis=0)
    y = lax.dot_general(hh, c_ref[...], (((1,), (1,)), ((), ())), preferred_element_type=F32, precision=hp)
    y = y + dg_ref[0:1, :] * u
    z = _gelu_tanh(y)
    gate = jax.nn.sigmoid(lax.dot_general(z, gw_ref[...], (((1,), (1,)), ((), ())), preferred_element_type=F32,
                                          precision=hp) + dg_ref[1:2, :])
    out = z * gate * jax.nn.sigmoid(g_ref[...].reshape(n_tok * nb, LANES))
    o_ref[...] = out.reshape(n_tok, nb, LANES)


def _ssm_sample(u_t, g_t, h_re, h_im, b_blk, c_blk, gw_blk, ab_row, dg_row):
    n_tok, nb, _ = u_t.shape
    seq_spec = pl.BlockSpec((n_tok, nb, LANES), lambda gb: (0, 0, gb))
    st_spec = pl.BlockSpec((nb, STATE_LANES), lambda gb: (0, gb))
    par_spec = lambda shp: pl.BlockSpec((None,) + shp, lambda gb: (gb,) + (0,) * len(shp))
    return pl.pallas_call(
        functools.partial(_ssm_sample_body, n_tok=n_tok, nb=nb),
        grid=(N_GBLOCKS,),
        in_specs=[seq_spec, seq_spec, st_spec, st_spec,
                  par_spec((LANES, 2 * STATE_LANES)), par_spec((LANES, 2 * STATE_LANES)),
                  par_spec((LANES, LANES)), par_spec((2, STATE_LANES)), par_spec((2, LANES))],
        out_specs=[seq_spec, st_spec, st_spec],
        out_shape=[jax.ShapeDtypeStruct(u_t.shape, F32),
                   jax.ShapeDtypeStruct(h_re.shape, F32), jax.ShapeDtypeStruct(h_im.shape, F32)],
        compiler_params=_cparams(("parallel",)),
        name="ssm_sample",
    )(u_t, g_t, h_re, h_im, b_blk, c_blk, gw_blk, ab_row, dg_row)


def _expand_body(*refs, signs):
    in_refs, (o32_ref, o16_ref) = refs[:len(signs)], refs[len(signs):]
    rows, p = in_refs[0].shape
    width = GROUPS_PER_BLOCK * p
    src = lax.broadcasted_iota(jnp.int32, (p, width), 0)
    dst = lax.broadcasted_iota(jnp.int32, (p, width), 1)
    spread = (src == dst % p).astype(F32)
    own = (lax.broadcasted_iota(jnp.int32, (rows, width), 0) // SSM_GROUP
           == lax.broadcasted_iota(jnp.int32, (rows, width), 1) // p)
    parts = []
    for ref, sign in zip(in_refs, signs):
        full = jnp.dot(ref[...], spread, preferred_element_type=F32, precision=lax.Precision.HIGHEST)
        parts.append(jnp.where(own, sign * full, 0.0))
    out = parts[0] if len(parts) == 1 else jnp.concatenate(parts, axis=1)
    o32_ref[...] = out
    o16_ref[...] = out.astype(BF16)


def _expand(xs, signs):
    g, c, p = xs[0].shape
    width = len(xs) * GROUPS_PER_BLOCK * p
    in_spec = pl.BlockSpec((LANES, p), lambda i: (i, 0))
    out_spec = pl.BlockSpec((None, LANES, width), lambda i: (i, 0, 0))
    shape = (N_GBLOCKS, LANES, width)
    return pl.pallas_call(
        functools.partial(_expand_body, signs=signs),
        grid=(N_GBLOCKS,),
        in_specs=[in_spec] * len(xs),
        out_specs=[out_spec, out_spec],
        out_shape=[jax.ShapeDtypeStruct(shape, F32), jax.ShapeDtypeStruct(shape, BF16)],
        compiler_params=_cparams(("parallel",)),
        name="ssm_expand",
    )(*[x.reshape(g * c, p) for x in xs])


def _ssm_block_weights(ab_re, ab_im, bb_re, bb_im, c_re, c_im, d, glu_w, glu_b):
    nb = N_GBLOCKS
    b_blk, b_blk16 = _expand([bb_re, bb_im], (1.0, 1.0))
    c_blk, c_blk16 = _expand([c_re, c_im], (1.0, -1.0))
    gw_blk, gw_blk16 = _expand([glu_w.transpose(0, 2, 1)], (1.0,))
    ab_row = jnp.stack([ab_re.reshape(nb, STATE_LANES), ab_im.reshape(nb, STATE_LANES)], axis=1)
    dg_row = jnp.stack([d.reshape(nb, LANES), glu_b.reshape(nb, LANES)], axis=1)
    return (b_blk, c_blk, gw_blk, ab_row, dg_row), (b_blk16, c_blk16, gw_blk16)


def _pair_weights(b_blk16, c_blk16, gw_blk16, ab_row, dg_row, nbatch):
    npair = N_GBLOCKS // 2
    n_slab = STATE_LANES // LANES
    b_st = b_blk16.reshape(npair, 2 * LANES, 2 * STATE_LANES)
    c_st = c_blk16.reshape(npair, 2 * LANES, 2 * STATE_LANES)
    gw_st = gw_blk16.reshape(npair, 2 * LANES, LANES)
    ab = ab_row.reshape(npair, 2, 2, n_slab, LANES).transpose(0, 2, 3, 1, 4)
    ab_tm = jnp.repeat(ab, nbatch, axis=3).reshape(npair, 2 * n_slab, 2 * nbatch, LANES)
    dg = dg_row.reshape(npair, 2, 2, LANES).transpose(0, 2, 1, 3)
    dg_tm = jnp.repeat(dg, nbatch, axis=2)
    return b_st, c_st, gw_st, ab_tm, dg_tm


def _dense_tail_prompt(x2d, ssm, att, w_out, norm_mlp_w, w_up, w_down, norm_final_w):
    x1 = _out_proj(ssm, att, w_out, x2d)
    hid = _matmul(x1, w_up, name="mlp_up", bm=512, bn=1024, bk=D_MODEL, norm_w=norm_mlp_w, epilogue="relu2",
                  out_dtype=BF16)
    x2 = _matmul(hid, w_down, name="mlp_down", bm=1024, bn=1024, bk=4096, res=x1, epilogue="res")
    return _rmsnorm(x2, norm_final_w)


def _dense_tail_sample(x2d, mixed, w_out, norm_mlp_w, w_up, w_down, norm_final_w):
    x1 = _matmul(mixed, w_out, name="out_proj_s", bm=LANES, bn=1024, bk=D_MODEL, res=x2d, epilogue="res")
    hid = _matmul(x1, w_up, name="mlp_up_s", bm=LANES, bn=1024, bk=D_MODEL, norm_w=norm_mlp_w, epilogue="relu2",
                  out_dtype=BF16)
    x2 = _matmul(hid, w_down, name="mlp_down_s", bm=LANES, bn=1024, bk=4096, res=x1, epilogue="res")
    return _rmsnorm(x2, norm_final_w)


def kernel(x_prompt, x_sample, cache_k, cache_v, state_ssm_re, state_ssm_im, page_table, norm_mix_w, w_in, sb_bias,
           ssm_a_re, ssm_a_im, ssm_b_re, ssm_b_im, ssm_c_re, ssm_c_im, ssm_d, ssm_log_dt, glu_w, glu_b, w_out,
           norm_mlp_w, w_up, w_down, norm_final_w):
    depth = w_in.shape[0]
    assert depth == 1
    nb_p, t_p, _ = x_prompt.shape
    nb_s, t_s, _ = x_sample.shape
    assert nb_p * 2 == SUBLANES
    assert nb_s * t_s == LANES
    l = 0

    ab_re, ab_im, bb_re, bb_im = _ssm_params(ssm_a_re[l], ssm_a_im[l], ssm_log_dt[l],
                                             ssm_b_re[l].transpose(0, 2, 1), ssm_b_im[l].transpose(0, 2, 1))
    blk, blk16 = _ssm_block_weights(ab_re, ab_im, bb_re, bb_im, ssm_c_re[l], ssm_c_im[l], ssm_d[l], glu_w[l],
                                    glu_b[l])
    pair = _pair_weights(*blk16, *blk[3:], nbatch=nb_p)

    xs = x_sample.reshape(nb_s * t_s, D_MODEL)
    proj_s, w_in_b = _matmul(xs, w_in[l], name="in_proj_s", bm=LANES, bn=512, bk=D_MODEL, norm_w=norm_mix_w[l],
                             cast_w=True)
    n_phys = cache_k.shape[1]
    pool = (depth * n_phys, PAGE_SIZE * N_HEADS, HEAD_DIM)
    att_s = _attn_sample(proj_s, t_s, cache_k.reshape(pool), cache_v.reshape(pool), page_table + l * n_phys,
                         sb_bias[l])
    by_token = lambda cols: cols.reshape(nb_s, t_s, SSM_WIDTH).transpose(1, 0, 2)
    u_t = by_token(proj_s[:, 4 * ATT_WIDTH:4 * ATT_WIDTH + SSM_WIDTH])
    g_t = by_token(proj_s[:, 4 * ATT_WIDTH + SSM_WIDTH:])
    ssm_t, hs_re, hs_im = _ssm_sample(u_t, g_t, state_ssm_re[l].reshape(nb_s, N_GROUPS * STATE_DIM),
                                      state_ssm_im[l].reshape(nb_s, N_GROUPS * STATE_DIM), *blk)
    ssm_s = ssm_t.transpose(1, 0, 2).reshape(nb_s * t_s, SSM_WIDTH)
    mixed_s = jnp.concatenate([ssm_s, att_s], axis=1).astype(BF16)

    xp = x_prompt.reshape(nb_p * t_p, D_MODEL)
    proj_p = _matmul(xp, w_in_b, name="in_proj", bm=512, bn=1024, bk=D_MODEL, norm_w=norm_mix_w[l])
    proj_p3 = proj_p.reshape(nb_p, t_p, IN_WIDTH)
    att_p, k_p, v_p, w_out_b, w_up_b = _attn_prompt(proj_p3, sb_bias[l], (w_out[l], w_up[l]))
    npair = N_GBLOCKS // 2
    ssm_p, hT, w_down_b = _ssm_prompt(proj_p3, *pair, (w_down[l],))
    y_s = _dense_tail_sample(xs, mixed_s, w_out_b, norm_mlp_w[l], w_up_b, w_down_b, norm_final_w)
    n_slab = STATE_LANES // LANES
    hT = hT.reshape(npair, 2, n_slab, 2, nb_p, 2, STATE_DIM).transpose(1, 4, 0, 3, 2, 5, 6).reshape(
        2, nb_p, N_GROUPS, STATE_DIM)
    y_p = _dense_tail_prompt(xp, ssm_p.reshape(nb_p * t_p, SSM_WIDTH), att_p.reshape(nb_p * t_p, ATT_WIDTH),
                             w_out_b, norm_mlp_w[l], w_up_b, w_down_b, norm_final_w)

    kv_shape_p = (1, nb_p, t_p, N_HEADS, HEAD_DIM)
    kv_shape_s = (1, nb_s, t_s, N_HEADS, HEAD_DIM)
    return (y_p.reshape(nb_p, t_p, D_MODEL), y_s.reshape(nb_s, t_s, D_MODEL),
            k_p.reshape(kv_shape_p), v_p.reshape(kv_shape_p),
            hT[0][None], hT[1][None],
            proj_s[:, ATT_WIDTH:2 * ATT_WIDTH].reshape(kv_shape_s),
            proj_s[:, 2 * ATT_WIDTH:3 * ATT_WIDTH].reshape(kv_shape_s),
            hs_re.reshape(1, nb_s, N_GROUPS, STATE_DIM), hs_im.reshape(1, nb_s, N_GROUPS, STATE_DIM))
```

```python
import functools
import math

import jax
import jax.numpy as jnp
from jax import lax
from jax.experimental import pallas as pl
from jax.experimental.pallas import tpu as pltpu

F32 = jnp.float32
BF16 = jnp.bfloat16

D_MODEL = 4096
HEAD_DIM = 128
N_HEADS = 16
ATT_WIDTH = N_HEADS * HEAD_DIM
SSM_WIDTH = 2048
SSM_GROUP = 16
N_GROUPS = 128
STATE_DIM = 64
IN_WIDTH = 4 * ATT_WIDTH + 2 * SSM_WIDTH
D_FF = 4 * D_MODEL
PAGE_SIZE = 128
EPS = 1e-6

LANES = 128
SUBLANES = 8
HEAD_HALVES = N_HEADS // SUBLANES
GROUPS_PER_BLOCK = LANES // SSM_GROUP
N_GBLOCKS = N_GROUPS // GROUPS_PER_BLOCK
STATE_LANES = GROUPS_PER_BLOCK * STATE_DIM
VMEM_LIMIT = 62 * 1024 * 1024


def _cparams(sem):
    return pltpu.CompilerParams(dimension_semantics=sem, vmem_limit_bytes=VMEM_LIMIT)


def _rms_rows(x, w):
    return x * lax.rsqrt(jnp.mean(x * x, axis=-1, keepdims=True) + EPS) * w


def _mm_body(*refs, norm, epilogue, nk, cast_w, ride):
    has_res = epilogue == "res"
    it = iter(refs)
    x_ref = next(it)
    nw_ref = next(it) if norm else None
    w_ref = next(it)
    res_ref = next(it) if has_res else None
    x2_ref = next(it) if ride else None
    res2_ref = next(it) if ride and has_res else None
    o_ref = next(it)
    wb_ref = next(it) if cast_w else None
    o2_ref = next(it) if ride else None
    xs_ref = next(it) if norm else None
    acc_ref = next(it) if nk > 1 else None
    xs2_ref = next(it) if ride and norm else None
    acc2_ref = next(it) if ride and nk > 1 else None
    i = pl.program_id(0)
    j = pl.program_id(1)
    k = pl.program_id(2)

    w = w_ref[...]
    if cast_w:
        w = w.astype(BF16)
        wb_ref[...] = w

    def rows(x_ref, xs_ref, res_ref, o_ref, acc_ref):
        if norm:
            @pl.when(j == 0)
            def _():
                xs_ref[...] = _rms_rows(x_ref[...], nw_ref[...]).astype(BF16)
            lhs = xs_ref[...]
        else:
            lhs = x_ref[...]
        part = jnp.dot(lhs, w, preferred_element_type=F32)

        def finish(acc):
            if epilogue == "relu2":
                r = jnp.maximum(acc, 0.0)
                acc = r * r
            elif has_res:
                acc = acc + res_ref[...]
            o_ref[...] = acc.astype(o_ref.dtype)

        if nk == 1:
            finish(part)
        else:
            @pl.when(k == 0)
            def _():
                acc_ref[...] = part

            @pl.when(k > 0)
            def _():
                acc_ref[...] += part

            @pl.when(k == nk - 1)
            def _():
                finish(acc_ref[...])

    rows(x_ref, xs_ref, res_ref, o_ref, acc_ref)
    if ride:
        @pl.when(i == 0)
        def _():
            rows(x2_ref, xs2_ref, res2_ref, o2_ref, acc2_ref)


def _matmul(x, w, *, name, bm, bn, bk, norm_w=None, res=None, epilogue="none", out_dtype=F32, cast_w=False,
            x2=None, res2=None):
    m, kdim = x.shape
    n = w.shape[1]
    bm = min(bm, m)
    nk = kdim // bk
    nj = n // bn
    norm = norm_w is not None
    ride = x2 is not None
    has_res = epilogue == "res"
    assert not (norm and nk != 1)
    assert not (cast_w and m != bm)
    assert not (ride and cast_w)
    in_specs = [pl.BlockSpec((bm, bk), lambda i, j, k: (i, k))]
    args = [x]
    if norm:
        in_specs.append(pl.BlockSpec((1, kdim), lambda i, j, k: (0, 0)))
        args.append(norm_w.reshape(1, kdim))
    in_specs.append(pl.BlockSpec((bk, bn), lambda i, j, k: (k, j)))
    args.append(w)
    if has_res:
        in_specs.append(pl.BlockSpec((bm, bn), lambda i, j, k: (i, j)))
        args.append(res)
    scratch = []
    if norm:
        scratch.append(pltpu.VMEM((bm, kdim), BF16))
    if nk > 1:
        scratch.append(pltpu.VMEM((bm, bn), F32))
    out_specs = [pl.BlockSpec((bm, bn), lambda i, j, k: (i, j))]
    out_shape = [jax.ShapeDtypeStruct((m, n), out_dtype)]
    if cast_w:
        out_specs.append(pl.BlockSpec((bk, bn), lambda i, j, k: (k, j)))
        out_shape.append(jax.ShapeDtypeStruct((kdim, n), BF16))
    if ride:
        m2 = x2.shape[0]
        col = lambda i, j: jnp.where(i == 0, j, nj - 1)
        in_specs.append(pl.BlockSpec((m2, bk), lambda i, j, k: (0, jnp.where(i == 0, k, nk - 1))))
        args.append(x2)
        if has_res:
            in_specs.append(pl.BlockSpec((m2, bn), lambda i, j, k: (0, col(i, j))))
            args.append(res2)
        out_specs.append(pl.BlockSpec((m2, bn), lambda i, j, k: (0, col(i, j))))
        out_shape.append(jax.ShapeDtypeStruct((m2, n), out_dtype))
        if norm:
            scratch.append(pltpu.VMEM((m2, kdim), BF16))
        if nk > 1:
            scratch.append(pltpu.VMEM((m2, bn), F32))
    single = len(out_specs) == 1
    return pl.pallas_call(
        functools.partial(_mm_body, norm=norm, epilogue=epilogue, nk=nk, cast_w=cast_w, ride=ride),
        grid=(m // bm, nj, nk),
        in_specs=in_specs,
        out_specs=out_specs[0] if single else out_specs,
        out_shape=out_shape[0] if single else out_shape,
        scratch_shapes=scratch,
        compiler_params=_cparams(("arbitrary" if ride else "parallel", "arbitrary", "arbitrary")),
        name=name,
    )(*args)


def _out_proj_body(a_ref, b_ref, wa_ref, wb_ref, res_ref, x2_ref, res2_ref, o_ref, o2_ref):
    ka = a_ref.shape[1]
    acc = jnp.dot(a_ref[...], wa_ref[...], preferred_element_type=F32)
    acc = acc + jnp.dot(b_ref[...], wb_ref[...], preferred_element_type=F32)
    o_ref[...] = acc + res_ref[...]

    @pl.when(pl.program_id(0) == 0)
    def _():
        acc2 = jnp.dot(x2_ref[:, :ka], wa_ref[...], preferred_element_type=F32)
        acc2 = acc2 + jnp.dot(x2_ref[:, ka:], wb_ref[...], preferred_element_type=F32)
        o2_ref[...] = acc2 + res2_ref[...]


def _out_proj(a, b, w, res, x2, res2, *, bm=1024, bn=1024):
    m, ka = a.shape
    kb = b.shape[1]
    n = w.shape[1]
    m2 = x2.shape[0]
    bm = min(bm, m)
    nj = n // bn
    assert ka == kb and x2.shape[1] == ka + kb
    row_spec = lambda wid: pl.BlockSpec((bm, wid), lambda i, j: (i, 0))
    col2_spec = pl.BlockSpec((m2, bn), lambda i, j: (0, jnp.where(i == 0, j, nj - 1)))
    return pl.pallas_call(
        _out_proj_body,
        grid=(m // bm, nj),
        in_specs=[row_spec(ka), row_spec(kb),
                  pl.BlockSpec((ka, bn), lambda i, j: (0, j)), pl.BlockSpec((kb, bn), lambda i, j: (1, j)),
                  pl.BlockSpec((bm, bn), lambda i, j: (i, j)),
                  pl.BlockSpec((m2, ka + kb), lambda i, j: (0, 0)), col2_spec],
        out_specs=[pl.BlockSpec((bm, bn), lambda i, j: (i, j)), col2_spec],
        out_shape=[jax.ShapeDtypeStruct((m, n), F32), jax.ShapeDtypeStruct((m2, n), F32)],
        compiler_params=_cparams(("arbitrary", "arbitrary")),
        name="out_proj",
    )(a, b, w, w, res, x2, res2)


def _rmsnorm_body(x_ref, w_ref, o_ref):
    o_ref[...] = _rms_rows(x_ref[...], w_ref[...])


def _rmsnorm(x, w, *, bm=256):
    m, d = x.shape
    bm = min(bm, m)
    return pl.pallas_call(
        _rmsnorm_body,
        grid=(m // bm,),
        in_specs=[pl.BlockSpec((bm, d), lambda i: (i, 0)), pl.BlockSpec((1, d), lambda i: (0, 0))],
        out_specs=pl.BlockSpec((bm, d), lambda i: (i, 0)),
        out_shape=jax.ShapeDtypeStruct((m, d), F32),
        compiler_params=_cparams(("parallel",)),
        name="rmsnorm",
    )(x, w.reshape(1, d))


LOG2E = 1.0 / math.log(2.0)
ATT_SCALE2 = HEAD_DIM ** -0.5 * LOG2E


def _softplus2(z2):
    return jnp.maximum(z2, 0.0) + jnp.log(1.0 + jnp.exp2(-jnp.abs(z2))) * LOG2E


def _ride_along_specs(ws, n_steps, step_index):
    specs = [pl.BlockSpec((w.shape[0] // n_steps, w.shape[1]), lambda *ids: (step_index(*ids), 0)) for w in ws]
    for w in ws:
        assert w.shape[0] % (n_steps * 2 * SUBLANES) == 0
    return specs, [jax.ShapeDtypeStruct(w.shape, BF16) for w in ws]


def _ride_along_casts(src_refs, dst_refs):
    for src, dst in zip(src_refs, dst_refs):
        dst[...] = src[...].astype(BF16)


def _attn_prompt_body(bias_ref, q_ref, k_ref, v_ref, g_ref, *refs, tk, nsub, nhead, n_cast):
    cast_src, (o_ref, ko_ref, vo_ref), refs = refs[:n_cast], refs[n_cast:n_cast + 3], refs[n_cast + 3:]
    cast_dst, (kb_ref, vb_ref), state_refs = refs[:n_cast], refs[n_cast:n_cast + 2], refs[n_cast + 2:]
    _ride_along_casts(cast_src, cast_dst)
    acc_refs, carry_refs = state_refs[:nhead], state_refs[nhead:]
    hg = pl.program_id(1)
    qi = pl.program_id(2)

    @pl.when(qi == 0)
    def _():
        k = k_ref[...]
        v = v_ref[...]
        ko_ref[...] = k
        vo_ref[...] = v
        kb_ref[...] = k.astype(BF16)
        vb_ref[...] = v.astype(BF16)

    tq = tk * nsub
    row = lax.broadcasted_iota(jnp.int32, (tk, tk), 0)
    col = lax.broadcasted_iota(jnp.int32, (tk, tk), 1)
    later = (row > col).astype(BF16)
    lanes = lambda h: slice(h * HEAD_DIM, (h + 1) * HEAD_DIM)
    q = [(q_ref[:, lanes(h)] * ATT_SCALE2).astype(BF16) for h in range(nhead)]
    bias = [bias_ref[hg * nhead + h] * LOG2E for h in range(nhead)]

    for ref in state_refs:
        ref[...] = jnp.zeros_like(ref)

    def block(h, lo, kb, valid):
        start = pl.multiple_of(kb * tk, tk)
        kblk = kb_ref[pl.ds(start, tk), lanes(h)]
        vblk = vb_ref[pl.ds(start, tk), lanes(h)]
        z = lax.dot_general(q[h][lo:], kblk, (((1,), (1,)), ((), ())), preferred_element_type=F32) + bias[h]
        sp = _softplus2(z)
        if valid is not None:
            sp = jnp.where(valid, sp, 0.0)
        inner = jnp.dot(sp.astype(BF16), later, preferred_element_type=F32)
        carry = carry_refs[h][lo:, :]
        w = jnp.exp2(z - sp - inner - carry)
        if valid is not None:
            w = jnp.where(valid, w, 0.0)
        acc_refs[h][lo:, :] += jnp.dot(w.astype(BF16), vblk, preferred_element_type=F32)
        carry_refs[h][lo:, :] = carry + inner[:, :1] + sp[:, :1]

    for j in reversed(range(nsub)):
        m = tq - j * tk
        ri = lax.broadcasted_iota(jnp.int32, (m, tk), 0)
        ci = lax.broadcasted_iota(jnp.int32, (m, tk), 1)
        valid = (ri >= tk) | (ci < ri)
        for h in range(nhead):
            block(h, j * tk, nsub * qi + j, valid)

    @pl.loop(0, qi)
    def _(it):
        for j in range(nsub):
            for h in range(nhead):
                block(h, 0, nsub * (qi - it) - 1 - j, None)

    for h in range(nhead):
        o_ref[:, lanes(h)] = (acc_refs[h][...] * jax.nn.sigmoid(g_ref[:, lanes(h)])).astype(o_ref.dtype)


def _attn_prompt(proj, sb_bias, cast_ws=(), *, tk=256, nsub=4, nhead=2):
    b, t, _ = proj.shape
    tq = tk * nsub
    assert t % tq == 0 and N_HEADS % nhead == 0
    wid = nhead * HEAD_DIM
    ngrp = N_HEADS // nhead
    nq = t // tq
    qspec = lambda off: pl.BlockSpec((None, tq, wid), lambda bi, h, qi: (bi, qi, off + h))
    kvspec = lambda off: pl.BlockSpec((None, t, wid), lambda bi, h, qi: (bi, 0, off + h))
    cast_specs, cast_shapes = _ride_along_specs(cast_ws, b * ngrp * nq, lambda bi, h, qi: (bi * ngrp + h) * nq + qi)
    return pl.pallas_call(
        functools.partial(_attn_prompt_body, tk=tk, nsub=nsub, nhead=nhead, n_cast=len(cast_ws)),
        grid=(b, ngrp, nq),
        in_specs=[pl.BlockSpec(memory_space=pltpu.SMEM),
                  qspec(0), kvspec(ngrp), kvspec(2 * ngrp), qspec(3 * ngrp), *cast_specs],
        out_specs=[qspec(0), kvspec(0), kvspec(0), *cast_specs],
        out_shape=[jax.ShapeDtypeStruct((b, t, ATT_WIDTH), BF16),
                   jax.ShapeDtypeStruct((b, t, ATT_WIDTH), F32),
                   jax.ShapeDtypeStruct((b, t, ATT_WIDTH), F32), *cast_shapes],
        scratch_shapes=[pltpu.VMEM((t, wid), BF16), pltpu.VMEM((t, wid), BF16),
                        *[pltpu.VMEM((tq, HEAD_DIM), F32)] * nhead, *[pltpu.VMEM((tq, 1), F32)] * nhead],
        compiler_params=_cparams(("parallel", "parallel", "arbitrary")),
        name="attn_prompt",
    )(sb_bias, proj, proj, proj, proj, *cast_ws)


SLOTS = PAGE_SIZE * HEAD_HALVES
HALF_WIDTH = SUBLANES * HEAD_DIM


def _attn_sample_body(pt_ref, q_ref, kn_ref, vn_ref, g_ref, bias_ref, *refs, pages_per_step, n_tok):
    kp_refs = refs[:pages_per_step]
    vp_refs = refs[pages_per_step:2 * pages_per_step]
    o_ref = refs[2 * pages_per_step]
    qbd_ref, kcat_ref, vcat_ref, acc_ref, carry_ref = refs[2 * pages_per_step + 1:]
    c = pl.program_id(1)
    n_rows = n_tok * N_HEADS
    bias = bias_ref[...] * LOG2E
    row = lax.broadcasted_iota(jnp.int32, (SLOTS, SLOTS), 0)
    col = lax.broadcasted_iota(jnp.int32, (SLOTS, SLOTS), 1)
    later = ((row // HEAD_HALVES) > (col // HEAD_HALVES)).astype(BF16)

    def attend(npages, causal):
        n = npages * SLOTS
        z = lax.dot_general(qbd_ref[...], kcat_ref[:n, :], (((1,), (1,)), ((), ())),
                            preferred_element_type=F32) + bias
        rhalf = (lax.broadcasted_iota(jnp.int32, (n_rows, n), 0) % N_HEADS) // SUBLANES
        slot = lax.broadcasted_iota(jnp.int32, (n_rows, n), 1)
        valid = rhalf == slot % HEAD_HALVES
        if causal:
            valid = valid & (slot // HEAD_HALVES < lax.broadcasted_iota(jnp.int32, (n_rows, n), 0) // N_HEADS)
        sp = jnp.where(valid, _softplus2(z), 0.0)
        pages = lambda a: [a[:, i * SLOTS:(i + 1) * SLOTS] for i in range(npages)]
        stacked = jnp.concatenate(pages(sp), axis=0).astype(BF16)
        inner = jnp.dot(stacked, later, preferred_element_type=F32)
        inner = jnp.concatenate([inner[i * n_rows:(i + 1) * n_rows] for i in range(npages)], axis=1)
        carry = carry_ref[...]
        carries = []
        for i in range(npages):
            carries.append(jnp.broadcast_to(carry, (n_rows, SLOTS)))
            f = i * SLOTS
            carry = carry + inner[:, f:f + 1] + sp[:, f:f + 1] + sp[:, f + 1:f + 2]
        carry_ref[...] = carry
        w = jnp.where(valid, jnp.exp2(z - sp - inner - jnp.concatenate(carries, axis=1)), 0.0)
        acc_ref[...] += jnp.dot(w.astype(BF16), vcat_ref[:n, :], preferred_element_type=F32)

    @pl.when(c == 0)
    def _():
        q = q_ref[...] * ATT_SCALE2
        qt = jnp.concatenate([q] * SUBLANES, axis=1)
        rh = lax.broadcasted_iota(jnp.int32, (n_rows, HALF_WIDTH), 0) % SUBLANES
        lh = lax.broadcasted_iota(jnp.int32, (n_rows, HALF_WIDTH), 1) // HEAD_DIM
        qbd_ref[...] = jnp.where(rh == lh, qt, 0.0).astype(BF16)
        acc_ref[...] = jnp.zeros_like(acc_ref)
        carry_ref[...] = jnp.zeros_like(carry_ref)
        n_new = kn_ref.shape[0]
        kcat_ref[:n_new, :] = kn_ref[...].astype(BF16)
        vcat_ref[:n_new, :] = vn_ref[...].astype(BF16)
        kcat_ref[n_new:SLOTS, :] = jnp.zeros((SLOTS - n_new, HALF_WIDTH), BF16)
        vcat_ref[n_new:SLOTS, :] = jnp.zeros((SLOTS - n_new, HALF_WIDTH), BF16)
        attend(1, True)

    for i in range(pages_per_step):
        for h8 in range(SUBLANES):
            dst = (slice(i * SLOTS, (i + 1) * SLOTS), slice(h8 * HEAD_DIM, (h8 + 1) * HEAD_DIM))
            kcat_ref[dst] = kp_refs[i][pl.ds(h8, SLOTS, stride=SUBLANES), :].astype(BF16)
            vcat_ref[dst] = vp_refs[i][pl.ds(h8, SLOTS, stride=SUBLANES), :].astype(BF16)
    attend(pages_per_step, False)

    @pl.when(c == pl.num_programs(1) - 1)
    def _():
        rh = lax.broadcasted_iota(jnp.int32, (n_rows, HEAD_DIM), 0) % SUBLANES
        out = jnp.zeros((n_rows, HEAD_DIM), F32)
        for h8 in range(SUBLANES):
            out = out + jnp.where(rh == h8, acc_ref[:, h8 * HEAD_DIM:(h8 + 1) * HEAD_DIM], 0.0)
        o_ref[...] = out * jax.nn.sigmoid(g_ref[...])


def _attn_sample(proj_s, cache_k, cache_v, page_table, sb_bias, *, pages_per_step=8):
    b, n_tok, _ = proj_s.shape
    n_pages = page_table.shape[1]
    n_rows = n_tok * N_HEADS
    assert n_pages % pages_per_step == 0
    steps = n_pages // pages_per_step
    q_rows = proj_s[:, :, :ATT_WIDTH].reshape(b, n_rows, HEAD_DIM)
    g_rows = proj_s[:, :, 3 * ATT_WIDTH:4 * ATT_WIDTH].reshape(b, n_rows, HEAD_DIM)
    bias_col = jnp.tile(sb_bias, n_tok).reshape(n_rows, 1)
    n_new = 2 * SUBLANES
    assert n_tok * HEAD_HALVES <= n_new

    def new_slots(a):
        a = a.reshape(b, n_tok * HEAD_HALVES, HALF_WIDTH)
        return jnp.pad(a, ((0, 0), (0, n_new - n_tok * HEAD_HALVES), (0, 0)))

    k_new = new_slots(proj_s[:, :, ATT_WIDTH:2 * ATT_WIDTH])
    v_new = new_slots(proj_s[:, :, 2 * ATT_WIDTH:3 * ATT_WIDTH])

    def page_spec(i):
        def imap(bi, c, pt):
            return (pt[bi, n_pages - 1 - (c * pages_per_step + i)], 0, 0)
        return pl.BlockSpec((None, PAGE_SIZE * N_HEADS, HEAD_DIM), imap)

    page_specs = [page_spec(i) for i in range(pages_per_step)]
    rows_spec = pl.BlockSpec((None, n_rows, HEAD_DIM), lambda bi, c, pt: (bi, 0, 0))
    new_spec = pl.BlockSpec((None, n_new, HALF_WIDTH), lambda bi, c, pt: (bi, 0, 0))
    grid_spec = pltpu.PrefetchScalarGridSpec(
        num_scalar_prefetch=1,
        grid=(b, steps),
        in_specs=[rows_spec, new_spec, new_spec, rows_spec,
                  pl.BlockSpec((n_rows, 1), lambda bi, c, pt: (0, 0))]
                 + page_specs * 2,
        out_specs=rows_spec,
        scratch_shapes=[pltpu.VMEM((n_rows, HALF_WIDTH), BF16),
                        pltpu.VMEM((pages_per_step * SLOTS, HALF_WIDTH), BF16),
                        pltpu.VMEM((pages_per_step * SLOTS, HALF_WIDTH), BF16),
                        pltpu.VMEM((n_rows, HALF_WIDTH), F32),
                        pltpu.VMEM((n_rows, 1), F32)],
    )
    out = pl.pallas_call(
        functools.partial(_attn_sample_body, pages_per_step=pages_per_step, n_tok=n_tok),
        grid_spec=grid_spec,
        out_shape=jax.ShapeDtypeStruct((b, n_rows, HEAD_DIM), F32),
        compiler_params=_cparams(("parallel", "arbitrary")),
        name="attn_sample",
    )(page_table, q_rows, k_new, v_new, g_rows, bias_col,
      *([cache_k] * pages_per_step), *([cache_v] * pages_per_step))
    return out.reshape(b, n_tok, ATT_WIDTH)


def _ssm_params_body(are_ref, aim_ref, ldt_ref, bre_ref, bim_ref, abr_ref, abi_ref, bbr_ref, bbi_ref):
    a_re = are_ref[...]
    a_im = aim_ref[...]
    dt = jnp.exp(ldt_ref[...])
    mag = jnp.exp(dt * a_re)
    ab_re = mag * jnp.cos(dt * a_im)
    ab_im = mag * jnp.sin(dt * a_im)
    abr_ref[...] = ab_re
    abi_ref[...] = ab_im
    den = a_re * a_re + a_im * a_im
    n_re = ab_re - 1.0
    co_re = (n_re * a_re + ab_im * a_im) / den
    co_im = (ab_im * a_re - n_re * a_im) / den
    b_re = bre_ref[...]
    b_im = bim_ref[...]
    bbr_ref[...] = co_re[:, None, :] * b_re - co_im[:, None, :] * b_im
    bbi_ref[...] = co_re[:, None, :] * b_im + co_im[:, None, :] * b_re


def _ssm_params(a_re, a_im, log_dt, b_re, b_im):
    g, p = a_re.shape
    c = b_re.shape[1]
    return pl.pallas_call(
        _ssm_params_body,
        name="ssm_params",
        out_shape=[jax.ShapeDtypeStruct((g, p), F32), jax.ShapeDtypeStruct((g, p), F32),
                   jax.ShapeDtypeStruct((g, c, p), F32), jax.ShapeDtypeStruct((g, c, p), F32)],
    )(a_re, a_im, log_dt.reshape(g, 1), b_re, b_im)


def _gelu_tanh(y):
    return 0.5 * y * (1.0 + jnp.tanh(math.sqrt(2.0 / math.pi) * (y + 0.044715 * (y * y * y))))


def _ssm_prompt_body(u_ref, g_ref, b_ref, c_ref, gw_ref, ab_ref, dg_ref, *refs, lc, nb, nsc, n_cast):
    cast_src, (o_ref, hT_ref), refs = refs[:n_cast], refs[n_cast:n_cast + 2], refs[n_cast + 2:]
    cast_dst, (st_ref, tm_ref), sub_refs = refs[:n_cast], refs[n_cast:n_cast + 2], refs[n_cast + 2:]
    _ride_along_casts(cast_src, cast_dst)
    ci = pl.program_id(1)
    rows = lc * SUBLANES
    n_slab = STATE_LANES // LANES
    seqs = [(sel, b) for sel in range(2) for b in range(nb)]
    seq_rows = lambda i: pl.ds(i, lc, stride=SUBLANES)
    for i, (sel, b) in enumerate(seqs):
        tm_ref.at[0][seq_rows(i), :] = u_ref[b, :, sel * LANES:(sel + 1) * LANES]
        tm_ref.at[1][seq_rows(i), :] = g_ref[b, :, sel * LANES:(sel + 1) * LANES]

    @pl.when(ci == 0)
    def _():
        st_ref[...] = jnp.zeros_like(st_ref)

    ar = [ab_ref[j] for j in range(n_slab)]
    ai = [ab_ref[n_slab + j] for j in range(n_slab)]
    h = [st_ref[j] for j in range(2 * n_slab)]
    sub = rows // nsc
    first = (lax.broadcasted_iota(jnp.int32, (sub, LANES), 0) % SUBLANES) < (SUBLANES // 2)
    d = jnp.broadcast_to(dg_ref[0][None], (sub // SUBLANES, SUBLANES, LANES)).reshape(sub, LANES)
    gb = jnp.broadcast_to(dg_ref[1][None], (sub // SUBLANES, SUBLANES, LANES)).reshape(sub, LANES)
    bu_refs, hh_refs = sub_refs[:nsc], sub_refs[nsc:]

    def drive(sc):
        u = tm_ref[0, sc * sub:(sc + 1) * sub, :]
        lhs = jnp.concatenate([jnp.where(first, u, 0.0), jnp.where(first, 0.0, u)], axis=1).astype(BF16)
        bu_refs[sc][...] = jnp.dot(lhs, b_ref[...], preferred_element_type=F32)

    drive(0)
    for sc in range(nsc):
        lo = sc * sub
        if sc + 1 < nsc:
            drive(sc + 1)
        bu_ref, hh_ref = bu_refs[sc], hh_refs[sc]
        for r0 in range(0, sub, SUBLANES):
            for j in range(n_slab):
                re_l = slice(j * LANES, (j + 1) * LANES)
                im_l = slice(STATE_LANES + j * LANES, STATE_LANES + (j + 1) * LANES)
                hr, hi = h[j], h[n_slab + j]
                h[j] = ar[j] * hr - ai[j] * hi + bu_ref[r0:r0 + SUBLANES, re_l]
                h[n_slab + j] = ar[j] * hi + ai[j] * hr + bu_ref[r0:r0 + SUBLANES, im_l]
                hh_ref[r0:r0 + SUBLANES, re_l] = h[j]
                hh_ref[r0:r0 + SUBLANES, im_l] = h[n_slab + j]
        u = tm_ref[0, lo:lo + sub, :]
        y2 = lax.dot_general(hh_ref[...].astype(BF16), c_ref[...], (((1,), (1,)), ((), ())),
                             preferred_element_type=F32)
        y = jnp.where(first, y2[:, :LANES], y2[:, LANES:]) + d * u
        z = _gelu_tanh(y)
        g2 = jnp.dot(z.astype(BF16), gw_ref[...], preferred_element_type=F32)
        gate = jax.nn.sigmoid(jnp.where(first, g2[:, :LANES], g2[:, LANES:]) + gb)
        tm_ref[2, lo:lo + sub, :] = z * gate * jax.nn.sigmoid(tm_ref[1, lo:lo + sub, :])
    for j in range(2 * n_slab):
        st_ref[j] = h[j]
        hT_ref[j] = h[j]
    for i, (sel, b) in enumerate(seqs):
        o_ref[b, :, sel * LANES:(sel + 1) * LANES] = tm_ref.at[2][seq_rows(i), :].astype(o_ref.dtype)


def _ssm_prompt(proj, b_st, c_st, gw_st, ab_tm, dg_tm, cast_ws=(), *, lc=256, nsc=4):
    nb, t, _ = proj.shape
    assert t % lc == 0 and (lc * SUBLANES) % (nsc * SUBLANES) == 0
    npair = N_GBLOCKS // 2
    n_slab2 = 2 * STATE_LANES // LANES
    wid = 2 * LANES
    nchunk = t // lc
    seq_spec = lambda off: pl.BlockSpec((nb, lc, wid), lambda p, c: (0, c, off + p))
    u_off = 4 * ATT_WIDTH // wid
    par_spec = lambda shp: pl.BlockSpec((None,) + shp, lambda p, c: (p,) + (0,) * len(shp))
    cast_specs, cast_shapes = _ride_along_specs(cast_ws, npair * nchunk, lambda p, c: p * nchunk + c)
    return pl.pallas_call(
        functools.partial(_ssm_prompt_body, lc=lc, nb=nb, nsc=nsc, n_cast=len(cast_ws)),
        grid=(npair, nchunk),
        in_specs=[seq_spec(u_off), seq_spec(u_off + npair),
                  par_spec((2 * LANES, 2 * STATE_LANES)), par_spec((2 * LANES, 2 * STATE_LANES)),
                  par_spec((LANES, 2 * LANES)), par_spec((n_slab2, SUBLANES, LANES)),
                  par_spec((2, SUBLANES, LANES)), *cast_specs],
        out_specs=[seq_spec(0), par_spec((n_slab2, SUBLANES, LANES)), *cast_specs],
        out_shape=[jax.ShapeDtypeStruct((nb, t, SSM_WIDTH), BF16),
                   jax.ShapeDtypeStruct((npair, n_slab2, SUBLANES, LANES), F32), *cast_shapes],
        scratch_shapes=[pltpu.VMEM((n_slab2, SUBLANES, LANES), F32),
                        pltpu.VMEM((3, lc * SUBLANES, LANES), F32)]
                       + [pltpu.VMEM((lc * SUBLANES // nsc, 2 * STATE_LANES), F32)] * (2 * nsc),
        compiler_params=_cparams(("parallel", "arbitrary")),
        name="ssm_prompt",
    )(proj, proj, b_st, c_st, gw_st, ab_tm, dg_tm, *cast_ws)


def _ssm_sample_body(u_ref, g_ref, hre_ref, him_ref, b_ref, c_ref, gw_ref, ab_ref, dg_ref,
                     o_ref, ore_ref, oim_ref, *, n_tok, nb):
    hp = lax.Precision.HIGHEST
    u = u_ref[...].reshape(n_tok * nb, LANES)
    bu = jnp.dot(u, b_ref[...], preferred_element_type=F32, precision=hp)
    ar = ab_ref[0:1, :]
    ai = ab_ref[1:2, :]
    hr = hre_ref[...]
    hi = him_ref[...]
    hs = []
    for t in range(n_tok):
        bre = bu[t * nb:(t + 1) * nb, :STATE_LANES]
        bim = bu[t * nb:(t + 1) * nb, STATE_LANES:]
        hr, hi = ar * hr - ai * hi + bre, ar * hi + ai * hr + bim
        hs.append(jnp.concatenate([hr, hi], axis=1))
    ore_ref[...] = hr
    oim_ref[...] = hi
    hh = jnp.concatenate(hs, axis=0)
    y = lax.dot_general(hh, c_ref[...], (((1,), (1,)), ((), ())), preferred_element_type=F32, precision=hp)
    y = y + dg_ref[0:1, :] * u
    z = _gelu_tanh(y)
    gate = jax.nn.sigmoid(jnp.dot(z, gw_ref[...], preferred_element_type=F32, precision=hp) + dg_ref[1:2, :])
    out = z * gate * jax.nn.sigmoid(g_ref[...].reshape(n_tok * nb, LANES))
    o_ref[...] = out.reshape(n_tok, nb, LANES)


def _ssm_sample(u_t, g_t, h_re, h_im, b_blk, c_blk, gw_blk, ab_row, dg_row):
    n_tok, nb, _ = u_t.shape
    seq_spec = pl.BlockSpec((n_tok, nb, LANES), lambda gb: (0, 0, gb))
    st_spec = pl.BlockSpec((nb, STATE_LANES), lambda gb: (0, gb))
    par_spec = lambda shp: pl.BlockSpec((None,) + shp, lambda gb: (gb,) + (0,) * len(shp))
    return pl.pallas_call(
        functools.partial(_ssm_sample_body, n_tok=n_tok, nb=nb),
        grid=(N_GBLOCKS,),
        in_specs=[seq_spec, seq_spec, st_spec, st_spec,
                  par_spec((LANES, 2 * STATE_LANES)), par_spec((LANES, 2 * STATE_LANES)),
                  par_spec((LANES, LANES)), par_spec((2, STATE_LANES)), par_spec((2, LANES))],
        out_specs=[seq_spec, st_spec, st_spec],
        out_shape=[jax.ShapeDtypeStruct(u_t.shape, F32),
                   jax.ShapeDtypeStruct(h_re.shape, F32), jax.ShapeDtypeStruct(h_im.shape, F32)],
        compiler_params=_cparams(("parallel",)),
        name="ssm_sample",
    )(u_t, g_t, h_re, h_im, b_blk, c_blk, gw_blk, ab_row, dg_row)


def _expand_body(re_ref, im_ref, o32_ref, o16_ref, *, im_sign):
    rows = re_ref.shape[0]
    src = lax.broadcasted_iota(jnp.int32, (STATE_DIM, STATE_LANES), 0)
    dst = lax.broadcasted_iota(jnp.int32, (STATE_DIM, STATE_LANES), 1)
    spread = (src == dst % STATE_DIM).astype(F32)
    own = (lax.broadcasted_iota(jnp.int32, (rows, STATE_LANES), 0) // SSM_GROUP
           == lax.broadcasted_iota(jnp.int32, (rows, STATE_LANES), 1) // STATE_DIM)
    halves = []
    for ref, sign in ((re_ref, 1.0), (im_ref, im_sign)):
        full = jnp.dot(ref[...], spread, preferred_element_type=F32, precision=lax.Precision.HIGHEST)
        halves.append(jnp.where(own, sign * full, 0.0))
    out = jnp.concatenate(halves, axis=1)
    o32_ref[...] = out
    o16_ref[...] = out.astype(BF16)


def _expand(x_re, x_im, *, im_sign):
    g, c, p = x_re.shape
    in_spec = pl.BlockSpec((LANES, p), lambda i: (i, 0))
    out_spec = pl.BlockSpec((None, LANES, 2 * STATE_LANES), lambda i: (i, 0, 0))
    shape = (N_GBLOCKS, LANES, 2 * STATE_LANES)
    return pl.pallas_call(
        functools.partial(_expand_body, im_sign=im_sign),
        grid=(N_GBLOCKS,),
        in_specs=[in_spec, in_spec],
        out_specs=[out_spec, out_spec],
        out_shape=[jax.ShapeDtypeStruct(shape, F32), jax.ShapeDtypeStruct(shape, BF16)],
        compiler_params=_cparams(("parallel",)),
        name="ssm_expand",
    )(x_re.reshape(g * c, p), x_im.reshape(g * c, p))


def _ssm_block_weights(ab_re, ab_im, bb_re, bb_im, c_re, c_im, d, glu_w, glu_b):
    nb, gpb = N_GBLOCKS, GROUPS_PER_BLOCK
    eye = jnp.eye(gpb, dtype=F32)
    b_blk, b_blk16 = _expand(bb_re, bb_im, im_sign=1.0)
    c_blk, c_blk16 = _expand(c_re, c_im, im_sign=-1.0)
    gw = glu_w.reshape(nb, gpb, SSM_GROUP, SSM_GROUP)
    gw_blk = jnp.einsum('bgce,gh->bgche', gw, eye).reshape(nb, LANES, LANES)
    ab_row = jnp.stack([ab_re.reshape(nb, STATE_LANES), ab_im.reshape(nb, STATE_LANES)], axis=1)
    dg_row = jnp.stack([d.reshape(nb, LANES), glu_b.reshape(nb, LANES)], axis=1)
    return (b_blk, c_blk, gw_blk, ab_row, dg_row), (b_blk16, c_blk16)


def _pair_weights(b_blk16, c_blk16, gw_blk, ab_row, dg_row, nbatch):
    npair = N_GBLOCKS // 2
    n_slab = STATE_LANES // LANES
    b_st = b_blk16.reshape(npair, 2 * LANES, 2 * STATE_LANES)
    c_st = c_blk16.reshape(npair, 2 * LANES, 2 * STATE_LANES)
    gw_st = gw_blk.reshape(npair, 2, LANES, LANES).transpose(0, 2, 1, 3).reshape(
        npair, LANES, 2 * LANES).astype(BF16)
    ab = ab_row.reshape(npair, 2, 2, n_slab, LANES).transpose(0, 2, 3, 1, 4)
    ab_tm = jnp.repeat(ab, nbatch, axis=3).reshape(npair, 2 * n_slab, 2 * nbatch, LANES)
    dg = dg_row.reshape(npair, 2, 2, LANES).transpose(0, 2, 1, 3)
    dg_tm = jnp.repeat(dg, nbatch, axis=2)
    return b_st, c_st, gw_st, ab_tm, dg_tm


def _dense_tail(xp, ssm_p, att_p, xs, mixed_s, w_out, norm_mlp_w, w_up, w_down, norm_final_w):
    x1_p, x1_s = _out_proj(ssm_p, att_p, w_out, xp, mixed_s, xs)
    hid_p = _matmul(x1_p, w_up, name="mlp_up", bm=512, bn=1024, bk=D_MODEL, norm_w=norm_mlp_w, epilogue="relu2",
                    out_dtype=BF16)
    hid_s = _matmul(x1_s, w_up, name="mlp_up_s", bm=LANES, bn=1024, bk=D_MODEL, norm_w=norm_mlp_w, epilogue="relu2",
                    out_dtype=BF16)
    x2_p = _matmul(hid_p, w_down, name="mlp_down", bm=1024, bn=1024, bk=4096, res=x1_p, epilogue="res")
    x2_s = _matmul(hid_s, w_down, name="mlp_down_s", bm=LANES, bn=1024, bk=4096, res=x1_s, epilogue="res")
    return _rmsnorm(x2_p, norm_final_w), _rmsnorm(x2_s, norm_final_w)


def kernel(x_prompt, x_sample, cache_k, cache_v, state_ssm_re, state_ssm_im, page_table, norm_mix_w, w_in, sb_bias,
           ssm_a_re, ssm_a_im, ssm_b_re, ssm_b_im, ssm_c_re, ssm_c_im, ssm_d, ssm_log_dt, glu_w, glu_b, w_out,
           norm_mlp_w, w_up, w_down, norm_final_w):
    depth = w_in.shape[0]
    assert depth == 1
    nb_p, t_p, _ = x_prompt.shape
    nb_s, t_s, _ = x_sample.shape
    assert nb_p * 2 == SUBLANES
    assert nb_s * t_s == LANES
    l = 0

    ab_re, ab_im, bb_re, bb_im = _ssm_params(ssm_a_re[l], ssm_a_im[l], ssm_log_dt[l],
                                             ssm_b_re[l].transpose(0, 2, 1), ssm_b_im[l].transpose(0, 2, 1))
    blk, blk16 = _ssm_block_weights(ab_re, ab_im, bb_re, bb_im, ssm_c_re[l], ssm_c_im[l], ssm_d[l], glu_w[l],
                                    glu_b[l])
    pair = _pair_weights(*blk16, *blk[2:], nbatch=nb_p)

    xs = x_sample.reshape(nb_s * t_s, D_MODEL)
    proj_s, w_in_b = _matmul(xs, w_in[l], name="in_proj_s", bm=LANES, bn=512, bk=D_MODEL, norm_w=norm_mix_w[l],
                             cast_w=True)
    proj_s3 = proj_s.reshape(nb_s, t_s, IN_WIDTH)
    n_phys = cache_k.shape[1]
    pool = (depth * n_phys, PAGE_SIZE * N_HEADS, HEAD_DIM)
    att_s = _attn_sample(proj_s3, cache_k.reshape(pool), cache_v.reshape(pool), page_table + l * n_phys,
                         sb_bias[l])
    u_t = proj_s3[:, :, 4 * ATT_WIDTH:4 * ATT_WIDTH + SSM_WIDTH].transpose(1, 0, 2)
    g_t = proj_s3[:, :, 4 * ATT_WIDTH + SSM_WIDTH:].transpose(1, 0, 2)
    ssm_t, hs_re, hs_im = _ssm_sample(u_t, g_t, state_ssm_re[l].reshape(nb_s, N_GROUPS * STATE_DIM),
                                      state_ssm_im[l].reshape(nb_s, N_GROUPS * STATE_DIM), *blk)
    ssm_s = ssm_t.transpose(1, 0, 2).reshape(nb_s * t_s, SSM_WIDTH)
    mixed_s = jnp.concatenate([ssm_s, att_s.reshape(nb_s * t_s, ATT_WIDTH)], axis=1).astype(BF16)

    xp = x_prompt.reshape(nb_p * t_p, D_MODEL)
    proj_p = _matmul(xp, w_in_b, name="in_proj", bm=512, bn=1024, bk=D_MODEL, norm_w=norm_mix_w[l])
    proj_p3 = proj_p.reshape(nb_p, t_p, IN_WIDTH)
    att_p, k_p, v_p, w_out_b, w_up_b = _attn_prompt(proj_p3, sb_bias[l], (w_out[l], w_up[l]))
    npair = N_GBLOCKS // 2
    ssm_p, hT, w_down_b = _ssm_prompt(proj_p3, *pair, (w_down[l],))
    n_slab = STATE_LANES // LANES
    hT = hT.reshape(npair, 2, n_slab, 2, nb_p, 2, STATE_DIM).transpose(1, 4, 0, 3, 2, 5, 6).reshape(
        2, nb_p, N_GROUPS, STATE_DIM)
    y_p, y_s = _dense_tail(xp, ssm_p.reshape(nb_p * t_p, SSM_WIDTH), att_p.reshape(nb_p * t_p, ATT_WIDTH),
                           xs, mixed_s, w_out_b, norm_mlp_w[l], w_up_b, w_down_b, norm_final_w)

    kv_shape_p = (1, nb_p, t_p, N_HEADS, HEAD_DIM)
    kv_shape_s = (1, nb_s, t_s, N_HEADS, HEAD_DIM)
    return (y_p.reshape(nb_p, t_p, D_MODEL), y_s.reshape(nb_s, t_s, D_MODEL),
            k_p.reshape(kv_shape_p), v_p.reshape(kv_shape_p),
            hT[0][None], hT[1][None],
            proj_s3[:, :, ATT_WIDTH:2 * ATT_WIDTH].reshape(kv_shape_s),
            proj_s3[:, :, 2 * ATT_WIDTH:3 * ATT_WIDTH].reshape(kv_shape_s),
            hs_re.reshape(1, nb_s, N_GROUPS, STATE_DIM), hs_im.reshape(1, nb_s, N_GROUPS, STATE_DIM))
```
